```python
import math
import jax
import jax.numpy as jnp
from jax import lax
import numpy as np

D_MODEL = 1024
BATCH = 8
SEQ = 2048
DEPTH = 4
DEC_BATCH = 32
DEC_SEQ = 4
PAST_LEN = 16384
PAGE_SIZE = 128

F32 = jnp.float32
EPS = 1e-6
N_EVEN = (DEPTH + 1) // 2
N_ODD = DEPTH // 2

A_HEADS = 8
A_HEAD_DIM = 64
A_WIDTH = A_HEADS * A_HEAD_DIM
MOBA_BLOCK = 256
MOBA_TOPK = 3
MOBA_QBLOCK = 32
B_HEADS = 4
B_HEAD_DIM = 128
B_WIDTH = B_HEADS * B_HEAD_DIM
CONV_W = 4
MLSTM_CHUNK = 64
IN_AB = 3 * A_WIDTH + 4 * B_WIDTH + 2 * B_HEADS
SPLIT_AB = (A_WIDTH, 2 * A_WIDTH, 3 * A_WIDTH, 3 * A_WIDTH + 2 * B_WIDTH, 3 * A_WIDTH + 3 * B_WIDTH, 3 * A_WIDTH + 4 * B_WIDTH)
MIX_AB = A_WIDTH + B_WIDTH
C_HEADS = 16
Q_LORA = 256
KV_LORA = 128
NOPE_DIM = 64
ROPE_DIM = 32
V_DIM = 64
IN_C = Q_LORA + KV_LORA + ROPE_DIM
ROPE_THETA = 10000.0
MLA_QBLOCK = 128
N_MEM = 256
X_HEADS = 4
X_HEAD_DIM = 128
X_WIDTH = X_HEADS * X_HEAD_DIM
N_GROUPS = 4
EXPERTS_PER_GROUP = 4
N_EXPERTS = N_GROUPS * EXPERTS_PER_GROUP
TOPK_IN_GROUP = 2
D_EXPERT = 256

kernel_name = 'moba_mlstm_mla_hier_moe_decoder_step'


def rmsnorm(x, g):
    xf = x.astype(F32)
    y = xf * lax.rsqrt(jnp.mean(xf * xf, axis=-1, keepdims=True) + EPS)
    return (y * g.astype(F32)).astype(x.dtype)


def alibi_slopes(n):
    return jnp.exp2(-8.0 * jnp.arange(1, n + 1, dtype=F32) / n)


def rope(x, pos):
    half = x.shape[-1] // 2
    freqs = ROPE_THETA ** (-jnp.arange(half, dtype=F32) / half)
    ang = pos.astype(F32)[:, None] * freqs
    cos, sin = jnp.cos(ang)[:, None, :], jnp.sin(ang)[:, None, :]
    x1, x2 = x[..., :half].astype(F32), x[..., half:].astype(F32)
    return jnp.concatenate([x1 * cos - x2 * sin, x1 * sin + x2 * cos], axis=-1).astype(x.dtype)


def gather_pages(pool, j, page_table):
    g = pool[j, page_table]
    return g.reshape((page_table.shape[0], -1) + pool.shape[3:])


def causal_conv(x, w, buf):
    L = x.shape[1]
    xx = jnp.concatenate([buf.astype(x.dtype), x], axis=1)
    y = sum(xx[:, t:t + L] * w[t] for t in range(CONV_W))
    return y, xx[:, L:]


def moba_attention(q, k, v, q_pos):
    B, L, H, dh = q.shape
    T = k.shape[1]
    nb = -(-T // MOBA_BLOCK)
    pad = nb * MOBA_BLOCK - T

    def to_blocks(a):
        a = jnp.pad(a, ((0, 0), (0, pad), (0, 0), (0, 0)))
        return a.reshape(B, nb, MOBA_BLOCK, H, dh).transpose(0, 3, 1, 2, 4)

    kb, vb = to_blocks(k), to_blocks(v)
    kmean = jnp.mean(kb.astype(F32), axis=3)
    n_sel = min(MOBA_TOPK, nb)
    slopes = alibi_slopes(H)
    scale = dh ** -0.5
    bi = jnp.arange(B)[:, None, None, None]
    hi = jnp.arange(H)[None, :, None, None]
    blk_off = jnp.arange(MOBA_BLOCK, dtype=jnp.int32)

    def one_block(args):
        qb, pos = args
        nq = pos.shape[0]
        qh = qb.transpose(0, 2, 1, 3).astype(F32)
        own = pos // MOBA_BLOCK
        gate = jnp.einsum('bhqd,bhnd->bhqn', qh, kmean)
        gate = jnp.where(jnp.arange(nb)[None, :] < own[:, None], gate, -jnp.inf)
        _, sel = lax.top_k(gate, n_sel)
        idx = jnp.concatenate([sel.astype(jnp.int32), jnp.broadcast_to(own[None, None, :, None], (B, H, nq, 1)).astype(jnp.int32)], axis=-1)
        blk_ok = jnp.concatenate([jnp.arange(n_sel)[None, :] < own[:, None], jnp.ones((nq, 1), bool)], axis=-1)
        kg = kb[bi, hi, idx].astype(F32)
        vg = vb[bi, hi, idx].astype(F32)
        kpos = idx[..., None] * MOBA_BLOCK + blk_off
        dist = pos[None, None, :, None, None] - kpos
        s = jnp.einsum('bhqd,bhqsjd->bhqsj', qh, kg) * scale - slopes[None, :, None, None, None] * dist.astype(F32)
        ok = blk_ok[None, None, :, :, None] & (dist >= 0)
        s = jnp.where(ok, s, -jnp.inf)
        p = jax.nn.softmax(s.reshape(B, H, nq, -1), axis=-1).reshape(s.shape)
        o = jnp.einsum('bhqsj,bhqsjd->bqhd', p, vg)
        return o.reshape(B, nq, H * dh).astype(q.dtype)

    qblk = math.gcd(L, MOBA_QBLOCK)
    nqb = L // qblk
    qs = q.reshape(B, nqb, qblk, H, dh).transpose(1, 0, 2, 3, 4)
    ps = q_pos.reshape(nqb, qblk)
    o = lax.map(one_block, (qs, ps))
    return o.transpose(1, 0, 2, 3).reshape(B, L, H * dh)


def mlstm_chunk(carry, inp):
    C, n, m = carry
    q, k, v, ig, lf = inp
    L = q.shape[2]
    b = jnp.cumsum(lf, axis=-1)
    causal = jnp.tril(jnp.ones((L, L), bool))
    logD = jnp.where(causal, b[..., :, None] - b[..., None, :] + ig[..., None, :], -jnp.inf)
    inter = b + m[..., None]
    mt = jnp.maximum(inter, jnp.max(logD, axis=-1))
    D = jnp.exp(logD - mt[..., None])
    w_inter = jnp.exp(inter - mt)
    A = jnp.einsum('bhtd,bhsd->bhts', q, k) * D
    num = jnp.einsum('bhts,bhsv->bhtv', A, v) + w_inter[..., None] * jnp.einsum('bhtd,bhdv->bhtv', q, C)
    den = jnp.sum(A, axis=-1) + w_inter * jnp.einsum('bhtd,bhd->bht', q, n)
    h = num / jnp.maximum(jnp.abs(den), jnp.exp(-mt))[..., None]
    bL = b[..., -1]
    logw = bL[..., None] - b + ig
    m_new = jnp.maximum(bL + m, jnp.max(logw, axis=-1))
    ws = jnp.exp(logw - m_new[..., None])
    decay = jnp.exp(bL + m - m_new)
    C_new = decay[..., None, None] * C + jnp.einsum('bhs,bhsd,bhsv->bhdv', ws, k, v)
    n_new = decay[..., None] * n + jnp.einsum('bhs,bhsd->bhd', ws, k)
    return (C_new, n_new, m_new), h


def mlstm(q, k, v, ig, lf, state):
    L = q.shape[2]
    ch = math.gcd(L, MLSTM_CHUNK)
    nc = L // ch

    def chunks(a):
        return jnp.moveaxis(a.reshape(a.shape[:2] + (nc, ch) + a.shape[3:]), 2, 0)

    state, h = lax.scan(mlstm_chunk, state, (chunks(q), chunks(k), chunks(v), chunks(ig), chunks(lf)))
    h = jnp.moveaxis(h, 0, 2)
    return h.reshape(h.shape[:2] + (L, h.shape[-1])), state


def even_mixer(xn, w_in, b_g, cw, g_h, w_out, k_past, v_past, conv_buf, mstate, q_pos):
    B, L, _ = xn.shape
    q_a, k_a, v_a, qk_b, v_b, o_b, gates = jnp.split(xn @ w_in, SPLIT_AB, axis=-1)
    ka = k_a.reshape(B, L, A_HEADS, A_HEAD_DIM)
    va = v_a.reshape(B, L, A_HEADS, A_HEAD_DIM)
    k_all = jnp.concatenate([k_past.astype(ka.dtype), ka], axis=1)
    v_all = jnp.concatenate([v_past.astype(va.dtype), va], axis=1)
    o_a = moba_attention(q_a.reshape(B, L, A_HEADS, A_HEAD_DIM), k_all, v_all, q_pos)
    qk_c, conv_new = causal_conv(qk_b, cw, conv_buf)
    qk_c = jax.nn.silu(qk_c)

    def heads(a):
        return a.reshape(B, L, B_HEADS, B_HEAD_DIM).transpose(0, 2, 1, 3).astype(F32)

    qb = heads(qk_c[..., :B_WIDTH])
    kb = heads(qk_c[..., B_WIDTH:]) * (B_HEAD_DIM ** -0.5)
    vb = heads(v_b)
    g = (gates + b_g).astype(F32).transpose(0, 2, 1)
    ig, lf = g[:, :B_HEADS], jax.nn.log_sigmoid(g[:, B_HEADS:])
    hb, mstate = mlstm(qb, kb, vb, ig, lf, mstate)
    hb = rmsnorm(hb.transpose(0, 2, 1, 3).astype(xn.dtype), g_h.reshape(B_HEADS, B_HEAD_DIM))
    o_bb = hb.reshape(B, L, B_WIDTH) * jax.nn.sigmoid(o_b)
    out = jnp.concatenate([o_a, o_bb], axis=-1) @ w_out
    return out, ka, va, conv_new, mstate


def mla_attention(q_lat, q_pe, c_all, pe_all, q_pos):
    B, L, H, _ = q_lat.shape
    T = c_all.shape[1]
    kpos = jnp.arange(T, dtype=jnp.int32)
    scale = (NOPE_DIM + ROPE_DIM) ** -0.5
    c32, pe32 = c_all.astype(F32), pe_all.astype(F32)

    def one_block(args):
        ql, qp, pos = args
        s = (jnp.einsum('bqhc,bkc->bhqk', ql.astype(F32), c32) + jnp.einsum('bqhr,bkr->bhqk', qp.astype(F32), pe32)) * scale
        s = jnp.where(kpos[None, None, None, :] <= pos[None, None, :, None], s, -jnp.inf)
        p = jax.nn.softmax(s, axis=-1)
        return jnp.einsum('bhqk,bkc->bqhc', p, c32)

    qblk = math.gcd(L, MLA_QBLOCK)
    nqb = L // qblk
    ql = q_lat.reshape(B, nqb, qblk, H, KV_LORA).transpose(1, 0, 2, 3, 4)
    qp = q_pe.reshape(B, nqb, qblk, H, ROPE_DIM).transpose(1, 0, 2, 3, 4)
    o = lax.map(one_block, (ql, qp, q_pos.reshape(nqb, qblk)))
    return o.transpose(1, 0, 2, 3, 4).reshape(B, L, H, KV_LORA)


def odd_mixer(xn, w_in, g_q, g_kv, w_qb, w_kb, w_vb, w_out, kv_past, pe_past, q_pos):
    B, L, _ = xn.shape
    cq, ckv, kpe = jnp.split(xn @ w_in, (Q_LORA, Q_LORA + KV_LORA), axis=-1)
    cq = rmsnorm(cq, g_q)
    ckv = rmsnorm(ckv, g_kv)
    q = (cq @ w_qb).reshape(B, L, C_HEADS, NOPE_DIM + ROPE_DIM)
    q_nope = q[..., :NOPE_DIM]
    q_pe = rope(q[..., NOPE_DIM:], q_pos)
    kpe = rope(kpe[:, :, None, :], q_pos)[:, :, 0]
    q_lat = jnp.einsum('blhn,chn->blhc', q_nope, w_kb)
    c_all = jnp.concatenate([kv_past.astype(ckv.dtype), ckv], axis=1)
    pe_all = jnp.concatenate([pe_past.astype(kpe.dtype), kpe], axis=1)
    o_lat = mla_attention(q_lat, q_pe, c_all, pe_all, q_pos)
    o = jnp.einsum('blhc,chv->blhv', o_lat.astype(xn.dtype), w_vb).reshape(B, L, C_HEADS * V_DIM)
    return o @ w_out, ckv, kpe


def mem_kv(mem, g, w_ck, w_cv):
    B, M, _ = mem.shape
    mn = rmsnorm(mem, g)
    return (mn @ w_ck).reshape(B, M, X_HEADS, X_HEAD_DIM), (mn @ w_cv).reshape(B, M, X_HEADS, X_HEAD_DIM)


def cross_attention(xn, mk, mv, w_cq, w_co):
    B, L, _ = xn.shape
    q = (xn @ w_cq).reshape(B, L, X_HEADS, X_HEAD_DIM)
    s = jnp.einsum('blhd,bmhd->bhlm', q.astype(F32), mk.astype(F32)) * (X_HEAD_DIM ** -0.5)
    p = jax.nn.softmax(s, axis=-1)
    o = jnp.einsum('bhlm,bmhd->blhd', p, mv.astype(F32)).astype(xn.dtype).reshape(B, L, X_WIDTH)
    return o @ w_co


def hier_moe(xn, w_group, b_group, w_router, b_router, w_up, w_gate, w_down):
    B, L, D = xn.shape
    T = B * L
    x2 = xn.reshape(T, D)
    tok = jnp.arange(T)
    glog = (x2 @ w_group + b_group).astype(F32)
    gprob = jax.nn.softmax(glog, axis=-1)
    grp = jnp.argmax(glog, axis=-1)
    p_grp = gprob[tok, grp][:, None]
    elog = (x2 @ w_router + b_router).astype(F32).reshape(T, N_GROUPS, EXPERTS_PER_GROUP)[tok, grp]
    top_v, top_i = lax.top_k(elog, TOPK_IN_GROUP)
    top_w = jax.nn.softmax(top_v, axis=-1) * p_grp
    eidx = grp[:, None] * EXPERTS_PER_GROUP + top_i
    gate = jnp.sum(jax.nn.one_hot(eidx, N_EXPERTS, dtype=F32) * top_w[..., None], axis=1)
    h = jax.nn.silu(jnp.einsum('td,edf->tef', x2, w_gate)) * jnp.einsum('td,edf->tef', x2, w_up)
    h = h * gate[:, :, None].astype(h.dtype)
    return jnp.einsum('tef,efd->td', h, w_down).reshape(B, L, D)


def setup_inputs(seed: int = 0) -> dict:
    key = jax.random.key(seed)
    ks = jax.random.split(key, 48)
    n_pages = PAST_LEN // PAGE_SIZE
    n_pool = (DEC_BATCH * n_pages * 5) // 4

    def nrm(i, shape, scale=1.0):
        return jax.random.normal(ks[i], shape, F32) * scale

    def gain(i, shape):
        return 1.0 + 0.02 * jax.random.normal(ks[i], shape, F32)

    out_scale = (2.0 * DEPTH) ** -0.5
    page_table = jax.random.permutation(ks[0], n_pool)[: DEC_BATCH * n_pages].reshape(DEC_BATCH, n_pages).astype(jnp.int32)
    b_gates = jnp.concatenate([nrm(1, (N_EVEN, B_HEADS), 0.1),
                               3.0 + jnp.linspace(0.0, 3.0, B_HEADS)[None, :] + nrm(2, (N_EVEN, B_HEADS), 0.1)], axis=-1)
    return {
        'x_prompt': nrm(3, (BATCH, SEQ, D_MODEL)),
        'x_sample': nrm(4, (DEC_BATCH, DEC_SEQ, D_MODEL)),
        'mem_prompt': nrm(5, (BATCH, N_MEM, D_MODEL)),
        'cache_moba_k': nrm(6, (N_EVEN, n_pool, PAGE_SIZE, A_HEADS, A_HEAD_DIM)),
        'cache_moba_v': nrm(7, (N_EVEN, n_pool, PAGE_SIZE, A_HEADS, A_HEAD_DIM)),
        'cache_mla_kv': nrm(8, (N_ODD, n_pool, PAGE_SIZE, KV_LORA)),
        'cache_mla_pe': nrm(9, (N_ODD, n_pool, PAGE_SIZE, ROPE_DIM)),
        'state_mlstm_c': nrm(10, (N_EVEN, DEC_BATCH, B_HEADS, B_HEAD_DIM, B_HEAD_DIM), 0.3),
        'state_mlstm_n': nrm(11, (N_EVEN, DEC_BATCH, B_HEADS, B_HEAD_DIM), 0.3),
        'state_mlstm_m': nrm(12, (N_EVEN, DEC_BATCH, B_HEADS), 0.5),
        'state_conv': nrm(13, (N_EVEN, DEC_BATCH, CONV_W - 1, 2 * B_WIDTH)),
        'cache_mem_k': nrm(14, (DEPTH, DEC_BATCH, N_MEM, X_HEADS, X_HEAD_DIM)),
        'cache_mem_v': nrm(15, (DEPTH, DEC_BATCH, N_MEM, X_HEADS, X_HEAD_DIM)),
        'page_table': page_table,
        'norm_mix': gain(16, (DEPTH, D_MODEL)),
        'norm_cross': gain(17, (DEPTH, D_MODEL)),
        'norm_mem': gain(18, (DEPTH, D_MODEL)),
        'norm_ffn': gain(19, (DEPTH, D_MODEL)),
        'norm_final': gain(20, (D_MODEL,)),
        'w_in_ab': nrm(21, (N_EVEN, D_MODEL, IN_AB), D_MODEL ** -0.5),
        'b_gates': b_gates,
        'conv_w': nrm(22, (N_EVEN, CONV_W, 2 * B_WIDTH), CONV_W ** -0.5),
        'norm_mlstm': gain(23, (N_EVEN, B_WIDTH)),
        'w_out_ab': nrm(24, (N_EVEN, MIX_AB, D_MODEL), out_scale * MIX_AB ** -0.5),
        'w_in_c': nrm(25, (N_ODD, D_MODEL, IN_C), D_MODEL ** -0.5),
        'norm_q_lat': gain(26, (N_ODD, Q_LORA)),
        'norm_kv_lat': gain(27, (N_ODD, KV_LORA)),
        'w_qb': nrm(28, (N_ODD, Q_LORA, C_HEADS * (NOPE_DIM + ROPE_DIM)), Q_LORA ** -0.5),
        'w_kb': nrm(29, (N_ODD, KV_LORA, C_HEADS, NOPE_DIM), KV_LORA ** -0.5),
        'w_vb': nrm(30, (N_ODD, KV_LORA, C_HEADS, V_DIM), KV_LORA ** -0.5),
        'w_out_c': nrm(31, (N_ODD, C_HEADS * V_DIM, D_MODEL), out_scale * (C_HEADS * V_DIM) ** -0.5),
        'w_cq': nrm(32, (DEPTH, D_MODEL, X_WIDTH), D_MODEL ** -0.5),
        'w_ck': nrm(33, (DEPTH, D_MODEL, X_WIDTH), D_MODEL ** -0.5),
        'w_cv': nrm(34, (DEPTH, D_MODEL, X_WIDTH), D_MODEL ** -0.5),
        'w_co': nrm(35, (DEPTH, X_WIDTH, D_MODEL), out_scale * X_WIDTH ** -0.5),
        'w_group': nrm(36, (DEPTH, D_MODEL, N_GROUPS), D_MODEL ** -0.5),
        'b_group': nrm(37, (DEPTH, N_GROUPS), 0.01),
        'w_router': nrm(38, (DEPTH, D_MODEL, N_EXPERTS), D_MODEL ** -0.5),
        'b_router': nrm(39, (DEPTH, N_EXPERTS), 0.01),
        'w_up': nrm(40, (DEPTH, N_EXPERTS, D_MODEL, D_EXPERT), D_MODEL ** -0.5),
        'w_gate': nrm(41, (DEPTH, N_EXPERTS, D_MODEL, D_EXPERT), D_MODEL ** -0.5),
        'w_down': nrm(42, (DEPTH, N_EXPERTS, D_EXPERT, D_MODEL), out_scale * D_EXPERT ** -0.5),
    }


def reference(x_prompt, x_sample, mem_prompt, cache_moba_k, cache_moba_v, cache_mla_kv, cache_mla_pe,
              state_mlstm_c, state_mlstm_n, state_mlstm_m, state_conv, cache_mem_k, cache_mem_v, page_table,
              norm_mix, norm_cross, norm_mem, norm_ffn, norm_final, w_in_ab, b_gates, conv_w, norm_mlstm, w_out_ab,
              w_in_c, norm_q_lat, norm_kv_lat, w_qb, w_kb, w_vb, w_out_c, w_cq, w_ck, w_cv, w_co,
              w_group, b_group, w_router, b_router, w_up, w_gate, w_down):
    bp, lp, _ = x_prompt.shape
    bs, ls, _ = x_sample.shape
    past_len = page_table.shape[1] * PAGE_SIZE
    pos_p = jnp.arange(lp, dtype=jnp.int32)
    pos_s = past_len + jnp.arange(ls, dtype=jnp.int32)
    xp, xs = x_prompt, x_sample
    ak_p, av_p, kv_p, pe_p, mc_p, mn_p, mm_p, cv_p, mk_p, mv_p = [], [], [], [], [], [], [], [], [], []
    ak_s, av_s, kv_s, pe_s, mc_s, mn_s, mm_s, cv_s = [], [], [], [], [], [], [], []
    for l in range(DEPTH):
        j = l // 2
        hp = rmsnorm(xp, norm_mix[l])
        hs = rmsnorm(xs, norm_mix[l])
        if l % 2 == 0:
            wts = (w_in_ab[j], b_gates[j], conv_w[j], norm_mlstm[j], w_out_ab[j])
            st0 = (jnp.zeros((bp, B_HEADS, B_HEAD_DIM, B_HEAD_DIM), F32), jnp.zeros((bp, B_HEADS, B_HEAD_DIM), F32), jnp.zeros((bp, B_HEADS), F32))
            empty = jnp.zeros((bp, 0, A_HEADS, A_HEAD_DIM), xp.dtype)
            buf0 = jnp.zeros((bp, CONV_W - 1, 2 * B_WIDTH), xp.dtype)
            dp, ka, va, cv, (mc, mn, mm) = even_mixer(hp, *wts, empty, empty, buf0, st0, pos_p)
            ak_p.append(ka); av_p.append(va); cv_p.append(cv); mc_p.append(mc); mn_p.append(mn); mm_p.append(mm)
            st = (state_mlstm_c[j].astype(F32), state_mlstm_n[j].astype(F32), state_mlstm_m[j].astype(F32))
            ds, ka, va, cv, (mc, mn, mm) = even_mixer(hs, *wts, gather_pages(cache_moba_k, j, page_table),
                                                      gather_pages(cache_moba_v, j, page_table), state_conv[j], st, pos_s)
            ak_s.append(ka); av_s.append(va); cv_s.append(cv); mc_s.append(mc); mn_s.append(mn); mm_s.append(mm)
        else:
            wts = (w_in_c[j], norm_q_lat[j], norm_kv_lat[j], w_qb[j], w_kb[j], w_vb[j], w_out_c[j])
            dp, ckv, kpe = odd_mixer(hp, *wts, jnp.zeros((bp, 0, KV_LORA), xp.dtype), jnp.zeros((bp, 0, ROPE_DIM), xp.dtype), pos_p)
            kv_p.append(ckv); pe_p.append(kpe)
            ds, ckv, kpe = odd_mixer(hs, *wts, gather_pages(cache_mla_kv, j, page_table), gather_pages(cache_mla_pe, j, page_table), pos_s)
            kv_s.append(ckv); pe_s.append(kpe)
        xp = xp + dp
        xs = xs + ds
        mk, mv = mem_kv(mem_prompt, norm_mem[l], w_ck[l], w_cv[l])
        mk_p.append(mk); mv_p.append(mv)
        xp = xp + cross_attention(rmsnorm(xp, norm_cross[l]), mk, mv, w_cq[l], w_co[l])
        xs = xs + cross_attention(rmsnorm(xs, norm_cross[l]), cache_mem_k[l], cache_mem_v[l], w_cq[l], w_co[l])
        moe_w = (w_group[l], b_group[l], w_router[l], b_router[l], w_up[l], w_gate[l], w_down[l])
        xp = xp + hier_moe(rmsnorm(xp, norm_ffn[l]), *moe_w)
        xs = xs + hier_moe(rmsnorm(xs, norm_ffn[l]), *moe_w)
    y_prompt = rmsnorm(xp, norm_final)
    y_sample = rmsnorm(xs, norm_final)
    return (y_prompt, y_sample,
            jnp.stack(ak_p), jnp.stack(av_p), jnp.stack(kv_p), jnp.stack(pe_p),
            jnp.stack(mc_p), jnp.stack(mn_p), jnp.stack(mm_p), jnp.stack(cv_p), jnp.stack(mk_p), jnp.stack(mv_p),
            jnp.stack(ak_s), jnp.stack(av_s), jnp.stack(kv_s), jnp.stack(pe_s),
            jnp.stack(mc_s), jnp.stack(mn_s), jnp.stack(mm_s), jnp.stack(cv_s))
```

```python
import functools
import math

import jax
import jax.numpy as jnp
from jax import lax
from jax.experimental import pallas as pl
from jax.experimental.pallas import tpu as pltpu

F32 = jnp.float32
BF16 = jnp.bfloat16
HI = lax.Precision.HIGHEST
EPS = 1e-6
NEG = -1e30
VMEM_LIMIT = 56 * 1024 * 1024
LANES = 128

PAGE_SIZE = 128
A_HEADS, A_HEAD_DIM = 8, 64
A_WIDTH = A_HEADS * A_HEAD_DIM
MOBA_BLOCK, MOBA_TOPK = 256, 3
B_HEADS, B_HEAD_DIM = 4, 128
B_WIDTH = B_HEADS * B_HEAD_DIM
CONV_W = 4
MLSTM_CHUNK = 64
C_HEADS, Q_LORA, KV_LORA, NOPE_DIM, ROPE_DIM, V_DIM = 16, 256, 128, 64, 32, 64
ROPE_THETA = 10000.0
X_HEADS, X_HEAD_DIM = 4, 128
X_WIDTH = X_HEADS * X_HEAD_DIM
N_GROUPS, EXPERTS_PER_GROUP = 4, 4
N_EXPERTS = N_GROUPS * EXPERTS_PER_GROUP
PAGES_PER_STEP = 8
MOBA_PAGES_PER_STEP = 16

NT_DIMS = (((1,), (1,)), ((), ()))
TN_DIMS = (((0,), (0,)), ((), ()))


def _params(*sem):
    return pltpu.CompilerParams(dimension_semantics=sem, vmem_limit_bytes=VMEM_LIMIT)


def _nt(a, b, precision=None):
    return lax.dot_general(a, b, NT_DIMS, precision=precision, preferred_element_type=F32)


def _dot(a, b, precision=None):
    return jnp.dot(a, b, precision=precision, preferred_element_type=F32)


def _rms(x, g):
    return x * lax.rsqrt(jnp.mean(x * x, axis=-1, keepdims=True) + EPS) * g


def _sigmoid(x):
    return 1.0 / (1.0 + jnp.exp(-x))


def _linear_body(*refs, n_in, has_gain, has_res, splits):
    x_refs, w_refs = refs[:n_in], refs[n_in:2 * n_in]
    p = 2 * n_in
    g_ref = refs[p] if has_gain else None
    p += int(has_gain)
    r_ref = refs[p] if has_res else None
    p += int(has_res)
    o_refs = refs[p:]
    xs = []
    for xr in x_refs:
        x = xr[...]
        if has_gain:
            x = _rms(x, g_ref[...])
        xs.append(x.astype(BF16))
    for (off, width), o_ref in zip(splits, o_refs):
        acc = None
        for x, wr in zip(xs, w_refs):
            y = _dot(x, wr[:, off:off + width])
            acc = y if acc is None else acc + y
        if has_res:
            acc = acc + r_ref[...]
        o_ref[...] = acc.astype(o_ref.dtype)


def fused_linear(xs, ws, splits=None, gain=None, residual=None, tm=512, name="linear"):
    m, n = xs[0].shape[0], ws[0].shape[1]
    splits = splits or [(0, n)]
    tm = min(tm, m)
    assert m % tm == 0
    in_specs = [pl.BlockSpec((tm, x.shape[1]), lambda i: (i, 0)) for x in xs]
    in_specs += [pl.BlockSpec(w.shape, lambda i: (0, 0)) for w in ws]
    args = list(xs) + list(ws)
    if gain is not None:
        in_specs.append(pl.BlockSpec((1, gain.shape[-1]), lambda i: (0, 0)))
        args.append(gain.reshape(1, -1))
    if residual is not None:
        assert len(splits) == 1
        in_specs.append(pl.BlockSpec((tm, n), lambda i: (i, 0)))
        args.append(residual)
    outs = pl.pallas_call(
        functools.partial(_linear_body, n_in=len(xs), has_gain=gain is not None, has_res=residual is not None,
                          splits=tuple(splits)),
        grid=(m // tm,),
        in_specs=in_specs,
        out_specs=[pl.BlockSpec((tm, w), lambda i: (i, 0)) for _, w in splits],
        out_shape=[jax.ShapeDtypeStruct((m, w), F32) for _, w in splits],
        compiler_params=_params("parallel"),
        name=name,
    )(*args)
    return outs


def _rmsnorm_body(x_ref, g_ref, o_ref):
    o_ref[...] = _rms(x_ref[...], g_ref[...])


def rmsnorm_rows(x, g, tm=1024):
    m, d = x.shape
    tm = min(tm, m)
    return pl.pallas_call(
        _rmsnorm_body,
        grid=(m // tm,),
        in_specs=[pl.BlockSpec((tm, d), lambda i: (i, 0)), pl.BlockSpec((1, d), lambda i: (0, 0))],
        out_specs=pl.BlockSpec((tm, d), lambda i: (i, 0)),
        out_shape=jax.ShapeDtypeStruct((m, d), F32),
        compiler_params=_params("parallel"),
        name="final_norm",
    )(x, g.reshape(1, d))


def _topk_mask(g, valid, n_iota, nb, topk):
    gm = jnp.where(valid, g, -jnp.inf)
    rank = jnp.zeros(g.shape, jnp.int32)
    for m in range(nb):
        gc = gm[:, m:m + 1]
        beats = (gc > gm) | ((gc == gm) & (m < n_iota))
        rank = rank + beats.astype(jnp.int32)
    return ((rank < topk) & valid).astype(F32)


def _moba_prompt_body(q_ref, k_ref, v_ref, o_ref, kmean_ref, *, nb, blk, heads, dh, topk):
    i = pl.program_id(1)

    @pl.when(i == 0)
    def _():
        for n in range(nb):
            kmean_ref[n:n + 1, :] = jnp.mean(k_ref[0, n * blk:(n + 1) * blk, :], axis=0, keepdims=True)

    q = q_ref[0]
    row = lax.broadcasted_iota(jnp.int32, (blk, blk), 0)
    col = lax.broadcasted_iota(jnp.int32, (blk, blk), 1)
    dist_own = (row - col).astype(F32)
    causal = row >= col
    n_iota = lax.broadcasted_iota(jnp.int32, (blk, nb), 1)
    scale = dh ** -0.5
    own_start = pl.multiple_of(i * blk, blk)
    for h in range(heads):
        sl = slice(h * dh, (h + 1) * dh)
        slope = 2.0 ** (-8.0 * (h + 1) / heads)
        qh = q[:, sl]
        gate = _nt(qh, kmean_ref[:, sl], HI)
        sel = _topk_mask(gate, n_iota < i, n_iota, nb, topk)
        qb = (qh * scale).astype(BF16)
        k_own = k_ref[0, pl.ds(own_start, blk), sl].astype(BF16)
        v_own = v_ref[0, pl.ds(own_start, blk), sl].astype(BF16)
        s = _nt(qb, k_own) - slope * dist_own
        s = jnp.where(causal, s, NEG)
        m0 = jnp.max(s, axis=1, keepdims=True)
        p = jnp.exp(s - m0)
        l0 = jnp.sum(p, axis=1, keepdims=True)
        acc0 = _dot(p.astype(BF16), v_own)

        def body(j, carry, sl=sl, slope=slope, qb=qb, sel=sel):
            m, l, acc = carry
            start = pl.multiple_of(j * blk, blk)
            kj = k_ref[0, pl.ds(start, blk), sl].astype(BF16)
            vj = v_ref[0, pl.ds(start, blk), sl].astype(BF16)
            base = ((i - j) * blk).astype(F32)
            sj = _nt(qb, kj) - slope * (dist_own + base)
            selj = jnp.sum(jnp.where(n_iota == j, sel, 0.0), axis=1, keepdims=True)
            sj = jnp.where(selj > 0.5, sj, NEG)
            m_new = jnp.maximum(m, jnp.max(sj, axis=1, keepdims=True))
            alpha = jnp.exp(m - m_new)
            pj = jnp.exp(sj - m_new)
            l = alpha * l + jnp.sum(pj, axis=1, keepdims=True)
            acc = alpha * acc + _dot(pj.astype(BF16), vj)
            return m_new, l, acc

        _, l, acc = lax.fori_loop(0, i, body, (m0, l0, acc0))
        o_ref[0, :, sl] = acc / l


def moba_prompt(q, k, v):
    b, l, w = q.shape
    blk = MOBA_BLOCK
    nb = l // blk
    return pl.pallas_call(
        functools.partial(_moba_prompt_body, nb=nb, blk=blk, heads=A_HEADS, dh=A_HEAD_DIM, topk=MOBA_TOPK),
        grid=(b, nb),
        in_specs=[pl.BlockSpec((1, blk, w), lambda bi, i: (bi, i, 0)),
                  pl.BlockSpec((1, l, w), lambda bi, i: (bi, 0, 0)),
                  pl.BlockSpec((1, l, w), lambda bi, i: (bi, 0, 0))],
        out_specs=pl.BlockSpec((1, blk, w), lambda bi, i: (bi, i, 0)),
        out_shape=jax.ShapeDtypeStruct((b, l, w), F32),
        scratch_shapes=[pltpu.VMEM((nb, w), F32)],
        compiler_params=_params("parallel", "arbitrary"),
        name="moba_prompt",
    )(q, k, v)


def _page_specs(layer, width, n):
    return [pl.BlockSpec((None, 1, PAGE_SIZE, width), lambda b, c, pt, u=u: (layer, pt[b, c * n + u], 0, 0))
            for u in range(n)]


def _moba_scores_body(pt_ref, qbd_ref, *rest, n_u):
    k_refs, (st_ref, bsum_ref) = rest[:n_u], rest[n_u:]
    qbd = qbd_ref[0]
    ppb = MOBA_BLOCK // PAGE_SIZE
    for n in range(n_u // ppb):
        bsum = None
        for u in range(n * ppb, (n + 1) * ppb):
            kp = k_refs[u][0]
            psum = jnp.sum(kp, axis=0, keepdims=True)
            bsum = psum if bsum is None else bsum + psum
            st_ref[0, :, u * PAGE_SIZE:(u + 1) * PAGE_SIZE] = _nt(qbd, kp.astype(BF16))
        bsum_ref[0, n:n + 1, :] = bsum


def _moba_select_body(bsum_ref, qf_ref, st_ref, knew_ref, e_ref, p_ref, pown_ref, *, nb, lq, heads, dh, topk, past):
    r_n = heads * lq
    kmean = bsum_ref[0] * (1.0 / MOBA_BLOCK)
    qf = qf_ref[0]
    gate = _nt(qf, kmean, HI)
    n_iota = lax.broadcasted_iota(jnp.int32, (r_n, nb), 1)
    sel = _topk_mask(gate, n_iota >= 0, n_iota, nb, topk)
    mask = _dot(sel.astype(BF16), e_ref[...])
    r1 = lax.broadcasted_iota(jnp.int32, (r_n, 1), 0)
    qpos = r1 % lq
    slope = jnp.exp2((-8.0 / heads) * ((r1 // lq) + 1).astype(F32))
    t_n = st_ref.shape[2]
    kpos = lax.broadcasted_iota(jnp.int32, (r_n, t_n), 1)
    dist = ((past + qpos) - kpos).astype(F32)
    scale = dh ** -0.5
    s = st_ref[0] * scale - slope * dist
    s = jnp.where(mask > 0.5, s, NEG)
    knew = knew_ref[0]
    s_own = []
    for t in range(lq):
        st = jnp.sum(qf * knew[t:t + 1, :], axis=1, keepdims=True) * scale - slope * (qpos - t).astype(F32)
        s_own.append(jnp.where(qpos >= t, st, NEG))
    m = jnp.max(s, axis=1, keepdims=True)
    for st in s_own:
        m = jnp.maximum(m, st)
    p = jnp.exp(s - m)
    l = jnp.sum(p, axis=1, keepdims=True)
    p_own = [jnp.exp(st - m) for st in s_own]
    for po in p_own:
        l = l + po
    inv = 1.0 / l
    p_ref[0] = (p * inv).astype(BF16)
    lane = lax.broadcasted_iota(jnp.int32, (r_n, LANES), 1)
    acc = jnp.zeros((r_n, LANES), F32)
    for t, po in enumerate(p_own):
        acc = acc + jnp.where(lane == t, po * inv, 0.0)
    pown_ref[0] = acc


def _moba_pv_body(pt_ref, p_ref, pown_ref, vnew_ref, *rest, n_u, lq, dh):
    v_refs, (o_ref, acc_ref) = rest[:n_u], rest[n_u:]
    c = pl.program_id(1)

    @pl.when(c == 0)
    def _():
        pown = pown_ref[0]
        vnew = vnew_ref[0]
        acc = pown[:, 0:1] * vnew[0:1, :]
        for t in range(1, lq):
            acc = acc + pown[:, t:t + 1] * vnew[t:t + 1, :]
        acc_ref[...] = acc

    acc = acc_ref[...]
    for u in range(n_u):
        acc = acc + _dot(p_ref[0, :, u * PAGE_SIZE:(u + 1) * PAGE_SIZE], v_refs[u][0].astype(BF16))
    acc_ref[...] = acc

    @pl.when(c == pl.num_programs(1) - 1)
    def _():
        r_n, w = acc_ref.shape
        r = lax.broadcasted_iota(jnp.int32, (r_n, w), 0)
        cc = lax.broadcasted_iota(jnp.int32, (r_n, w), 1)
        own_head = jnp.where((r // lq) == (cc // dh), acc_ref[...], 0.0)
        qi = lax.broadcasted_iota(jnp.int32, (8, r_n), 0)
        ri = lax.broadcasted_iota(jnp.int32, (8, r_n), 1)
        pick = ((ri % lq) == qi).astype(F32)
        o_ref[0] = _dot(pick, own_head, HI)


def moba_sample(q, k_new, v_new, cache_k, cache_v, layer, page_table):
    b, lq, w = q.shape
    n_pages = page_table.shape[1]
    past = n_pages * PAGE_SIZE
    nb = past // MOBA_BLOCK
    heads, dh = A_HEADS, A_HEAD_DIM
    r_n = heads * lq
    n_u = MOBA_PAGES_PER_STEP
    n_c = n_pages // n_u
    ppb = MOBA_BLOCK // PAGE_SIZE
    head_of_col = jnp.arange(w) // dh
    qbd = q[:, None, :, :] * (head_of_col[None, None, None, :] == jnp.arange(heads)[None, :, None, None]).astype(F32)
    qbd = qbd.reshape(b, r_n, w)
    pad8 = lambda a: jnp.pad(a, ((0, 0), (0, 8 - lq), (0, 0)))
    expand = (jnp.arange(past)[None, :] // MOBA_BLOCK == jnp.arange(nb)[:, None]).astype(BF16)

    st, bsum = pl.pallas_call(
        functools.partial(_moba_scores_body, n_u=n_u),
        grid_spec=pltpu.PrefetchScalarGridSpec(
            num_scalar_prefetch=1, grid=(b, n_c),
            in_specs=[pl.BlockSpec((1, r_n, w), lambda bi, c, pt: (bi, 0, 0))] + _page_specs(layer, w, n_u),
            out_specs=[pl.BlockSpec((1, r_n, n_u * PAGE_SIZE), lambda bi, c, pt: (bi, 0, c)),
                       pl.BlockSpec((1, n_u // ppb, w), lambda bi, c, pt: (bi, c, 0))]),
        out_shape=[jax.ShapeDtypeStruct((b, r_n, past), F32), jax.ShapeDtypeStruct((b, nb, w), F32)],
        compiler_params=_params("parallel", "arbitrary"),
        name="moba_sample_scores",
    )(page_table, qbd.astype(BF16), *([cache_k] * n_u))

    p, pown = pl.pallas_call(
        functools.partial(_moba_select_body, nb=nb, lq=lq, heads=heads, dh=dh, topk=MOBA_TOPK, past=past),
        grid=(b,),
        in_specs=[pl.BlockSpec((1, nb, w), lambda bi: (bi, 0, 0)),
                  pl.BlockSpec((1, r_n, w), lambda bi: (bi, 0, 0)),
                  pl.BlockSpec((1, r_n, past), lambda bi: (bi, 0, 0)),
                  pl.BlockSpec((1, 8, w), lambda bi: (bi, 0, 0)),
                  pl.BlockSpec((nb, past), lambda bi: (0, 0))],
        out_specs=[pl.BlockSpec((1, r_n, past), lambda bi: (bi, 0, 0)),
                   pl.BlockSpec((1, r_n, LANES), lambda bi: (bi, 0, 0))],
        out_shape=[jax.ShapeDtypeStruct((b, r_n, past), BF16), jax.ShapeDtypeStruct((b, r_n, LANES), F32)],
        compiler_params=_params("parallel"),
        name="moba_sample_select",
    )(bsum, qbd, st, pad8(k_new), expand)

    o = pl.pallas_call(
        functools.partial(_moba_pv_body, n_u=n_u, lq=lq, dh=dh),
        grid_spec=pltpu.PrefetchScalarGridSpec(
            num_scalar_prefetch=1, grid=(b, n_c),
            in_specs=[pl.BlockSpec((1, r_n, n_u * PAGE_SIZE), lambda bi, c, pt: (bi, 0, c)),
                      pl.BlockSpec((1, r_n, LANES), lambda bi, c, pt: (bi, 0, 0)),
                      pl.BlockSpec((1, 8, w), lambda bi, c, pt: (bi, 0, 0))] + _page_specs(layer, w, n_u),
            out_specs=pl.BlockSpec((1, 8, w), lambda bi, c, pt: (bi, 0, 0)),
            scratch_shapes=[pltpu.VMEM((r_n, w), F32)]),
        out_shape=jax.ShapeDtypeStruct((b, 8, w), F32),
        compiler_params=_params("parallel", "arbitrary"),
        name="moba_sample_pv",
    )(page_table, p, pown, pad8(v_new), *([cache_v] * n_u))
    return o[:, :lq, :]


def _log_sigmoid(x):
    return -(jnp.maximum(-x, 0.0) + jnp.log1p(jnp.exp(-jnp.abs(x))))


def _mlstm_body(qk_ref, v_ref, og_ref, gcol_ref, grow_ref, bcol_ref, brow_ref, cw_ref, gh_ref, cbuf_ref,
                c0_ref, n0_ref, m0_ref, h_ref, c_out, n_out, m_out, xbuf, c_s, n_s, m_s, *, ch, valid, heads, dh):
    c = pl.program_id(1)
    width = heads * dh

    @pl.when(c == 0)
    def _():
        xbuf[0:8, :] = cbuf_ref[0]
        c_s[...] = c0_ref[0]
        n_s[...] = n0_ref[0]
        m_s[...] = m0_ref[0]

    xbuf[8:8 + ch, :] = qk_ref[0]
    cw = cw_ref[...]
    back = CONV_W - 1
    y = xbuf[8 - back:8 - back + ch, :] * cw[0:1, :]
    for t in range(1, CONV_W):
        y = y + xbuf[8 - back + t:8 - back + t + ch, :] * cw[t:t + 1, :]
    xbuf[8 - back:8, :] = xbuf[8 + ch - back:8 + ch, :]
    y = y * _sigmoid(y)

    t_col = lax.broadcasted_iota(jnp.int32, (ch, LANES), 0)
    t_row = lax.broadcasted_iota(jnp.int32, (8, ch), 1)
    gcol = gcol_ref[0] + bcol_ref[...]
    grow = grow_ref[0, 0] + brow_ref[:, :ch]
    ig_col = jnp.where(t_col < valid, gcol, NEG)
    lf_col = jnp.where(t_col < valid, _log_sigmoid(gcol), 0.0)
    ig_row = jnp.where(t_row < valid, grow, NEG)
    lf_row = jnp.where(t_row < valid, _log_sigmoid(grow), 0.0)
    ti = lax.broadcasted_iota(jnp.int32, (ch, ch), 0)
    si = lax.broadcasted_iota(jnp.int32, (ch, ch), 1)
    causal = ti >= si
    b_col = _dot(causal.astype(F32), lf_col, HI)
    b_row = _dot(lf_row, (ti <= si).astype(F32), HI)

    v = v_ref[0]
    og = og_ref[0]
    gh = gh_ref[...]
    for h in range(heads):
        sl = slice(h * dh, (h + 1) * dh)
        qh = y[:, sl]
        kh = y[:, width + h * dh:width + (h + 1) * dh] * (dh ** -0.5)
        vh = v[:, sl]
        igc, bc = ig_col[:, h:h + 1], b_col[:, heads + h:heads + h + 1]
        igr, br = ig_row[h:h + 1, :], b_row[heads + h:heads + h + 1, :]
        m_prev = m_s[h][:, 0:1]
        c_prev = c_s[h]
        n_prev = n_s[h]
        log_d = jnp.where(causal, bc - br + igr, NEG)
        inter = bc + m_prev
        mt = jnp.maximum(inter, jnp.max(log_d, axis=1, keepdims=True))
        d = jnp.exp(log_d - mt)
        w_inter = jnp.exp(inter - mt)
        qb, kb, vb = qh.astype(BF16), kh.astype(BF16), vh.astype(BF16)
        a = _nt(qb, kb) * d
        num = _dot(a.astype(BF16), vb) + w_inter * _dot(qb, c_prev.astype(BF16))
        den = jnp.sum(a, axis=1, keepdims=True) + w_inter * jnp.sum(qh * n_prev, axis=1, keepdims=True)
        hh = num / jnp.maximum(jnp.abs(den), jnp.exp(-mt))
        b_last = bc[ch - 1:ch, :]
        logw = b_last - bc + igc
        m_new = jnp.maximum(b_last + m_prev, jnp.max(logw, axis=0, keepdims=True))
        ws = jnp.exp(logw - m_new)
        decay = jnp.exp(b_last + m_prev - m_new)
        kw = kh * ws
        c_s[h] = decay * c_prev + lax.dot_general(kw.astype(BF16), vb, TN_DIMS, preferred_element_type=F32)
        n_s[h] = decay * n_prev + jnp.sum(kw, axis=0, keepdims=True)
        m_s[h] = jnp.broadcast_to(m_new, (1, LANES))
        hn = _rms(hh, gh[:, sl])
        h_ref[0, :, sl] = hn * _sigmoid(og[:, sl])

    @pl.when(c == pl.num_programs(1) - 1)
    def _():
        c_out[0] = c_s[...]
        n_out[0] = n_s[...]
        m_out[0] = m_s[...]


def mlstm_layer(qk, v, og, gates, b_g, conv_w, g_h, conv_buf, c0, n0, m0, valid):
    b, lp, w2 = qk.shape
    w = w2 // 2
    heads, dh, ch = B_HEADS, B_HEAD_DIM, MLSTM_CHUNK
    nc = lp // ch
    assert valid == lp or nc == 1
    grow = gates[:, :, :8].reshape(b, nc, ch, 8).transpose(0, 1, 3, 2)
    bcol = jnp.pad(b_g, (0, LANES - 8)).reshape(1, LANES)
    brow = jnp.broadcast_to(b_g[:, None], (8, LANES))
    cw = jnp.pad(conv_w, ((0, 8 - CONV_W), (0, 0)))
    cbuf = jnp.pad(conv_buf, ((0, 0), (8 - (CONV_W - 1), 0), (0, 0)))
    n0 = n0.reshape(b, heads, 1, dh)
    m0 = jnp.broadcast_to(m0[:, :, None, None], (b, heads, 1, LANES))
    full = lambda shape: pl.BlockSpec(shape, lambda bi, c: (0,) * len(shape))
    per_b = lambda shape: pl.BlockSpec(shape, lambda bi, c: (bi,) + (0,) * (len(shape) - 1))
    seq = lambda width: pl.BlockSpec((1, ch, width), lambda bi, c: (bi, c, 0))
    h, c_f, n_f, m_f = pl.pallas_call(
        functools.partial(_mlstm_body, ch=ch, valid=valid if nc == 1 else ch, heads=heads, dh=dh),
        grid=(b, nc),
        in_specs=[seq(w2), seq(w), seq(w), seq(LANES),
                  pl.BlockSpec((1, 1, 8, ch), lambda bi, c: (bi, c, 0, 0)),
                  full((1, LANES)), full((8, LANES)), full((8, w2)), full((1, w)),
                  per_b((1, 8, w2)), per_b((1, heads, dh, dh)), per_b((1, heads, 1, dh)), per_b((1, heads, 1, LANES))],
        out_specs=[seq(w), per_b((1, heads, dh, dh)), per_b((1, heads, 1, dh)), per_b((1, heads, 1, LANES))],
        out_shape=[jax.ShapeDtypeStruct((b, lp, w), F32), jax.ShapeDtypeStruct((b, heads, dh, dh), F32),
                   jax.ShapeDtypeStruct((b, heads, 1, dh), F32), jax.ShapeDtypeStruct((b, heads, 1, LANES), F32)],
        scratch_shapes=[pltpu.VMEM((8 + ch, w2), F32), pltpu.VMEM((heads, dh, dh), F32),
                        pltpu.VMEM((heads, 1, dh), F32), pltpu.VMEM((heads, 1, LANES), F32)],
        compiler_params=_params("parallel", "arbitrary"),
        name="mlstm",
    )(qk, v, og, gates, grow, bcol, brow, cw, g_h.reshape(1, w), cbuf, c0, n0, m0)
    return h, c_f, n_f.reshape(b, heads, dh), m_f[:, :, 0, 0]


def _mla_prep_body(x_ref, g_ref, win_ref, gq_ref, gkv_ref, wn_ref, wa_ref, wb_ref, wkb_ref, cos_ref, sin_ref,
                   ckv_ref, kpe_ref, kcat_ref, qcat_ref, *, heads):
    xn = _rms(x_ref[...], g_ref[...]).astype(BF16)
    y = _dot(xn, win_ref[...])
    cos, sin = cos_ref[...], sin_ref[...]
    ckv = _rms(y[:, Q_LORA:Q_LORA + KV_LORA], gkv_ref[...])
    k0 = Q_LORA + KV_LORA
    kpe = y[:, k0:k0 + ROPE_DIM] * cos[:, :ROPE_DIM] + y[:, k0 + ROPE_DIM:k0 + 2 * ROPE_DIM] * sin[:, :ROPE_DIM]
    ckv_ref[...] = ckv
    kpe_ref[...] = kpe
    kcat_ref[:, 0:KV_LORA] = ckv.astype(BF16)
    kcat_ref[:, KV_LORA:KV_LORA + ROPE_DIM] = kpe.astype(BF16)
    cq = _rms(y[:, :Q_LORA], gq_ref[...]).astype(BF16)
    nope = _dot(cq, wn_ref[...])
    qpe = _dot(cq, wa_ref[...]) * cos + _dot(cq, wb_ref[...]) * sin
    scale = (NOPE_DIM + ROPE_DIM) ** -0.5
    for h in range(heads):
        lat = _dot(nope[:, h * NOPE_DIM:(h + 1) * NOPE_DIM].astype(BF16), wkb_ref[h])
        qcat_ref[h, :, 0:KV_LORA] = (lat * scale).astype(BF16)
        qcat_ref[h, :, KV_LORA:KV_LORA + ROPE_DIM] = (qpe[:, h * ROPE_DIM:(h + 1) * ROPE_DIM] * scale).astype(BF16)


def _rot_half(wpe):
    half = wpe.shape[-1] // 2
    return jnp.concatenate([-wpe[..., half:], wpe[..., :half]], axis=-1)


def mla_prep(x, gain, w_in, g_q, g_kv, w_qb, w_kb, pos, tm=512):
    m, d = x.shape
    heads = C_HEADS
    tm = min(tm, m)
    half = ROPE_DIM // 2
    freqs = ROPE_THETA ** (-jnp.arange(half, dtype=F32) / half)
    ang = pos.astype(F32)[:, None] * freqs
    cos = jnp.tile(jnp.cos(ang), (1, 2 * heads))
    sin = jnp.tile(jnp.sin(ang), (1, 2 * heads))
    k0 = Q_LORA + KV_LORA
    win = jnp.concatenate([w_in, _rot_half(w_in[:, k0:k0 + ROPE_DIM]),
                           jnp.zeros((d, 512 - k0 - 2 * ROPE_DIM), F32)], axis=1).astype(BF16)
    wq = w_qb.reshape(Q_LORA, heads, NOPE_DIM + ROPE_DIM)
    wn = wq[:, :, :NOPE_DIM].reshape(Q_LORA, heads * NOPE_DIM).astype(BF16)
    wa = wq[:, :, NOPE_DIM:].reshape(Q_LORA, heads * ROPE_DIM).astype(BF16)
    wb = _rot_half(wq[:, :, NOPE_DIM:]).reshape(Q_LORA, heads * ROPE_DIM).astype(BF16)
    wkb = w_kb.transpose(1, 2, 0).astype(BF16)
    full = lambda a: pl.BlockSpec(a.shape, lambda i: (0,) * a.ndim)
    rows = lambda width: pl.BlockSpec((tm, width), lambda i: (i, 0))
    dcat = KV_LORA + ROPE_DIM
    gq, gkv, gm = g_q.reshape(1, -1), g_kv.reshape(1, -1), gain.reshape(1, -1)
    return pl.pallas_call(
        functools.partial(_mla_prep_body, heads=heads),
        grid=(m // tm,),
        in_specs=[rows(d), full(gm), full(win), full(gq), full(gkv), full(wn), full(wa), full(wb), full(wkb),
                  rows(heads * ROPE_DIM), rows(heads * ROPE_DIM)],
        out_specs=[rows(KV_LORA), rows(ROPE_DIM), rows(dcat), pl.BlockSpec((heads, tm, dcat), lambda i: (0, i, 0))],
        out_shape=[jax.ShapeDtypeStruct((m, KV_LORA), F32), jax.ShapeDtypeStruct((m, ROPE_DIM), F32),
                   jax.ShapeDtypeStruct((m, dcat), BF16), jax.ShapeDtypeStruct((heads, m, dcat), BF16)],
        compiler_params=_params("parallel"),
        name="mla_prep",
    )(x, gm, win, gq, gkv, wn, wa, wb, wkb, cos, sin)


def _mla_flash_body(q_ref, k_ref, wvb_ref, o_ref, m_s, l_s, acc_s, *, heads, tq, tk):
    i, j = pl.program_id(1), pl.program_id(2)
    rows = heads * tq

    @pl.when(j == 0)
    def _():
        m_s[...] = jnp.full(m_s.shape, NEG, F32)
        l_s[...] = jnp.zeros(l_s.shape, F32)
        acc_s[...] = jnp.zeros(acc_s.shape, F32)

    @pl.when(j * tk <= i * tq + tq - 1)
    def _():
        q = q_ref[...].reshape(rows, q_ref.shape[-1])
        k = k_ref[...]
        s = _nt(q, k)
        qpos = i * tq + lax.broadcasted_iota(jnp.int32, (heads, tq, tk), 1).reshape(rows, tk)
        kpos = j * tk + lax.broadcasted_iota(jnp.int32, (rows, tk), 1)
        s = jnp.where(kpos <= qpos, s, NEG)
        m_prev = m_s[...]
        m_new = jnp.maximum(m_prev, jnp.max(s, axis=1, keepdims=True))
        alpha = jnp.exp(m_prev - m_new)
        p = jnp.exp(s - m_new)
        l_s[...] = alpha * l_s[...] + jnp.sum(p, axis=1, keepdims=True)
        acc_s[...] = alpha * acc_s[...] + _dot(p.astype(BF16), k[:, :KV_LORA])
        m_s[...] = m_new

    @pl.when(j == pl.num_programs(2) - 1)
    def _():
        for h in range(heads):
            o_lat = acc_s[h * tq:(h + 1) * tq, :] / l_s[h * tq:(h + 1) * tq, :]
            o_ref[:, h * V_DIM:(h + 1) * V_DIM] = _dot(o_lat.astype(BF16), wvb_ref[h])


def mla_flash_prompt(qcat, kcat, wvb, batch, tq=128, tk=512):
    heads, m, dcat = qcat.shape
    l = m // batch
    nq, nk = l // tq, l // tk
    last_k = lambda i: (i * tq + tq - 1) // tk
    return pl.pallas_call(
        functools.partial(_mla_flash_body, heads=heads, tq=tq, tk=tk),
        grid=(batch, nq, nk),
        in_specs=[pl.BlockSpec((heads, tq, dcat), lambda b, i, j: (0, b * nq + i, 0)),
                  pl.BlockSpec((tk, dcat), lambda b, i, j: (b * nk + jnp.minimum(j, last_k(i)), 0)),
                  pl.BlockSpec(wvb.shape, lambda b, i, j: (0, 0, 0))],
        out_specs=pl.BlockSpec((tq, heads * V_DIM), lambda b, i, j: (b * nq + i, 0)),
        out_shape=jax.ShapeDtypeStruct((m, heads * V_DIM), F32),
        scratch_shapes=[pltpu.VMEM((heads * tq, 1), F32), pltpu.VMEM((heads * tq, 1), F32),
                        pltpu.VMEM((heads * tq, KV_LORA), F32)],
        compiler_params=_params("parallel", "parallel", "arbitrary"),
        name="mla_flash_prompt",
    )(qcat, kcat, wvb)


def _mla_sample_body(pt_ref, q_ref, knew_ref, *rest, n_u, lq):
    kv_refs, pe_refs = rest[:n_u], rest[n_u:2 * n_u]
    o_ref, m_s, l_s, acc_s = rest[2 * n_u:]
    c = pl.program_id(1)
    q = q_ref[0]
    r_n = q.shape[0]

    @pl.when(c == 0)
    def _():
        kn = knew_ref[0]
        s = _nt(q, kn)
        t = lax.broadcasted_iota(jnp.int32, s.shape, 1)
        qpos = lax.broadcasted_iota(jnp.int32, s.shape, 0) % lq
        s = jnp.where(t <= qpos, s, NEG)
        m0 = jnp.max(s, axis=1, keepdims=True)
        p = jnp.exp(s - m0)
        m_s[...] = m0
        l_s[...] = jnp.sum(p, axis=1, keepdims=True)
        acc_s[...] = _dot(p.astype(BF16), kn[:, :KV_LORA])

    ql, qp = q[:, :KV_LORA], q[:, KV_LORA:]
    kvs = [kv_refs[u][0].astype(BF16) for u in range(n_u)]
    s = jnp.concatenate([_nt(ql, kvs[u]) + _nt(qp, pe_refs[u][0].astype(BF16)) for u in range(n_u)], axis=1)
    m_prev = m_s[...]
    m_new = jnp.maximum(m_prev, jnp.max(s, axis=1, keepdims=True))
    alpha = jnp.exp(m_prev - m_new)
    p = jnp.exp(s - m_new).astype(BF16)
    l_s[...] = alpha * l_s[...] + jnp.sum(p.astype(F32), axis=1, keepdims=True)
    acc = alpha * acc_s[...]
    for u in range(n_u):
        acc = acc + _dot(p[:, u * PAGE_SIZE:(u + 1) * PAGE_SIZE], kvs[u])
    acc_s[...] = acc
    m_s[...] = m_new

    @pl.when(c == pl.num_programs(1) - 1)
    def _():
        o_ref[0] = acc_s[...] / l_s[...]
    del r_n


def mla_sample(qcat, kcat_new, cache_kv, cache_pe, layer, page_table, lq):
    heads, m, dcat = qcat.shape
    b = m // lq
    n_pages = page_table.shape[1]
    n_u = PAGES_PER_STEP
    r_n = heads * lq
    q = qcat.reshape(heads, b, lq, dcat).transpose(1, 0, 2, 3).reshape(b, r_n, dcat)
    knew = jnp.pad(kcat_new.reshape(b, lq, dcat), ((0, 0), (0, PAGE_SIZE - lq), (0, 0)))
    kv_specs = [pl.BlockSpec((None, 1, PAGE_SIZE, KV_LORA), lambda bi, c, pt, u=u: (layer, pt[bi, c * n_u + u], 0, 0))
                for u in range(n_u)]
    pe_specs = [pl.BlockSpec((None, 1, PAGE_SIZE, ROPE_DIM), lambda bi, c, pt, u=u: (layer, pt[bi, c * n_u + u], 0, 0))
                for u in range(n_u)]
    o = pl.pallas_call(
        functools.partial(_mla_sample_body, n_u=n_u, lq=lq),
        grid_spec=pltpu.PrefetchScalarGridSpec(
            num_scalar_prefetch=1, grid=(b, n_pages // n_u),
            in_specs=[pl.BlockSpec((1, r_n, dcat), lambda bi, c, pt: (bi, 0, 0)),
                      pl.BlockSpec((1, PAGE_SIZE, dcat), lambda bi, c, pt: (bi, 0, 0))] + kv_specs + pe_specs,
            out_specs=pl.BlockSpec((1, r_n, KV_LORA), lambda bi, c, pt: (bi, 0, 0)),
            scratch_shapes=[pltpu.VMEM((r_n, 1), F32), pltpu.VMEM((r_n, 1), F32), pltpu.VMEM((r_n, KV_LORA), F32)]),
        out_shape=jax.ShapeDtypeStruct((b, r_n, KV_LORA), F32),
        compiler_params=_params("parallel", "arbitrary"),
        name="mla_sample",
    )(page_table, q, knew, *([cache_kv] * n_u), *([cache_pe] * n_u))
    return o.reshape(b, heads, lq, KV_LORA).transpose(1, 0, 2, 3).reshape(heads, m, KV_LORA)


def _headproj_body(x_ref, w_ref, o_ref, *, heads):
    for h in range(heads):
        o_ref[:, h * V_DIM:(h + 1) * V_DIM] = _dot(x_ref[h].astype(BF16), w_ref[h])


def head_value_proj(o_lat, wvb):
    heads, m, _ = o_lat.shape
    return pl.pallas_call(
        functools.partial(_headproj_body, heads=heads),
        out_shape=jax.ShapeDtypeStruct((m, heads * V_DIM), F32),
        compiler_params=pltpu.CompilerParams(vmem_limit_bytes=VMEM_LIMIT),
        name="mla_value_proj",
    )(o_lat, wvb)


def _cross_body(q_ref, k_ref, v_ref, o_ref, *, heads, dh):
    q = q_ref[0]
    scale = dh ** -0.5
    for h in range(heads):
        sl = slice(h * dh, (h + 1) * dh)
        s = _nt((q[:, sl] * scale).astype(BF16), k_ref[0, :, sl].astype(BF16))
        m = jnp.max(s, axis=1, keepdims=True)
        p = jnp.exp(s - m)
        l = jnp.sum(p, axis=1, keepdims=True)
        o_ref[0, :, sl] = _dot(p.astype(BF16), v_ref[0, :, sl].astype(BF16)) / l


def cross_core(q, mk, mv, tq=512):
    b, lq, w = q.shape
    n_mem = mk.shape[1]
    tq = min(tq, lq)
    return pl.pallas_call(
        functools.partial(_cross_body, heads=X_HEADS, dh=X_HEAD_DIM),
        grid=(b, lq // tq),
        in_specs=[pl.BlockSpec((1, tq, w), lambda bi, i: (bi, i, 0)),
                  pl.BlockSpec((1, n_mem, w), lambda bi, i: (bi, 0, 0)),
                  pl.BlockSpec((1, n_mem, w), lambda bi, i: (bi, 0, 0))],
        out_specs=pl.BlockSpec((1, tq, w), lambda bi, i: (bi, i, 0)),
        out_shape=jax.ShapeDtypeStruct((b, lq, w), F32),
        compiler_params=_params("parallel", "parallel"),
        name="cross_core",
    )(q, mk, mv)


def _moe_gates(logits):
    lane = lax.broadcasted_iota(jnp.int32, logits.shape, 1).astype(F32)
    big = 1e9
    is_g = lane < N_GROUPS
    gl = jnp.where(is_g, logits, -jnp.inf)
    gmax = jnp.max(gl, axis=1, keepdims=True)
    grp = jnp.min(jnp.where(is_g & (gl == gmax), lane, big), axis=1, keepdims=True)
    p_grp = 1.0 / jnp.sum(jnp.where(is_g, jnp.exp(gl - gmax), 0.0), axis=1, keepdims=True)
    e_idx = lane - N_GROUPS
    in_grp = (e_idx >= grp * EXPERTS_PER_GROUP) & (e_idx < (grp + 1) * EXPERTS_PER_GROUP)
    el = jnp.where(in_grp, logits, -jnp.inf)
    t1 = jnp.max(el, axis=1, keepdims=True)
    i1 = jnp.min(jnp.where(in_grp & (el == t1), lane, big), axis=1, keepdims=True)
    el2 = jnp.where(lane == i1, -jnp.inf, el)
    t2 = jnp.max(el2, axis=1, keepdims=True)
    i2 = jnp.min(jnp.where(in_grp & (lane != i1) & (el2 == t2), lane, big), axis=1, keepdims=True)
    e2 = jnp.exp(t2 - t1)
    w1 = 1.0 / (1.0 + e2)
    w2 = e2 / (1.0 + e2)
    return p_grp * (jnp.where(lane == i1, w1, 0.0) + jnp.where(lane == i2, w2, 0.0))


def _moe_body(x_ref, g_ref, wr_ref, br_ref, wg_ref, wu_ref, wd_ref, o_ref, xn_s, gate_s, acc_s):
    e = pl.program_id(1)

    @pl.when(e == 0)
    def _():
        x = x_ref[...]
        xn = _rms(x, g_ref[...])
        xn_s[...] = xn.astype(BF16)
        gate_s[...] = _moe_gates(_dot(xn, wr_ref[...], HI) + br_ref[...])
        acc_s[...] = x

    xn = xn_s[...]
    lane = lax.broadcasted_iota(jnp.int32, gate_s.shape, 1)
    ge = jnp.sum(jnp.where(lane == e + N_GROUPS, gate_s[...], 0.0), axis=1, keepdims=True)
    a = _dot(xn, wg_ref[0])
    u = _dot(xn, wu_ref[0])
    hid = (a * _sigmoid(a)) * u * ge
    acc_s[...] += _dot(hid.astype(BF16), wd_ref[0])

    @pl.when(e == pl.num_programs(1) - 1)
    def _():
        o_ref[...] = acc_s[...]


def moe_layer(x, gain, w_group, b_group, w_router, b_router, w_up, w_gate, w_down, tm=1024):
    m, d = x.shape
    tm = min(tm, m)
    n_e, _, f = w_up.shape
    wr = jnp.concatenate([w_group, w_router, jnp.zeros((d, LANES - N_GROUPS - N_EXPERTS), F32)], axis=1)
    br = jnp.concatenate([b_group, b_router, jnp.zeros((LANES - N_GROUPS - N_EXPERTS,), F32)]).reshape(1, LANES)
    return pl.pallas_call(
        _moe_body,
        grid=(m // tm, n_e),
        in_specs=[pl.BlockSpec((tm, d), lambda i, e: (i, 0)),
                  pl.BlockSpec((1, d), lambda i, e: (0, 0)),
                  pl.BlockSpec((d, LANES), lambda i, e: (0, 0)),
                  pl.BlockSpec((1, LANES), lambda i, e: (0, 0)),
                  pl.BlockSpec((1, d, f), lambda i, e: (e, 0, 0)),
                  pl.BlockSpec((1, d, f), lambda i, e: (e, 0, 0)),
                  pl.BlockSpec((1, f, d), lambda i, e: (e, 0, 0))],
        out_specs=pl.BlockSpec((tm, d), lambda i, e: (i, 0)),
        out_shape=jax.ShapeDtypeStruct((m, d), F32),
        scratch_shapes=[pltpu.VMEM((tm, d), BF16), pltpu.VMEM((tm, LANES), F32), pltpu.VMEM((tm, d), F32)],
        compiler_params=_params("parallel", "arbitrary"),
        name="moe",
    )(x, gain.reshape(1, d), wr, br, w_gate, w_up, w_down)


AB_SPLITS = [(0, A_WIDTH), (A_WIDTH, A_WIDTH), (2 * A_WIDTH, A_WIDTH), (3 * A_WIDTH, 2 * B_WIDTH),
             (3 * A_WIDTH + 2 * B_WIDTH, B_WIDTH), (3 * A_WIDTH + 3 * B_WIDTH, B_WIDTH),
             (3 * A_WIDTH + 4 * B_WIDTH, LANES)]


def _even_projection(x, gain, w_in_bf):
    return fused_linear([x], [w_in_bf], splits=AB_SPLITS, gain=gain, name="in_proj_ab")


def _pad_seq(a, lp):
    return jnp.pad(a, ((0, 0), (0, lp - a.shape[1]), (0, 0)))


def kernel(x_prompt, x_sample, mem_prompt, cache_moba_k, cache_moba_v, cache_mla_kv, cache_mla_pe, state_mlstm_c, state_mlstm_n, state_mlstm_m, state_conv, cache_mem_k, cache_mem_v, page_table, norm_mix, norm_cross, norm_mem, norm_ffn, norm_final, w_in_ab, b_gates, conv_w, norm_mlstm, w_out_ab, w_in_c, norm_q_lat, norm_kv_lat, w_qb, w_kb, w_vb, w_out_c, w_cq, w_ck, w_cv, w_co, w_group, b_group, w_router, b_router, w_up, w_gate, w_down):
    bp, lp, d = x_prompt.shape
    bs, ls, _ = x_sample.shape
    depth = norm_mix.shape[0]
    mp, ms = bp * lp, bs * ls
    n_mem = mem_prompt.shape[1]
    past = page_table.shape[1] * PAGE_SIZE
    pos_p = jnp.tile(jnp.arange(lp, dtype=jnp.int32), bp)
    pos_s = jnp.tile(past + jnp.arange(ls, dtype=jnp.int32), bs)
    n_pool = cache_moba_k.shape[1]
    ck_pages = cache_moba_k.reshape(cache_moba_k.shape[0], n_pool, PAGE_SIZE, A_WIDTH)
    cv_pages = cache_moba_v.reshape(cache_moba_v.shape[0], n_pool, PAGE_SIZE, A_WIDTH)

    xp = x_prompt.reshape(mp, d)
    xs = x_sample.reshape(ms, d)
    mem = mem_prompt.reshape(bp * n_mem, d)
    ak_p, av_p, kv_p, pe_p, mc_p, mn_p, mm_p, cv_p, mk_p, mv_p = [], [], [], [], [], [], [], [], [], []
    ak_s, av_s, kv_s, pe_s, mc_s, mn_s, mm_s, cv_s = [], [], [], [], [], [], [], []
    back = CONV_W - 1
    ch = MLSTM_CHUNK
    ls_pad = -(-ls // ch) * ch
    for l in range(depth):
        j = l // 2
        if l % 2 == 0:
            n_ab = w_in_ab.shape[2]
            n_pad = AB_SPLITS[-1][0] + LANES
            w_in_bf = jnp.pad(w_in_ab[j], ((0, 0), (0, n_pad - n_ab))).astype(BF16)
            w_out_bf = w_out_ab[j].astype(BF16)
            qa, ka, va, qkb, vb, ob, gt = _even_projection(xp, norm_mix[l], w_in_bf)
            o_a = moba_prompt(qa.reshape(bp, lp, A_WIDTH), ka.reshape(bp, lp, A_WIDTH), va.reshape(bp, lp, A_WIDTH))
            qkb3 = qkb.reshape(bp, lp, 2 * B_WIDTH)
            hb, mc, mn, mm = mlstm_layer(
                qkb3, vb.reshape(bp, lp, B_WIDTH), ob.reshape(bp, lp, B_WIDTH), gt.reshape(bp, lp, LANES),
                b_gates[j], conv_w[j], norm_mlstm[j], jnp.zeros((bp, back, 2 * B_WIDTH), F32),
                jnp.zeros((bp, B_HEADS, B_HEAD_DIM, B_HEAD_DIM), F32), jnp.zeros((bp, B_HEADS, B_HEAD_DIM), F32),
                jnp.zeros((bp, B_HEADS), F32), valid=lp)
            (xp,) = fused_linear([o_a.reshape(mp, A_WIDTH), hb.reshape(mp, B_WIDTH)],
                                 [w_out_bf[:A_WIDTH], w_out_bf[A_WIDTH:]], residual=xp, name="out_proj_ab")
            ak_p.append(ka.reshape(bp, lp, A_HEADS, A_HEAD_DIM)); av_p.append(va.reshape(bp, lp, A_HEADS, A_HEAD_DIM))
            cv_p.append(qkb3[:, lp - back:, :]); mc_p.append(mc); mn_p.append(mn); mm_p.append(mm)
            qa, ka, va, qkb, vb, ob, gt = _even_projection(xs, norm_mix[l], w_in_bf)
            o_a = moba_sample(qa.reshape(bs, ls, A_WIDTH), ka.reshape(bs, ls, A_WIDTH), va.reshape(bs, ls, A_WIDTH),
                              ck_pages, cv_pages, j, page_table)
            qkb3 = qkb.reshape(bs, ls, 2 * B_WIDTH)
            hb, mc, mn, mm = mlstm_layer(
                _pad_seq(qkb3, ls_pad), _pad_seq(vb.reshape(bs, ls, B_WIDTH), ls_pad),
                _pad_seq(ob.reshape(bs, ls, B_WIDTH), ls_pad), _pad_seq(gt.reshape(bs, ls, LANES), ls_pad),
                b_gates[j], conv_w[j], norm_mlstm[j], state_conv[j],
                state_mlstm_c[j], state_mlstm_n[j], state_mlstm_m[j], valid=ls)
            (xs,) = fused_linear([o_a.reshape(ms, A_WIDTH), hb[:, :ls].reshape(ms, B_WIDTH)],
                                 [w_out_bf[:A_WIDTH], w_out_bf[A_WIDTH:]], residual=xs, name="out_proj_ab")
            ak_s.append(ka.reshape(bs, ls, A_HEADS, A_HEAD_DIM)); av_s.append(va.reshape(bs, ls, A_HEADS, A_HEAD_DIM))
            conv_all = jnp.concatenate([state_conv[j], qkb3], axis=1)
            cv_s.append(conv_all[:, ls:, :]); mc_s.append(mc); mn_s.append(mn); mm_s.append(mm)
        else:
            wvb = w_vb[j].transpose(1, 0, 2).astype(BF16)
            w_out_bf = w_out_c[j].astype(BF16)
            ckv, kpe, kcat, qcat = mla_prep(xp, norm_mix[l], w_in_c[j], norm_q_lat[j], norm_kv_lat[j], w_qb[j], w_kb[j], pos_p)
            o = mla_flash_prompt(qcat, kcat, wvb, bp)
            (xp,) = fused_linear([o], [w_out_bf], residual=xp, name="out_proj_c")
            kv_p.append(ckv.reshape(bp, lp, KV_LORA)); pe_p.append(kpe.reshape(bp, lp, ROPE_DIM))
            ckv, kpe, kcat, qcat = mla_prep(xs, norm_mix[l], w_in_c[j], norm_q_lat[j], norm_kv_lat[j], w_qb[j], w_kb[j], pos_s)
            o_lat = mla_sample(qcat, kcat, cache_mla_kv, cache_mla_pe, j, page_table, ls)
            o = head_value_proj(o_lat, wvb)
            (xs,) = fused_linear([o], [w_out_bf], residual=xs, name="out_proj_c")
            kv_s.append(ckv.reshape(bs, ls, KV_LORA)); pe_s.append(kpe.reshape(bs, ls, ROPE_DIM))
        w_ckv = jnp.concatenate([w_ck[l], w_cv[l]], axis=1).astype(BF16)
        mk, mv = fused_linear([mem], [w_ckv], splits=[(0, X_WIDTH), (X_WIDTH, X_WIDTH)], gain=norm_mem[l], name="mem_kv")
        mk_p.append(mk.reshape(bp, n_mem, X_HEADS, X_HEAD_DIM)); mv_p.append(mv.reshape(bp, n_mem, X_HEADS, X_HEAD_DIM))
        w_cq_bf, w_co_bf = w_cq[l].astype(BF16), w_co[l].astype(BF16)
        (q,) = fused_linear([xp], [w_cq_bf], gain=norm_cross[l], name="cross_q")
        o = cross_core(q.reshape(bp, lp, X_WIDTH), mk.reshape(bp, n_mem, X_WIDTH), mv.reshape(bp, n_mem, X_WIDTH))
        (xp,) = fused_linear([o.reshape(mp, X_WIDTH)], [w_co_bf], residual=xp, name="cross_out")
        (q,) = fused_linear([xs], [w_cq_bf], gain=norm_cross[l], name="cross_q")
        q8 = _pad_seq(q.reshape(bs, ls, X_WIDTH), 8)
        o = cross_core(q8, cache_mem_k[l].reshape(bs, n_mem, X_WIDTH), cache_mem_v[l].reshape(bs, n_mem, X_WIDTH))
        (xs,) = fused_linear([o[:, :ls].reshape(ms, X_WIDTH)], [w_co_bf], residual=xs, name="cross_out")
        wu, wg, wd = w_up[l].astype(BF16), w_gate[l].astype(BF16), w_down[l].astype(BF16)
        xp = moe_layer(xp, norm_ffn[l], w_group[l], b_group[l], w_router[l], b_router[l], wu, wg, wd)
        xs = moe_layer(xs, norm_ffn[l], w_group[l], b_group[l], w_router[l], b_router[l], wu, wg, wd)
    y_prompt = rmsnorm_rows(xp, norm_final).reshape(bp, lp, d)
    y_sample = rmsnorm_rows(xs, norm_final).reshape(bs, ls, d)
    return (y_prompt, y_sample,
            jnp.stack(ak_p), jnp.stack(av_p), jnp.stack(kv_p), jnp.stack(pe_p),
            jnp.stack(mc_p), jnp.stack(mn_p), jnp.stack(mm_p), jnp.stack(cv_p), jnp.stack(mk_p), jnp.stack(mv_p),
            jnp.stack(ak_s), jnp.stack(av_s), jnp.stack(kv_s), jnp.stack(pe_s),
            jnp.stack(mc_s), jnp.stack(mn_s), jnp.stack(mm_s), jnp.stack(cv_s))
```

```python
import functools
import math

import jax
import jax.numpy as jnp
from jax import lax
from jax.experimental import pallas as pl
from jax.experimental.pallas import tpu as pltpu

F32 = jnp.float32
BF16 = jnp.bfloat16
HI = lax.Precision.HIGHEST
EPS = 1e-6
NEG = -1e30
VMEM_LIMIT = 56 * 1024 * 1024
LANES = 128

PAGE_SIZE = 128
A_HEADS, A_HEAD_DIM = 8, 64
A_WIDTH = A_HEADS * A_HEAD_DIM
MOBA_BLOCK, MOBA_TOPK = 256, 3
B_HEADS, B_HEAD_DIM = 4, 128
B_WIDTH = B_HEADS * B_HEAD_DIM
CONV_W = 4
MLSTM_CHUNK = 64
C_HEADS, Q_LORA, KV_LORA, NOPE_DIM, ROPE_DIM, V_DIM = 16, 256, 128, 64, 32, 64
ROPE_THETA = 10000.0
X_HEADS, X_HEAD_DIM = 4, 128
X_WIDTH = X_HEADS * X_HEAD_DIM
N_GROUPS, EXPERTS_PER_GROUP = 4, 4
N_EXPERTS = N_GROUPS * EXPERTS_PER_GROUP
PAGES_PER_STEP = 16
MOBA_PAGES_PER_STEP = 16

NT_DIMS = (((1,), (1,)), ((), ()))
TN_DIMS = (((0,), (0,)), ((), ()))


def _params(*sem):
    return pltpu.CompilerParams(dimension_semantics=sem, vmem_limit_bytes=VMEM_LIMIT)


def _nt(a, b, precision=None):
    return lax.dot_general(a, b, NT_DIMS, precision=precision, preferred_element_type=F32)


def _dot(a, b, precision=None):
    return jnp.dot(a, b, precision=precision, preferred_element_type=F32)


def _rms(x, g):
    return x * lax.rsqrt(jnp.mean(x * x, axis=-1, keepdims=True) + EPS) * g


def _sigmoid(x):
    return 1.0 / (1.0 + jnp.exp(-x))


def _linear_body(*refs, n_in, has_gain, has_res, splits):
    x_refs, w_refs = refs[:n_in], refs[n_in:2 * n_in]
    p = 2 * n_in
    g_ref = refs[p] if has_gain else None
    p += int(has_gain)
    r_ref = refs[p] if has_res else None
    p += int(has_res)
    o_refs = refs[p:]
    xs = []
    for xr in x_refs:
        x = xr[...]
        if has_gain:
            x = _rms(x, g_ref[...])
        xs.append(x.astype(BF16))
    for (off, width), o_ref in zip(splits, o_refs):
        acc = None
        for x, wr in zip(xs, w_refs):
            y = _dot(x, wr[:, off:off + width])
            acc = y if acc is None else acc + y
        if has_res:
            acc = acc + r_ref[...]
        o_ref[...] = acc.astype(o_ref.dtype)


def fused_linear(xs, ws, splits=None, gain=None, residual=None, tm=512, name="linear"):
    m, n = xs[0].shape[0], ws[0].shape[1]
    splits = splits or [(0, n)]
    tm = min(tm, m)
    assert m % tm == 0
    in_specs = [pl.BlockSpec((tm, x.shape[1]), lambda i: (i, 0)) for x in xs]
    in_specs += [pl.BlockSpec(w.shape, lambda i: (0, 0)) for w in ws]
    args = list(xs) + list(ws)
    if gain is not None:
        in_specs.append(pl.BlockSpec((1, gain.shape[-1]), lambda i: (0, 0)))
        args.append(gain.reshape(1, -1))
    if residual is not None:
        assert len(splits) == 1
        in_specs.append(pl.BlockSpec((tm, n), lambda i: (i, 0)))
        args.append(residual)
    outs = pl.pallas_call(
        functools.partial(_linear_body, n_in=len(xs), has_gain=gain is not None, has_res=residual is not None,
                          splits=tuple(splits)),
        grid=(m // tm,),
        in_specs=in_specs,
        out_specs=[pl.BlockSpec((tm, w), lambda i: (i, 0)) for _, w in splits],
        out_shape=[jax.ShapeDtypeStruct((m, w), F32) for _, w in splits],
        compiler_params=_params("parallel"),
        name=name,
    )(*args)
    return outs


def _rmsnorm_body(x_ref, g_ref, o_ref):
    o_ref[...] = _rms(x_ref[...], g_ref[...])


def rmsnorm_rows(x, g, tm=1024):
    m, d = x.shape
    tm = min(tm, m)
    return pl.pallas_call(
        _rmsnorm_body,
        grid=(m // tm,),
        in_specs=[pl.BlockSpec((tm, d), lambda i: (i, 0)), pl.BlockSpec((1, d), lambda i: (0, 0))],
        out_specs=pl.BlockSpec((tm, d), lambda i: (i, 0)),
        out_shape=jax.ShapeDtypeStruct((m, d), F32),
        compiler_params=_params("parallel"),
        name="final_norm",
    )(x, g.reshape(1, d))


def _topk_mask(g, valid, n_iota, nb, topk):
    gm = jnp.where(valid, g, -jnp.inf)
    rank = jnp.zeros(g.shape, jnp.int32)
    for m in range(nb):
        gc = gm[:, m:m + 1]
        beats = (gc > gm) | ((gc == gm) & (m < n_iota))
        rank = rank + beats.astype(jnp.int32)
    return ((rank < topk) & valid).astype(F32)


def _topk_rows(g, valid, n_iota, nb, topk):
    gm = jnp.where(valid, g, -jnp.inf)
    rank = jnp.zeros(g.shape, jnp.int32)
    for m in range(nb):
        gr = gm[m:m + 1, :]
        beats = (gr > gm) | ((gr == gm) & (m < n_iota))
        rank = rank + beats.astype(jnp.int32)
    return ((rank < topk) & valid).astype(F32)


def _moba_prompt_body(q_ref, k_ref, v_ref, o_ref, kmean_s, kh_s, vt_s, qt_s, sel_s, ml_s, acc_s, ot_s, *,
                      nb, blk, heads, dh, topk):
    i = pl.program_id(1)

    @pl.when(i == 0)
    def _():
        for n in range(nb):
            rows = slice(n * blk, (n + 1) * blk)
            kmean_s[n:n + 1, :] = jnp.mean(k_ref[0, rows, :], axis=0, keepdims=True)
            vt_s[:, rows] = v_ref[0, rows, :].T.astype(BF16)
        for h in range(heads):
            kh_s[h] = k_ref[0, :, h * dh:(h + 1) * dh].astype(BF16)

    qt = q_ref[0].T
    krow = lax.broadcasted_iota(jnp.int32, (blk, blk), 0)
    qcol = lax.broadcasted_iota(jnp.int32, (blk, blk), 1)
    dmat = (qcol - krow).astype(F32)
    n_iota = lax.broadcasted_iota(jnp.int32, (nb, blk), 0)
    slopes = [2.0 ** (-8.0 * (h + 1) / heads) for h in range(heads)]
    head_rows = [slice(h * dh, (h + 1) * dh) for h in range(heads)]
    for h in range(heads):
        gate_t = _dot(kmean_s[:, head_rows[h]], qt[head_rows[h]], HI)
        sel_s[h] = _topk_rows(gate_t, n_iota < i, n_iota, nb, topk)
    qt_s[...] = (qt * dh ** -0.5).astype(BF16)

    def block_step(kstart, logits_fn, first):
        s_all = [_dot(kh_s[h, pl.ds(kstart, blk), :], qt_s[head_rows[h], :]) for h in range(heads)]
        for h in range(heads):
            logits = logits_fn(h, s_all[h])
            vt = vt_s[head_rows[h], pl.ds(kstart, blk)]
            m_row, l_row = ml_s.at[2 * h:2 * h + 1, :], ml_s.at[2 * h + 1:2 * h + 2, :]
            if first:
                m_new = jnp.max(logits, axis=0, keepdims=True)
                p = jnp.exp(logits - m_new)
                acc_s[h] = _dot(vt, p.astype(BF16))
                l_row[...] = jnp.sum(p, axis=0, keepdims=True)
            else:
                m = m_row[...]
                m_new = jnp.maximum(m, jnp.max(logits, axis=0, keepdims=True))
                alpha = jnp.exp(m - m_new)
                p = jnp.exp(logits - m_new)
                acc_s[h] = alpha * acc_s[h] + _dot(vt, p.astype(BF16))
                l_row[...] = alpha * l_row[...] + jnp.sum(p, axis=0, keepdims=True)
            m_row[...] = m_new

    block_step(pl.multiple_of(i * blk, blk), lambda h, s: jnp.where(dmat >= 0, s - slopes[h] * dmat, NEG), True)

    def body(j, carry):
        dist = dmat + ((i - j) * blk).astype(F32)
        pick = n_iota == j

        def logits_fn(h, s):
            selrow = jnp.sum(jnp.where(pick, sel_s[h], 0.0), axis=0, keepdims=True)
            return jnp.where(selrow > 0.5, s - slopes[h] * dist, NEG)

        block_step(pl.multiple_of(j * blk, blk), logits_fn, False)
        return carry

    lax.fori_loop(0, i, body, 0)
    for h in range(heads):
        ot_s[head_rows[h], :] = acc_s[h] / ml_s[2 * h + 1:2 * h + 2, :]
    o_ref[0] = ot_s[...].T


def moba_prompt(q, k, v):
    b, l, w = q.shape
    blk = MOBA_BLOCK
    nb = l // blk
    heads, dh = A_HEADS, A_HEAD_DIM
    return pl.pallas_call(
        functools.partial(_moba_prompt_body, nb=nb, blk=blk, heads=heads, dh=dh, topk=MOBA_TOPK),
        grid=(b, nb),
        in_specs=[pl.BlockSpec((1, blk, w), lambda bi, i: (bi, i, 0)),
                  pl.BlockSpec((1, l, w), lambda bi, i: (bi, 0, 0)),
                  pl.BlockSpec((1, l, w), lambda bi, i: (bi, 0, 0))],
        out_specs=pl.BlockSpec((1, blk, w), lambda bi, i: (bi, i, 0)),
        out_shape=jax.ShapeDtypeStruct((b, l, w), F32),
        scratch_shapes=[pltpu.VMEM((nb, w), F32), pltpu.VMEM((heads, l, dh), BF16), pltpu.VMEM((w, l), BF16),
                        pltpu.VMEM((w, blk), BF16), pltpu.VMEM((heads, nb, blk), F32),
                        pltpu.VMEM((2 * heads, blk), F32), pltpu.VMEM((heads, dh, blk), F32),
                        pltpu.VMEM((w, blk), F32)],
        compiler_params=_params("parallel", "arbitrary"),
        name="moba_prompt",
    )(q, k, v)


def _page_specs(layer, width, n):
    return [pl.BlockSpec((None, 1, PAGE_SIZE, width), lambda b, c, pt, u=u: (layer, pt[b, c * n + u], 0, 0))
            for u in range(n)]


def _moba_scores_body(pt_ref, qbd_ref, *rest, n_u):
    k_refs, (st_ref, bsum_ref, kbuf) = rest[:n_u], rest[n_u:]
    ppb = MOBA_BLOCK // PAGE_SIZE
    for n in range(n_u // ppb):
        bsum = None
        for u in range(n * ppb, (n + 1) * ppb):
            kp = k_refs[u][0]
            psum = jnp.sum(kp, axis=0, keepdims=True)
            bsum = psum if bsum is None else bsum + psum
            kbuf[u * PAGE_SIZE:(u + 1) * PAGE_SIZE, :] = kp.astype(BF16)
        bsum_ref[0, n:n + 1, :] = bsum
    st_ref[0] = _nt(qbd_ref[0], kbuf[...])


def _moba_select_body(bsum_ref, qf_ref, st_ref, knew_ref, e_ref, p_ref, pown_ref, *, nb, lq, heads, dh, topk, past):
    r_n = heads * lq
    kmean = bsum_ref[0] * (1.0 / MOBA_BLOCK)
    qf = qf_ref[0]
    gate = _nt(qf, kmean, HI)
    n_iota = lax.broadcasted_iota(jnp.int32, (r_n, nb), 1)
    sel = _topk_mask(gate, n_iota >= 0, n_iota, nb, topk)
    mask = _dot(sel.astype(BF16), e_ref[...])
    r1 = lax.broadcasted_iota(jnp.int32, (r_n, 1), 0)
    qpos = r1 % lq
    slope = jnp.exp2((-8.0 / heads) * ((r1 // lq) + 1).astype(F32))
    t_n = st_ref.shape[2]
    kpos = lax.broadcasted_iota(jnp.int32, (r_n, t_n), 1)
    dist = ((past + qpos) - kpos).astype(F32)
    scale = dh ** -0.5
    s = st_ref[0] * scale - slope * dist
    s = jnp.where(mask > 0.5, s, NEG)
    knew = knew_ref[0]
    s_own = []
    for t in range(lq):
        st = jnp.sum(qf * knew[t:t + 1, :], axis=1, keepdims=True) * scale - slope * (qpos - t).astype(F32)
        s_own.append(jnp.where(qpos >= t, st, NEG))
    m = jnp.max(s, axis=1, keepdims=True)
    for st in s_own:
        m = jnp.maximum(m, st)
    p = jnp.exp(s - m)
    l = jnp.sum(p, axis=1, keepdims=True)
    p_own = [jnp.exp(st - m) for st in s_own]
    for po in p_own:
        l = l + po
    inv = 1.0 / l
    p_ref[0] = (p * inv).astype(BF16)
    lane = lax.broadcasted_iota(jnp.int32, (r_n, LANES), 1)
    acc = jnp.zeros((r_n, LANES), F32)
    for t, po in enumerate(p_own):
        acc = acc + jnp.where(lane == t, po * inv, 0.0)
    pown_ref[0] = acc


def _moba_pv_body(pt_ref, p_ref, pown_ref, vnew_ref, *rest, n_u, lq, dh):
    v_refs, (o_ref, acc_ref, vbuf) = rest[:n_u], rest[n_u:]
    c = pl.program_id(1)

    @pl.when(c == 0)
    def _():
        pown = pown_ref[0]
        vnew = vnew_ref[0]
        acc = pown[:, 0:1] * vnew[0:1, :]
        for t in range(1, lq):
            acc = acc + pown[:, t:t + 1] * vnew[t:t + 1, :]
        acc_ref[...] = acc

    for u in range(n_u):
        vbuf[u * PAGE_SIZE:(u + 1) * PAGE_SIZE, :] = v_refs[u][0].astype(BF16)
    acc_ref[...] += _dot(p_ref[0], vbuf[...])

    @pl.when(c == pl.num_programs(1) - 1)
    def _():
        r_n, w = acc_ref.shape
        r = lax.broadcasted_iota(jnp.int32, (r_n, w), 0)
        cc = lax.broadcasted_iota(jnp.int32, (r_n, w), 1)
        own_head = jnp.where((r // lq) == (cc // dh), acc_ref[...], 0.0)
        qi = lax.broadcasted_iota(jnp.int32, (8, r_n), 0)
        ri = lax.broadcasted_iota(jnp.int32, (8, r_n), 1)
        pick = ((ri % lq) == qi).astype(F32)
        o_ref[0] = _dot(pick, own_head, HI)


def moba_sample(q, k_new, v_new, cache_k, cache_v, layer, page_table):
    b, lq, w = q.shape
    n_pages = page_table.shape[1]
    past = n_pages * PAGE_SIZE
    nb = past // MOBA_BLOCK
    heads, dh = A_HEADS, A_HEAD_DIM
    r_n = heads * lq
    n_u = MOBA_PAGES_PER_STEP
    n_c = n_pages // n_u
    ppb = MOBA_BLOCK // PAGE_SIZE
    head_of_col = jnp.arange(w) // dh
    qbd = q[:, None, :, :] * (head_of_col[None, None, None, :] == jnp.arange(heads)[None, :, None, None]).astype(F32)
    qbd = qbd.reshape(b, r_n, w)
    pad8 = lambda a: jnp.pad(a, ((0, 0), (0, 8 - lq), (0, 0)))
    expand = (jnp.arange(past)[None, :] // MOBA_BLOCK == jnp.arange(nb)[:, None]).astype(BF16)

    st, bsum = pl.pallas_call(
        functools.partial(_moba_scores_body, n_u=n_u),
        grid_spec=pltpu.PrefetchScalarGridSpec(
            num_scalar_prefetch=1, grid=(b, n_c),
            in_specs=[pl.BlockSpec((1, r_n, w), lambda bi, c, pt: (bi, 0, 0))] + _page_specs(layer, w, n_u),
            out_specs=[pl.BlockSpec((1, r_n, n_u * PAGE_SIZE), lambda bi, c, pt: (bi, 0, c)),
                       pl.BlockSpec((1, n_u // ppb, w), lambda bi, c, pt: (bi, c, 0))],
            scratch_shapes=[pltpu.VMEM((n_u * PAGE_SIZE, w), BF16)]),
        out_shape=[jax.ShapeDtypeStruct((b, r_n, past), F32), jax.ShapeDtypeStruct((b, nb, w), F32)],
        compiler_params=_params("parallel", "arbitrary"),
        name="moba_sample_scores",
    )(page_table, qbd.astype(BF16), *([cache_k] * n_u))

    p, pown = pl.pallas_call(
        functools.partial(_moba_select_body, nb=nb, lq=lq, heads=heads, dh=dh, topk=MOBA_TOPK, past=past),
        grid=(b,),
        in_specs=[pl.BlockSpec((1, nb, w), lambda bi: (bi, 0, 0)),
                  pl.BlockSpec((1, r_n, w), lambda bi: (bi, 0, 0)),
                  pl.BlockSpec((1, r_n, past), lambda bi: (bi, 0, 0)),
                  pl.BlockSpec((1, 8, w), lambda bi: (bi, 0, 0)),
                  pl.BlockSpec((nb, past), lambda bi: (0, 0))],
        out_specs=[pl.BlockSpec((1, r_n, past), lambda bi: (bi, 0, 0)),
                   pl.BlockSpec((1, r_n, LANES), lambda bi: (bi, 0, 0))],
        out_shape=[jax.ShapeDtypeStruct((b, r_n, past), BF16), jax.ShapeDtypeStruct((b, r_n, LANES), F32)],
        compiler_params=_params("parallel"),
        name="moba_sample_select",
    )(bsum, qbd, st, pad8(k_new), expand)

    o = pl.pallas_call(
        functools.partial(_moba_pv_body, n_u=n_u, lq=lq, dh=dh),
        grid_spec=pltpu.PrefetchScalarGridSpec(
            num_scalar_prefetch=1, grid=(b, n_c),
            in_specs=[pl.BlockSpec((1, r_n, n_u * PAGE_SIZE), lambda bi, c, pt: (bi, 0, c)),
                      pl.BlockSpec((1, r_n, LANES), lambda bi, c, pt: (bi, 0, 0)),
                      pl.BlockSpec((1, 8, w), lambda bi, c, pt: (bi, 0, 0))] + _page_specs(layer, w, n_u),
            out_specs=pl.BlockSpec((1, 8, w), lambda bi, c, pt: (bi, 0, 0)),
            scratch_shapes=[pltpu.VMEM((r_n, w), F32), pltpu.VMEM((n_u * PAGE_SIZE, w), BF16)]),
        out_shape=jax.ShapeDtypeStruct((b, 8, w), F32),
        compiler_params=_params("parallel", "arbitrary"),
        name="moba_sample_pv",
    )(page_table, p, pown, pad8(v_new), *([cache_v] * n_u))
    return o[:, :lq, :]


def _log_sigmoid(x):
    return -(jnp.maximum(-x, 0.0) + jnp.log1p(jnp.exp(-jnp.abs(x))))


def _mlstm_body(qk_ref, v_ref, og_ref, gcol_ref, grow_ref, bcol_ref, brow_ref, cw_ref, gh_ref, cbuf_ref,
                c0_ref, n0_ref, m0_ref, h_ref, c_out, n_out, m_out, xbuf, c_s, n_s, m_s, *, ch, valid, heads, dh):
    c = pl.program_id(1)
    width = heads * dh

    @pl.when(c == 0)
    def _():
        xbuf[0:8, :] = cbuf_ref[0]
        c_s[...] = c0_ref[0]
        n_s[...] = n0_ref[0]
        m_s[...] = m0_ref[0]

    xbuf[8:8 + ch, :] = qk_ref[0]
    cw = cw_ref[...]
    back = CONV_W - 1
    y = xbuf[8 - back:8 - back + ch, :] * cw[0:1, :]
    for t in range(1, CONV_W):
        y = y + xbuf[8 - back + t:8 - back + t + ch, :] * cw[t:t + 1, :]
    xbuf[8 - back:8, :] = xbuf[8 + ch - back:8 + ch, :]
    y = y * _sigmoid(y)

    t_col = lax.broadcasted_iota(jnp.int32, (ch, LANES), 0)
    t_row = lax.broadcasted_iota(jnp.int32, (8, ch), 1)
    gcol = gcol_ref[0] + bcol_ref[...]
    grow = grow_ref[0, 0] + brow_ref[:, :ch]
    ig_col = jnp.where(t_col < valid, gcol, NEG)
    lf_col = jnp.where(t_col < valid, _log_sigmoid(gcol), 0.0)
    ig_row = jnp.where(t_row < valid, grow, NEG)
    lf_row = jnp.where(t_row < valid, _log_sigmoid(grow), 0.0)
    ti = lax.broadcasted_iota(jnp.int32, (ch, ch), 0)
    si = lax.broadcasted_iota(jnp.int32, (ch, ch), 1)
    causal = ti >= si
    b_col = _dot(causal.astype(F32), lf_col, HI)
    b_row = _dot(lf_row, (ti <= si).astype(F32), HI)

    v = v_ref[0]
    og = og_ref[0]
    gh = gh_ref[...]
    qhs = [y[:, h * dh:(h + 1) * dh] for h in range(heads)]
    khs = [y[:, width + h * dh:width + (h + 1) * dh] * (dh ** -0.5) for h in range(heads)]
    qk_all = [_nt(qhs[h].astype(BF16), khs[h].astype(BF16)) for h in range(heads)]
    qc_all = [_dot(qhs[h].astype(BF16), c_s[h].astype(BF16)) for h in range(heads)]
    for h in range(heads):
        sl = slice(h * dh, (h + 1) * dh)
        qh, kh = qhs[h], khs[h]
        vh = v[:, sl]
        igc, bc = ig_col[:, h:h + 1], b_col[:, heads + h:heads + h + 1]
        igr, br = ig_row[h:h + 1, :], b_row[heads + h:heads + h + 1, :]
        m_prev = m_s[h][:, 0:1]
        c_prev = c_s[h]
        n_prev = n_s[h]
        log_d = jnp.where(causal, bc - br + igr, NEG)
        inter = bc + m_prev
        mt = jnp.maximum(inter, jnp.max(log_d, axis=1, keepdims=True))
        d = jnp.exp(log_d - mt)
        w_inter = jnp.exp(inter - mt)
        vb = vh.astype(BF16)
        a = qk_all[h] * d
        num = _dot(a.astype(BF16), vb) + w_inter * qc_all[h]
        den = jnp.sum(a, axis=1, keepdims=True) + w_inter * jnp.sum(qh * n_prev, axis=1, keepdims=True)
        hh = num / jnp.maximum(jnp.abs(den), jnp.exp(-mt))
        b_last = bc[ch - 1:ch, :]
        logw = b_last - bc + igc
        m_new = jnp.maximum(b_last + m_prev, jnp.max(logw, axis=0, keepdims=True))
        ws = jnp.exp(logw - m_new)
        decay = jnp.exp(b_last + m_prev - m_new)
        kw = kh * ws
        c_s[h] = decay * c_prev + lax.dot_general(kw.astype(BF16), vb, TN_DIMS, preferred_element_type=F32)
        n_s[h] = decay * n_prev + jnp.sum(kw, axis=0, keepdims=True)
        m_s[h] = jnp.broadcast_to(m_new, (1, LANES))
        hn = _rms(hh, gh[:, sl])
        h_ref[0, :, sl] = hn * _sigmoid(og[:, sl])

    @pl.when(c == pl.num_programs(1) - 1)
    def _():
        c_out[0] = c_s[...]
        n_out[0] = n_s[...]
        m_out[0] = m_s[...]


def mlstm_layer(qk, v, og, gates, b_g, conv_w, g_h, conv_buf, c0, n0, m0, valid):
    b, lp, w2 = qk.shape
    w = w2 // 2
    heads, dh, ch = B_HEADS, B_HEAD_DIM, MLSTM_CHUNK
    nc = lp // ch
    assert valid == lp or nc == 1
    grow = gates[:, :, :8].reshape(b, nc, ch, 8).transpose(0, 1, 3, 2)
    bcol = jnp.pad(b_g, (0, LANES - 8)).reshape(1, LANES)
    brow = jnp.broadcast_to(b_g[:, None], (8, LANES))
    cw = jnp.pad(conv_w, ((0, 8 - CONV_W), (0, 0)))
    cbuf = jnp.pad(conv_buf, ((0, 0), (8 - (CONV_W - 1), 0), (0, 0)))
    n0 = n0.reshape(b, heads, 1, dh)
    m0 = jnp.broadcast_to(m0[:, :, None, None], (b, heads, 1, LANES))
    full = lambda shape: pl.BlockSpec(shape, lambda bi, c: (0,) * len(shape))
    per_b = lambda shape: pl.BlockSpec(shape, lambda bi, c: (bi,) + (0,) * (len(shape) - 1))
    seq = lambda width: pl.BlockSpec((1, ch, width), lambda bi, c: (bi, c, 0))
    h, c_f, n_f, m_f = pl.pallas_call(
        functools.partial(_mlstm_body, ch=ch, valid=valid if nc == 1 else ch, heads=heads, dh=dh),
        grid=(b, nc),
        in_specs=[seq(w2), seq(w), seq(w), seq(LANES),
                  pl.BlockSpec((1, 1, 8, ch), lambda bi, c: (bi, c, 0, 0)),
                  full((1, LANES)), full((8, LANES)), full((8, w2)), full((1, w)),
                  per_b((1, 8, w2)), per_b((1, heads, dh, dh)), per_b((1, heads, 1, dh)), per_b((1, heads, 1, LANES))],
        out_specs=[seq(w), per_b((1, heads, dh, dh)), per_b((1, heads, 1, dh)), per_b((1, heads, 1, LANES))],
        out_shape=[jax.ShapeDtypeStruct((b, lp, w), F32), jax.ShapeDtypeStruct((b, heads, dh, dh), F32),
                   jax.ShapeDtypeStruct((b, heads, 1, dh), F32), jax.ShapeDtypeStruct((b, heads, 1, LANES), F32)],
        scratch_shapes=[pltpu.VMEM((8 + ch, w2), F32), pltpu.VMEM((heads, dh, dh), F32),
                        pltpu.VMEM((heads, 1, dh), F32), pltpu.VMEM((heads, 1, LANES), F32)],
        compiler_params=_params("parallel", "arbitrary"),
        name="mlstm",
    )(qk, v, og, gates, grow, bcol, brow, cw, g_h.reshape(1, w), cbuf, c0, n0, m0)
    return h, c_f, n_f.reshape(b, heads, dh), m_f[:, :, 0, 0]


def _mla_prep_body(x_ref, g_ref, win_ref, gq_ref, gkv_ref, wnt_ref, wat_ref, wbt_ref, wkb_ref,
                   cosk_ref, sink_ref, cost_ref, sint_ref,
                   ckv_ref, kpe_ref, kcat_ref, kvt_ref, qt_ref, *, heads):
    xn = _rms(x_ref[...], g_ref[...]).astype(BF16)
    y = _dot(xn, win_ref[...])
    ckv = _rms(y[:, Q_LORA:Q_LORA + KV_LORA], gkv_ref[...])
    k0 = Q_LORA + KV_LORA
    kpe = y[:, k0:k0 + ROPE_DIM] * cosk_ref[...] + y[:, k0 + ROPE_DIM:k0 + 2 * ROPE_DIM] * sink_ref[...]
    ckv_ref[...] = ckv
    kpe_ref[...] = kpe
    kcat_ref[:, 0:KV_LORA] = ckv.astype(BF16)
    kvt_ref[...] = ckv.T.astype(BF16)
    kcat_ref[:, KV_LORA:KV_LORA + ROPE_DIM] = kpe.astype(BF16)
    cqt = _rms(y[:, :Q_LORA], gq_ref[...]).T.astype(BF16)
    nope_t = _dot(wnt_ref[...], cqt)
    qpe_t = _dot(wat_ref[...], cqt) * cost_ref[...] + _dot(wbt_ref[...], cqt) * sint_ref[...]
    scale = (NOPE_DIM + ROPE_DIM) ** -0.5
    for h in range(heads):
        lat_t = _dot(wkb_ref[h], nope_t[h * NOPE_DIM:(h + 1) * NOPE_DIM].astype(BF16))
        qt_ref[h, 0:KV_LORA, :] = (lat_t * scale).astype(BF16)
        qt_ref[h, KV_LORA:KV_LORA + ROPE_DIM, :] = (qpe_t[h * ROPE_DIM:(h + 1) * ROPE_DIM] * scale).astype(BF16)


def _rot_half(wpe):
    half = wpe.shape[-1] // 2
    return jnp.concatenate([-wpe[..., half:], wpe[..., :half]], axis=-1)


def mla_prep(x, gain, w_in, g_q, g_kv, w_qb, w_kb, pos, tm=512):
    m, d = x.shape
    heads = C_HEADS
    tm = min(tm, m)
    half = ROPE_DIM // 2
    freqs = ROPE_THETA ** (-jnp.arange(half, dtype=F32) / half)
    ang = pos.astype(F32)[:, None] * freqs
    cos_k = jnp.tile(jnp.cos(ang), (1, 2))
    sin_k = jnp.tile(jnp.sin(ang), (1, 2))
    cos_t = jnp.tile(cos_k, (1, heads)).T
    sin_t = jnp.tile(sin_k, (1, heads)).T
    k0 = Q_LORA + KV_LORA
    win = jnp.concatenate([w_in, _rot_half(w_in[:, k0:k0 + ROPE_DIM]),
                           jnp.zeros((d, 512 - k0 - 2 * ROPE_DIM), F32)], axis=1).astype(BF16)
    wq = w_qb.reshape(Q_LORA, heads, NOPE_DIM + ROPE_DIM)
    wnt = wq[:, :, :NOPE_DIM].reshape(Q_LORA, heads * NOPE_DIM).T.astype(BF16)
    wat = wq[:, :, NOPE_DIM:].reshape(Q_LORA, heads * ROPE_DIM).T.astype(BF16)
    wbt = _rot_half(wq[:, :, NOPE_DIM:]).reshape(Q_LORA, heads * ROPE_DIM).T.astype(BF16)
    wkb = w_kb.transpose(1, 0, 2).astype(BF16)
    full = lambda a: pl.BlockSpec(a.shape, lambda i: (0,) * a.ndim)
    rows = lambda width: pl.BlockSpec((tm, width), lambda i: (i, 0))
    cols = lambda height: pl.BlockSpec((height, tm), lambda i: (0, i))
    dcat = KV_LORA + ROPE_DIM
    gq, gkv, gm = g_q.reshape(1, -1), g_kv.reshape(1, -1), gain.reshape(1, -1)
    return pl.pallas_call(
        functools.partial(_mla_prep_body, heads=heads),
        grid=(m // tm,),
        in_specs=[rows(d), full(gm), full(win), full(gq), full(gkv), full(wnt), full(wat), full(wbt), full(wkb),
                  rows(ROPE_DIM), rows(ROPE_DIM), cols(heads * ROPE_DIM), cols(heads * ROPE_DIM)],
        out_specs=[rows(KV_LORA), rows(ROPE_DIM), rows(dcat), cols(KV_LORA),
                   pl.BlockSpec((heads, dcat, tm), lambda i: (0, 0, i))],
        out_shape=[jax.ShapeDtypeStruct((m, KV_LORA), F32), jax.ShapeDtypeStruct((m, ROPE_DIM), F32),
                   jax.ShapeDtypeStruct((m, dcat), BF16), jax.ShapeDtypeStruct((KV_LORA, m), BF16),
                   jax.ShapeDtypeStruct((heads, dcat, m), BF16)],
        compiler_params=_params("parallel"),
        name="mla_prep",
    )(x, gm, win, gq, gkv, wnt, wat, wbt, wkb, cos_k, sin_k, cos_t, sin_t)


def _mla_flash_body(qt_ref, k_ref, kvt_ref, wvbt_ref, o_ref, ml_s, acc_s, ot_s, *, heads, tq):
    i = pl.program_id(1)
    krow = lax.broadcasted_iota(jnp.int32, (tq, tq), 0)
    qcol = lax.broadcasted_iota(jnp.int32, (tq, tq), 1)
    causal = krow <= qcol

    def tile_step(kstart, first):
        k = k_ref[pl.ds(kstart, tq), :]
        vt = kvt_ref[:, pl.ds(kstart, tq)]
        s_all = [_dot(k, qt_ref[h]) for h in range(heads)]
        for h in range(heads):
            m_row, l_row = ml_s.at[2 * h:2 * h + 1, :], ml_s.at[2 * h + 1:2 * h + 2, :]
            if first:
                s = jnp.where(causal, s_all[h], NEG)
                m_new = jnp.max(s, axis=0, keepdims=True)
                p = jnp.exp(s - m_new)
                acc_s[h] = _dot(vt, p.astype(BF16))
                l_row[...] = jnp.sum(p, axis=0, keepdims=True)
            else:
                s = s_all[h]
                m = m_row[...]
                m_new = jnp.maximum(m, jnp.max(s, axis=0, keepdims=True))
                alpha = jnp.exp(m - m_new)
                p = jnp.exp(s - m_new)
                acc_s[h] = alpha * acc_s[h] + _dot(vt, p.astype(BF16))
                l_row[...] = alpha * l_row[...] + jnp.sum(p, axis=0, keepdims=True)
            m_row[...] = m_new

    tile_step(pl.multiple_of(i * tq, tq), True)

    def body(j, carry):
        tile_step(pl.multiple_of(j * tq, tq), False)
        return carry

    lax.fori_loop(0, i, body, 0)
    for h in range(heads):
        o_lat = (acc_s[h] / ml_s[2 * h + 1:2 * h + 2, :]).astype(BF16)
        ot_s[h * V_DIM:(h + 1) * V_DIM, :] = _dot(wvbt_ref[h], o_lat)
    o_ref[...] = ot_s[...].T


def mla_flash_prompt(qt, kcat, kvt, wvbt, batch, tq=256):
    heads, dcat, m = qt.shape
    l = m // batch
    nq = l // tq
    return pl.pallas_call(
        functools.partial(_mla_flash_body, heads=heads, tq=tq),
        grid=(batch, nq),
        in_specs=[pl.BlockSpec((heads, dcat, tq), lambda b, i: (0, 0, b * nq + i)),
                  pl.BlockSpec((l, dcat), lambda b, i: (b, 0)),
                  pl.BlockSpec((KV_LORA, l), lambda b, i: (0, b)),
                  pl.BlockSpec(wvbt.shape, lambda b, i: (0, 0, 0))],
        out_specs=pl.BlockSpec((tq, heads * V_DIM), lambda b, i: (b * nq + i, 0)),
        out_shape=jax.ShapeDtypeStruct((m, heads * V_DIM), F32),
        scratch_shapes=[pltpu.VMEM((2 * heads, tq), F32), pltpu.VMEM((heads, KV_LORA, tq), F32),
                        pltpu.VMEM((heads * V_DIM, tq), F32)],
        compiler_params=_params("parallel", "arbitrary"),
        name="mla_flash_prompt",
    )(qt, kcat, kvt, wvbt)


def _mla_sample_body(pt_ref, q_ref, knew_ref, *rest, n_u, lq):
    kv_refs, pe_refs = rest[:n_u], rest[n_u:2 * n_u]
    o_ref, m_s, l_s, acc_s, kbuf = rest[2 * n_u:]
    c = pl.program_id(1)
    q = q_ref[0]

    @pl.when(c == 0)
    def _():
        kn = knew_ref[0]
        s = _nt(q, kn)
        t = lax.broadcasted_iota(jnp.int32, s.shape, 1)
        qpos = lax.broadcasted_iota(jnp.int32, s.shape, 0) % lq
        s = jnp.where(t <= qpos, s, NEG)
        m0 = jnp.max(s, axis=1, keepdims=True)
        p = jnp.exp(s - m0)
        m_s[...] = m0
        l_s[...] = jnp.sum(p, axis=1, keepdims=True)
        acc_s[...] = _dot(p.astype(BF16), kn[:, :KV_LORA])

    for u in range(n_u):
        rows = slice(u * PAGE_SIZE, (u + 1) * PAGE_SIZE)
        kbuf[rows, 0:KV_LORA] = kv_refs[u][0].astype(BF16)
        kbuf[rows, KV_LORA:KV_LORA + ROPE_DIM] = pe_refs[u][0].astype(BF16)
    s = _nt(q, kbuf[...])
    m_prev = m_s[...]
    m_new = jnp.maximum(m_prev, jnp.max(s, axis=1, keepdims=True))
    alpha = jnp.exp(m_prev - m_new)
    p = jnp.exp(s - m_new)
    l_s[...] = alpha * l_s[...] + jnp.sum(p, axis=1, keepdims=True)
    acc_s[...] = alpha * acc_s[...] + _dot(p.astype(BF16), kbuf[:, 0:KV_LORA])
    m_s[...] = m_new

    @pl.when(c == pl.num_programs(1) - 1)
    def _():
        o_ref[0] = acc_s[...] / l_s[...]


def mla_sample(qt, kcat_new, cache_kv, cache_pe, layer, page_table, lq):
    heads, dcat, m = qt.shape
    b = m // lq
    n_pages = page_table.shape[1]
    n_u = PAGES_PER_STEP
    r_n = heads * lq
    q = qt.reshape(heads, dcat, b, lq).transpose(2, 0, 3, 1).reshape(b, r_n, dcat)
    knew = jnp.pad(kcat_new.reshape(b, lq, dcat), ((0, 0), (0, PAGE_SIZE - lq), (0, 0)))
    kv_specs = [pl.BlockSpec((None, 1, PAGE_SIZE, KV_LORA), lambda bi, c, pt, u=u: (layer, pt[bi, c * n_u + u], 0, 0))
                for u in range(n_u)]
    pe_specs = [pl.BlockSpec((None, 1, PAGE_SIZE, ROPE_DIM), lambda bi, c, pt, u=u: (layer, pt[bi, c * n_u + u], 0, 0))
                for u in range(n_u)]
    o = pl.pallas_call(
        functools.partial(_mla_sample_body, n_u=n_u, lq=lq),
        grid_spec=pltpu.PrefetchScalarGridSpec(
            num_scalar_prefetch=1, grid=(b, n_pages // n_u),
            in_specs=[pl.BlockSpec((1, r_n, dcat), lambda bi, c, pt: (bi, 0, 0)),
                      pl.BlockSpec((1, PAGE_SIZE, dcat), lambda bi, c, pt: (bi, 0, 0))] + kv_specs + pe_specs,
            out_specs=pl.BlockSpec((1, r_n, KV_LORA), lambda bi, c, pt: (bi, 0, 0)),
            scratch_shapes=[pltpu.VMEM((r_n, 1), F32), pltpu.VMEM((r_n, 1), F32), pltpu.VMEM((r_n, KV_LORA), F32),
                            pltpu.VMEM((n_u * PAGE_SIZE, dcat), BF16)]),
        out_shape=jax.ShapeDtypeStruct((b, r_n, KV_LORA), F32),
        compiler_params=_params("parallel", "arbitrary"),
        name="mla_sample",
    )(page_table, q, knew, *([cache_kv] * n_u), *([cache_pe] * n_u))
    return o.reshape(b, heads, lq, KV_LORA).transpose(1, 0, 2, 3).reshape(heads, m, KV_LORA)


def _headproj_body(x_ref, w_ref, o_ref, *, heads):
    for h in range(heads):
        o_ref[:, h * V_DIM:(h + 1) * V_DIM] = _dot(x_ref[h].astype(BF16), w_ref[h])


def head_value_proj(o_lat, wvb):
    heads, m, _ = o_lat.shape
    return pl.pallas_call(
        functools.partial(_headproj_body, heads=heads),
        out_shape=jax.ShapeDtypeStruct((m, heads * V_DIM), F32),
        compiler_params=pltpu.CompilerParams(vmem_limit_bytes=VMEM_LIMIT),
        name="mla_value_proj",
    )(o_lat, wvb)


def _cross_body(q_ref, k_ref, v_ref, o_ref, *, heads, dh):
    q = q_ref[0]
    scale = dh ** -0.5
    for h in range(heads):
        sl = slice(h * dh, (h + 1) * dh)
        s = _nt((q[:, sl] * scale).astype(BF16), k_ref[0, :, sl].astype(BF16))
        m = jnp.max(s, axis=1, keepdims=True)
        p = jnp.exp(s - m)
        l = jnp.sum(p, axis=1, keepdims=True)
        o_ref[0, :, sl] = _dot(p.astype(BF16), v_ref[0, :, sl].astype(BF16)) / l


def cross_core(q, mk, mv, tq=512):
    b, lq, w = q.shape
    n_mem = mk.shape[1]
    tq = min(tq, lq)
    return pl.pallas_call(
        functools.partial(_cross_body, heads=X_HEADS, dh=X_HEAD_DIM),
        grid=(b, lq // tq),
        in_specs=[pl.BlockSpec((1, tq, w), lambda bi, i: (bi, i, 0)),
                  pl.BlockSpec((1, n_mem, w), lambda bi, i: (bi, 0, 0)),
                  pl.BlockSpec((1, n_mem, w), lambda bi, i: (bi, 0, 0))],
        out_specs=pl.BlockSpec((1, tq, w), lambda bi, i: (bi, i, 0)),
        out_shape=jax.ShapeDtypeStruct((b, lq, w), F32),
        compiler_params=_params("parallel", "parallel"),
        name="cross_core",
    )(q, mk, mv)


def _moe_gates(logits):
    lane = lax.broadcasted_iota(jnp.int32, logits.shape, 1).astype(F32)
    big = 1e9
    is_g = lane < N_GROUPS
    gl = jnp.where(is_g, logits, -jnp.inf)
    gmax = jnp.max(gl, axis=1, keepdims=True)
    grp = jnp.min(jnp.where(is_g & (gl == gmax), lane, big), axis=1, keepdims=True)
    p_grp = 1.0 / jnp.sum(jnp.where(is_g, jnp.exp(gl - gmax), 0.0), axis=1, keepdims=True)
    e_idx = lane - N_GROUPS
    in_grp = (e_idx >= grp * EXPERTS_PER_GROUP) & (e_idx < (grp + 1) * EXPERTS_PER_GROUP)
    el = jnp.where(in_grp, logits, -jnp.inf)
    t1 = jnp.max(el, axis=1, keepdims=True)
    i1 = jnp.min(jnp.where(in_grp & (el == t1), lane, big), axis=1, keepdims=True)
    el2 = jnp.where(lane == i1, -jnp.inf, el)
    t2 = jnp.max(el2, axis=1, keepdims=True)
    i2 = jnp.min(jnp.where(in_grp & (lane != i1) & (el2 == t2), lane, big), axis=1, keepdims=True)
    e2 = jnp.exp(t2 - t1)
    w1 = 1.0 / (1.0 + e2)
    w2 = e2 / (1.0 + e2)
    return p_grp * (jnp.where(lane == i1, w1, 0.0) + jnp.where(lane == i2, w2, 0.0))


def _moe_body(x_ref, g_ref, wr_ref, br_ref, wg_ref, wu_ref, wd_ref, o_ref, xn_s, gate_s, acc_s):
    e = pl.program_id(1)

    @pl.when(e == 0)
    def _():
        x = x_ref[...]
        xn = _rms(x, g_ref[...])
        xn_s[...] = xn.astype(BF16)
        gate_s[...] = _moe_gates(_dot(xn, wr_ref[...], HI) + br_ref[...])
        acc_s[...] = x

    xn = xn_s[...]
    lane = lax.broadcasted_iota(jnp.int32, gate_s.shape, 1)
    ge = jnp.sum(jnp.where(lane == e + N_GROUPS, gate_s[...], 0.0), axis=1, keepdims=True)
    a = _dot(xn, wg_ref[0])
    u = _dot(xn, wu_ref[0])
    hid = (a * _sigmoid(a)) * u * ge
    acc_s[...] += _dot(hid.astype(BF16), wd_ref[0])

    @pl.when(e == pl.num_programs(1) - 1)
    def _():
        o_ref[...] = acc_s[...]


def moe_layer(x, gain, w_group, b_group, w_router, b_router, w_up, w_gate, w_down, tm=1024):
    m, d = x.shape
    tm = min(tm, m)
    n_e, _, f = w_up.shape
    wr = jnp.concatenate([w_group, w_router, jnp.zeros((d, LANES - N_GROUPS - N_EXPERTS), F32)], axis=1)
    br = jnp.concatenate([b_group, b_router, jnp.zeros((LANES - N_GROUPS - N_EXPERTS,), F32)]).reshape(1, LANES)
    return pl.pallas_call(
        _moe_body,
        grid=(m // tm, n_e),
        in_specs=[pl.BlockSpec((tm, d), lambda i, e: (i, 0)),
                  pl.BlockSpec((1, d), lambda i, e: (0, 0)),
                  pl.BlockSpec((d, LANES), lambda i, e: (0, 0)),
                  pl.BlockSpec((1, LANES), lambda i, e: (0, 0)),
                  pl.BlockSpec((1, d, f), lambda i, e: (e, 0, 0)),
                  pl.BlockSpec((1, d, f), lambda i, e: (e, 0, 0)),
                  pl.BlockSpec((1, f, d), lambda i, e: (e, 0, 0))],
        out_specs=pl.BlockSpec((tm, d), lambda i, e: (i, 0)),
        out_shape=jax.ShapeDtypeStruct((m, d), F32),
        scratch_shapes=[pltpu.VMEM((tm, d), BF16), pltpu.VMEM((tm, LANES), F32), pltpu.VMEM((tm, d), F32)],
        compiler_params=_params("parallel", "arbitrary"),
        name="moe",
    )(x, gain.reshape(1, d), wr, br, w_gate, w_up, w_down)


AB_SPLITS = [(0, A_WIDTH), (A_WIDTH, A_WIDTH), (2 * A_WIDTH, A_WIDTH), (3 * A_WIDTH, 2 * B_WIDTH),
             (3 * A_WIDTH + 2 * B_WIDTH, B_WIDTH), (3 * A_WIDTH + 3 * B_WIDTH, B_WIDTH),
             (3 * A_WIDTH + 4 * B_WIDTH, LANES)]


def _even_projection(x, gain, w_in_bf):
    return fused_linear([x], [w_in_bf], splits=AB_SPLITS, gain=gain, name="in_proj_ab")


def _pad_seq(a, lp):
    return jnp.pad(a, ((0, 0), (0, lp - a.shape[1]), (0, 0)))


def kernel(x_prompt, x_sample, mem_prompt, cache_moba_k, cache_moba_v, cache_mla_kv, cache_mla_pe, state_mlstm_c, state_mlstm_n, state_mlstm_m, state_conv, cache_mem_k, cache_mem_v, page_table, norm_mix, norm_cross, norm_mem, norm_ffn, norm_final, w_in_ab, b_gates, conv_w, norm_mlstm, w_out_ab, w_in_c, norm_q_lat, norm_kv_lat, w_qb, w_kb, w_vb, w_out_c, w_cq, w_ck, w_cv, w_co, w_group, b_group, w_router, b_router, w_up, w_gate, w_down):
    bp, lp, d = x_prompt.shape
    bs, ls, _ = x_sample.shape
    depth = norm_mix.shape[0]
    mp, ms = bp * lp, bs * ls
    n_mem = mem_prompt.shape[1]
    past = page_table.shape[1] * PAGE_SIZE
    pos_p = jnp.tile(jnp.arange(lp, dtype=jnp.int32), bp)
    pos_s = jnp.tile(past + jnp.arange(ls, dtype=jnp.int32), bs)
    n_pool = cache_moba_k.shape[1]
    ck_pages = cache_moba_k.reshape(cache_moba_k.shape[0], n_pool, PAGE_SIZE, A_WIDTH)
    cv_pages = cache_moba_v.reshape(cache_moba_v.shape[0], n_pool, PAGE_SIZE, A_WIDTH)

    xp = x_prompt.reshape(mp, d)
    xs = x_sample.reshape(ms, d)
    mem = mem_prompt.reshape(bp * n_mem, d)
    ak_p, av_p, kv_p, pe_p, mc_p, mn_p, mm_p, cv_p, mk_p, mv_p = [], [], [], [], [], [], [], [], [], []
    ak_s, av_s, kv_s, pe_s, mc_s, mn_s, mm_s, cv_s = [], [], [], [], [], [], [], []
    back = CONV_W - 1
    ch = MLSTM_CHUNK
    ls_pad = -(-ls // ch) * ch
    for l in range(depth):
        j = l // 2
        if l % 2 == 0:
            n_ab = w_in_ab.shape[2]
            n_pad = AB_SPLITS[-1][0] + LANES
            w_in_bf = jnp.pad(w_in_ab[j], ((0, 0), (0, n_pad - n_ab))).astype(BF16)
            w_out_bf = w_out_ab[j].astype(BF16)
            qa, ka, va, qkb, vb, ob, gt = _even_projection(xp, norm_mix[l], w_in_bf)
            o_a = moba_prompt(qa.reshape(bp, lp, A_WIDTH), ka.reshape(bp, lp, A_WIDTH), va.reshape(bp, lp, A_WIDTH))
            qkb3 = qkb.reshape(bp, lp, 2 * B_WIDTH)
            hb, mc, mn, mm = mlstm_layer(
                qkb3, vb.reshape(bp, lp, B_WIDTH), ob.reshape(bp, lp, B_WIDTH), gt.reshape(bp, lp, LANES),
                b_gates[j], conv_w[j], norm_mlstm[j], jnp.zeros((bp, back, 2 * B_WIDTH), F32),
                jnp.zeros((bp, B_HEADS, B_HEAD_DIM, B_HEAD_DIM), F32), jnp.zeros((bp, B_HEADS, B_HEAD_DIM), F32),
                jnp.zeros((bp, B_HEADS), F32), valid=lp)
            (xp,) = fused_linear([o_a.reshape(mp, A_WIDTH), hb.reshape(mp, B_WIDTH)],
                                 [w_out_bf[:A_WIDTH], w_out_bf[A_WIDTH:]], residual=xp, name="out_proj_ab")
            ak_p.append(ka.reshape(bp, lp, A_HEADS, A_HEAD_DIM)); av_p.append(va.reshape(bp, lp, A_HEADS, A_HEAD_DIM))
            cv_p.append(qkb3[:, lp - back:, :]); mc_p.append(mc); mn_p.append(mn); mm_p.append(mm)
            qa, ka, va, qkb, vb, ob, gt = _even_projection(xs, norm_mix[l], w_in_bf)
            o_a = moba_sample(qa.reshape(bs, ls, A_WIDTH), ka.reshape(bs, ls, A_WIDTH), va.reshape(bs, ls, A_WIDTH),
                              ck_pages, cv_pages, j, page_table)
            qkb3 = qkb.reshape(bs, ls, 2 * B_WIDTH)
            hb, mc, mn, mm = mlstm_layer(
                _pad_seq(qkb3, ls_pad), _pad_seq(vb.reshape(bs, ls, B_WIDTH), ls_pad),
                _pad_seq(ob.reshape(bs, ls, B_WIDTH), ls_pad), _pad_seq(gt.reshape(bs, ls, LANES), ls_pad),
                b_gates[j], conv_w[j], norm_mlstm[j], state_conv[j],
                state_mlstm_c[j], state_mlstm_n[j], state_mlstm_m[j], valid=ls)
            (xs,) = fused_linear([o_a.reshape(ms, A_WIDTH), hb[:, :ls].reshape(ms, B_WIDTH)],
                                 [w_out_bf[:A_WIDTH], w_out_bf[A_WIDTH:]], residual=xs, name="out_proj_ab")
            ak_s.append(ka.reshape(bs, ls, A_HEADS, A_HEAD_DIM)); av_s.append(va.reshape(bs, ls, A_HEADS, A_HEAD_DIM))
            conv_all = jnp.concatenate([state_conv[j], qkb3], axis=1)
            cv_s.append(conv_all[:, ls:, :]); mc_s.append(mc); mn_s.append(mn); mm_s.append(mm)
        else:
            wvb = w_vb[j].transpose(1, 0, 2).astype(BF16)
            w_out_bf = w_out_c[j].astype(BF16)
            ckv, kpe, kcat, kvt, qt = mla_prep(xp, norm_mix[l], w_in_c[j], norm_q_lat[j], norm_kv_lat[j], w_qb[j], w_kb[j], pos_p)
            o = mla_flash_prompt(qt, kcat, kvt, w_vb[j].transpose(1, 2, 0).astype(BF16), bp)
            (xp,) = fused_linear([o], [w_out_bf], residual=xp, name="out_proj_c")
            kv_p.append(ckv.reshape(bp, lp, KV_LORA)); pe_p.append(kpe.reshape(bp, lp, ROPE_DIM))
            ckv, kpe, kcat, _, qt = mla_prep(xs, norm_mix[l], w_in_c[j], norm_q_lat[j], norm_kv_lat[j], w_qb[j], w_kb[j], pos_s)
            o_lat = mla_sample(qt, kcat, cache_mla_kv, cache_mla_pe, j, page_table, ls)
            o = head_value_proj(o_lat, wvb)
            (xs,) = fused_linear([o], [w_out_bf], residual=xs, name="out_proj_c")
            kv_s.append(ckv.reshape(bs, ls, KV_LORA)); pe_s.append(kpe.reshape(bs, ls, ROPE_DIM))
        w_ckv = jnp.concatenate([w_ck[l], w_cv[l]], axis=1).astype(BF16)
        mk, mv = fused_linear([mem], [w_ckv], splits=[(0, X_WIDTH), (X_WIDTH, X_WIDTH)], gain=norm_mem[l], name="mem_kv")
        mk_p.append(mk.reshape(bp, n_mem, X_HEADS, X_HEAD_DIM)); mv_p.append(mv.reshape(bp, n_mem, X_HEADS, X_HEAD_DIM))
        w_cq_bf, w_co_bf = w_cq[l].astype(BF16), w_co[l].astype(BF16)
        (q,) = fused_linear([xp], [w_cq_bf], gain=norm_cross[l], name="cross_q")
        o = cross_core(q.reshape(bp, lp, X_WIDTH), mk.reshape(bp, n_mem, X_WIDTH), mv.reshape(bp, n_mem, X_WIDTH))
        (xp,) = fused_linear([o.reshape(mp, X_WIDTH)], [w_co_bf], residual=xp, name="cross_out")
        (q,) = fused_linear([xs], [w_cq_bf], gain=norm_cross[l], name="cross_q")
        q8 = _pad_seq(q.reshape(bs, ls, X_WIDTH), 8)
        o = cross_core(q8, cache_mem_k[l].reshape(bs, n_mem, X_WIDTH), cache_mem_v[l].reshape(bs, n_mem, X_WIDTH))
        (xs,) = fused_linear([o[:, :ls].reshape(ms, X_WIDTH)], [w_co_bf], residual=xs, name="cross_out")
        wu, wg, wd = w_up[l].astype(BF16), w_gate[l].astype(BF16), w_down[l].astype(BF16)
        xp = moe_layer(xp, norm_ffn[l], w_group[l], b_group[l], w_router[l], b_router[l], wu, wg, wd)
        xs = moe_layer(xs, norm_ffn[l], w_group[l], b_group[l], w_router[l], b_router[l], wu, wg, wd)
    y_prompt = rmsnorm_rows(xp, norm_final).reshape(bp, lp, d)
    y_sample = rmsnorm_rows(xs, norm_final).reshape(bs, ls, d)
    return (y_prompt, y_sample,
            jnp.stack(ak_p), jnp.stack(av_p), jnp.stack(kv_p), jnp.stack(pe_p),
            jnp.stack(mc_p), jnp.stack(mn_p), jnp.stack(mm_p), jnp.stack(cv_p), jnp.stack(mk_p), jnp.stack(mv_p),
            jnp.stack(ak_s), jnp.stack(av_s), jnp.stack(kv_s), jnp.stack(pe_s),
            jnp.stack(mc_s), jnp.stack(mn_s), jnp.stack(mm_s), jnp.stack(cv_s))
```

```python
import functools
import math

import jax
import jax.numpy as jnp
from jax import lax
from jax.experimental import pallas as pl
from jax.experimental.pallas import tpu as pltpu

F32 = jnp.float32
BF16 = jnp.bfloat16
HI = lax.Precision.HIGHEST
EPS = 1e-6
NEG = -1e30
VMEM_LIMIT = 56 * 1024 * 1024
LANES = 128

PAGE_SIZE = 128
A_HEADS, A_HEAD_DIM = 8, 64
A_WIDTH = A_HEADS * A_HEAD_DIM
MOBA_BLOCK, MOBA_TOPK = 256, 3
B_HEADS, B_HEAD_DIM = 4, 128
B_WIDTH = B_HEADS * B_HEAD_DIM
CONV_W = 4
MLSTM_CHUNK = 64
C_HEADS, Q_LORA, KV_LORA, NOPE_DIM, ROPE_DIM, V_DIM = 16, 256, 128, 64, 32, 64
ROPE_THETA = 10000.0
X_HEADS, X_HEAD_DIM = 4, 128
X_WIDTH = X_HEADS * X_HEAD_DIM
N_GROUPS, EXPERTS_PER_GROUP = 4, 4
N_EXPERTS = N_GROUPS * EXPERTS_PER_GROUP
PAGES_PER_STEP = 16
MOBA_PAGES_PER_STEP = 16

NT_DIMS = (((1,), (1,)), ((), ()))
TN_DIMS = (((0,), (0,)), ((), ()))


def _params(*sem):
    return pltpu.CompilerParams(dimension_semantics=sem, vmem_limit_bytes=VMEM_LIMIT)


def _nt(a, b, precision=None):
    return lax.dot_general(a, b, NT_DIMS, precision=precision, preferred_element_type=F32)


def _dot(a, b, precision=None):
    return jnp.dot(a, b, precision=precision, preferred_element_type=F32)


def _rms(x, g):
    return x * lax.rsqrt(jnp.mean(x * x, axis=-1, keepdims=True) + EPS) * g


def _sigmoid(x):
    return 1.0 / (1.0 + jnp.exp(-x))


def _linear_body(*refs, n_in, has_gain, has_res, splits):
    x_refs, w_refs = refs[:n_in], refs[n_in:2 * n_in]
    p = 2 * n_in
    g_ref = refs[p] if has_gain else None
    p += int(has_gain)
    r_ref = refs[p] if has_res else None
    p += int(has_res)
    o_refs = refs[p:]
    xs = []
    for xr in x_refs:
        x = xr[...]
        if has_gain:
            x = _rms(x, g_ref[...])
        xs.append(x.astype(BF16))
    for (off, width), o_ref in zip(splits, o_refs):
        acc = None
        for x, wr in zip(xs, w_refs):
            y = _dot(x, wr[:, off:off + width])
            acc = y if acc is None else acc + y
        if has_res:
            acc = acc + r_ref[...]
        o_ref[...] = acc.astype(o_ref.dtype)


def fused_linear(xs, ws, splits=None, gain=None, residual=None, tm=512, name="linear"):
    m, n = xs[0].shape[0], ws[0].shape[1]
    splits = splits or [(0, n)]
    tm = min(tm, m)
    assert m % tm == 0
    in_specs = [pl.BlockSpec((tm, x.shape[1]), lambda i: (i, 0)) for x in xs]
    in_specs += [pl.BlockSpec(w.shape, lambda i: (0, 0)) for w in ws]
    args = list(xs) + list(ws)
    if gain is not None:
        in_specs.append(pl.BlockSpec((1, gain.shape[-1]), lambda i: (0, 0)))
        args.append(gain.reshape(1, -1))
    if residual is not None:
        assert len(splits) == 1
        in_specs.append(pl.BlockSpec((tm, n), lambda i: (i, 0)))
        args.append(residual)
    outs = pl.pallas_call(
        functools.partial(_linear_body, n_in=len(xs), has_gain=gain is not None, has_res=residual is not None,
                          splits=tuple(splits)),
        grid=(m // tm,),
        in_specs=in_specs,
        out_specs=[pl.BlockSpec((tm, w), lambda i: (i, 0)) for _, w in splits],
        out_shape=[jax.ShapeDtypeStruct((m, w), F32) for _, w in splits],
        compiler_params=_params("parallel"),
        name=name,
    )(*args)
    return outs


def _rmsnorm_body(x_ref, g_ref, o_ref):
    o_ref[...] = _rms(x_ref[...], g_ref[...])


def rmsnorm_rows(x, g, tm=1024):
    m, d = x.shape
    tm = min(tm, m)
    return pl.pallas_call(
        _rmsnorm_body,
        grid=(m // tm,),
        in_specs=[pl.BlockSpec((tm, d), lambda i: (i, 0)), pl.BlockSpec((1, d), lambda i: (0, 0))],
        out_specs=pl.BlockSpec((tm, d), lambda i: (i, 0)),
        out_shape=jax.ShapeDtypeStruct((m, d), F32),
        compiler_params=_params("parallel"),
        name="final_norm",
    )(x, g.reshape(1, d))


def _topk_mask(g, valid, n_iota, nb, topk):
    gm = jnp.where(valid, g, -jnp.inf)
    rank = jnp.zeros(g.shape, jnp.int32)
    for m in range(nb):
        gc = gm[:, m:m + 1]
        beats = (gc > gm) | ((gc == gm) & (m < n_iota))
        rank = rank + beats.astype(jnp.int32)
    return ((rank < topk) & valid).astype(F32)


def _topk_rows(g, valid, n_iota, nb, topk):
    gm = jnp.where(valid, g, -jnp.inf)
    rank = jnp.zeros(g.shape, jnp.int32)
    for m in range(nb):
        gr = gm[m:m + 1, :]
        beats = (gr > gm) | ((gr == gm) & (m < n_iota))
        rank = rank + beats.astype(jnp.int32)
    return ((rank < topk) & valid).astype(F32)


def _moba_prompt_body(q_ref, k_ref, v_ref, o_ref, kmean_s, kh_s, vt_s, qt_s, sel_s, ml_s, acc_s, ot_s, *,
                      nb, blk, heads, dh, topk):
    i = pl.program_id(1)

    @pl.when(i == 0)
    def _():
        for n in range(nb):
            rows = slice(n * blk, (n + 1) * blk)
            kmean_s[n:n + 1, :] = jnp.mean(k_ref[0, rows, :], axis=0, keepdims=True)
            vt_s[:, rows] = v_ref[0, rows, :].T.astype(BF16)
        for h in range(heads):
            kh_s[h] = k_ref[0, :, h * dh:(h + 1) * dh].astype(BF16)

    qt = q_ref[0].T
    krow = lax.broadcasted_iota(jnp.int32, (blk, blk), 0)
    qcol = lax.broadcasted_iota(jnp.int32, (blk, blk), 1)
    dmat = (qcol - krow).astype(F32)
    n_iota = lax.broadcasted_iota(jnp.int32, (nb, blk), 0)
    slopes = [2.0 ** (-8.0 * (h + 1) / heads) for h in range(heads)]
    head_rows = [slice(h * dh, (h + 1) * dh) for h in range(heads)]
    for h in range(heads):
        gate_t = _dot(kmean_s[:, head_rows[h]], qt[head_rows[h]], HI)
        sel_s[h] = _topk_rows(gate_t, n_iota < i, n_iota, nb, topk)
    qt_s[...] = (qt * dh ** -0.5).astype(BF16)

    def block_step(kstart, logits_fn, first):
        s_all = [_dot(kh_s[h, pl.ds(kstart, blk), :], qt_s[head_rows[h], :]) for h in range(heads)]
        for h in range(heads):
            logits = logits_fn(h, s_all[h])
            vt = vt_s[head_rows[h], pl.ds(kstart, blk)]
            m_row, l_row = ml_s.at[2 * h:2 * h + 1, :], ml_s.at[2 * h + 1:2 * h + 2, :]
            if first:
                m_new = jnp.max(logits, axis=0, keepdims=True)
                p = jnp.exp(logits - m_new)
                acc_s[h] = _dot(vt, p.astype(BF16))
                l_row[...] = jnp.sum(p, axis=0, keepdims=True)
            else:
                m = m_row[...]
                m_new = jnp.maximum(m, jnp.max(logits, axis=0, keepdims=True))
                alpha = jnp.exp(m - m_new)
                p = jnp.exp(logits - m_new)
                acc_s[h] = alpha * acc_s[h] + _dot(vt, p.astype(BF16))
                l_row[...] = alpha * l_row[...] + jnp.sum(p, axis=0, keepdims=True)
            m_row[...] = m_new

    block_step(pl.multiple_of(i * blk, blk), lambda h, s: jnp.where(dmat >= 0, s - slopes[h] * dmat, NEG), True)

    def body(j, carry):
        dist = dmat + ((i - j) * blk).astype(F32)
        pick = n_iota == j

        def logits_fn(h, s):
            selrow = jnp.sum(jnp.where(pick, sel_s[h], 0.0), axis=0, keepdims=True)
            return jnp.where(selrow > 0.5, s - slopes[h] * dist, NEG)

        block_step(pl.multiple_of(j * blk, blk), logits_fn, False)
        return carry

    lax.fori_loop(0, i, body, 0)
    for h in range(heads):
        ot_s[head_rows[h], :] = acc_s[h] / ml_s[2 * h + 1:2 * h + 2, :]
    o_ref[0] = ot_s[...].T


def moba_prompt(q, k, v):
    b, l, w = q.shape
    blk = MOBA_BLOCK
    nb = l // blk
    heads, dh = A_HEADS, A_HEAD_DIM
    return pl.pallas_call(
        functools.partial(_moba_prompt_body, nb=nb, blk=blk, heads=heads, dh=dh, topk=MOBA_TOPK),
        grid=(b, nb),
        in_specs=[pl.BlockSpec((1, blk, w), lambda bi, i: (bi, i, 0)),
                  pl.BlockSpec((1, l, w), lambda bi, i: (bi, 0, 0)),
                  pl.BlockSpec((1, l, w), lambda bi, i: (bi, 0, 0))],
        out_specs=pl.BlockSpec((1, blk, w), lambda bi, i: (bi, i, 0)),
        out_shape=jax.ShapeDtypeStruct((b, l, w), F32),
        scratch_shapes=[pltpu.VMEM((nb, w), F32), pltpu.VMEM((heads, l, dh), BF16), pltpu.VMEM((w, l), BF16),
                        pltpu.VMEM((w, blk), BF16), pltpu.VMEM((heads, nb, blk), F32),
                        pltpu.VMEM((2 * heads, blk), F32), pltpu.VMEM((heads, dh, blk), F32),
                        pltpu.VMEM((w, blk), F32)],
        compiler_params=_params("parallel", "arbitrary"),
        name="moba_prompt",
    )(q, k, v)


def _moba_bsum_body(pt_ref, *rest, n_u):
    k_refs, bsum_ref = rest[:n_u], rest[n_u]
    ppb = MOBA_BLOCK // PAGE_SIZE
    for n in range(n_u // ppb):
        acc = None
        for u in range(n * ppb, (n + 1) * ppb):
            psum = jnp.sum(k_refs[u][0], axis=0)
            acc = psum if acc is None else acc + psum
        bsum_ref[0, n] = acc


def _moba_select_body(bsum_ref, qf_ref, idx_ref, *, nb, topk):
    kmean = bsum_ref[0] * (1.0 / MOBA_BLOCK)
    gate = _nt(qf_ref[0], kmean, HI)
    r_n = gate.shape[0]
    n_iota = lax.broadcasted_iota(jnp.int32, (r_n, nb), 1)
    rank = jnp.zeros(gate.shape, jnp.int32)
    for m in range(nb):
        gc = gate[:, m:m + 1]
        rank = rank + ((gc > gate) | ((gc == gate) & (m < n_iota))).astype(jnp.int32)
    nf = n_iota.astype(F32)
    lane = lax.broadcasted_iota(jnp.int32, (r_n, LANES), 1)
    out = jnp.zeros((r_n, LANES), F32)
    for k in range(topk):
        idx_k = jnp.sum(jnp.where(rank == k, nf, 0.0), axis=1, keepdims=True)
        out = out + jnp.where(lane == k, idx_k, 0.0)
    idx_ref[0] = out.astype(jnp.int32)


def _moba_gather_body(pt_ref, sel_ref, q_ref, knew_ref, vnew_ref, k_hbm, v_hbm, o_ref, kbuf, vbuf, sem, *,
                      layer, lq, topk, heads, dh, past):
    b, h = pl.program_id(0), pl.program_id(1)
    n_h = pl.num_programs(1)
    step = b * n_h + h
    n_steps = pl.num_programs(0) * n_h
    slot = step % 2
    ppb = MOBA_BLOCK // PAGE_SIZE
    n_sel = lq * topk
    t_all = n_sel * MOBA_BLOCK

    def copies(bb, hh, sl):
        out = []
        for j in range(n_sel):
            blk = sel_ref[bb, hh * n_sel + j]
            for pg in range(ppb):
                page = pt_ref[bb, blk * ppb + pg]
                rows = pl.ds((j * ppb + pg) * PAGE_SIZE, PAGE_SIZE)
                out.append(pltpu.make_async_copy(k_hbm.at[layer, page, :, hh, :], kbuf.at[sl, rows, :], sem.at[sl, 0]))
                out.append(pltpu.make_async_copy(v_hbm.at[layer, page, :, hh, :], vbuf.at[sl, rows, :], sem.at[sl, 1]))
        return out

    @pl.when(step == 0)
    def _():
        for c in copies(b, h, slot):
            c.start()

    @pl.when(step + 1 < n_steps)
    def _():
        nxt = step + 1
        for c in copies(nxt // n_h, nxt % n_h, 1 - slot):
            c.start()

    for c in copies(b, h, slot):
        c.wait()

    q8 = q_ref[0, 0]
    scale = dh ** -0.5
    slope = jnp.exp2((-8.0 / heads) * (h + 1).astype(F32))
    s = _nt(q8.astype(BF16), kbuf[slot].astype(BF16))
    row = lax.broadcasted_iota(jnp.int32, (8, t_all), 0)
    col = lax.broadcasted_iota(jnp.int32, (8, t_all), 1)
    choice = col // MOBA_BLOCK
    kpos = jnp.zeros((8, t_all), jnp.int32)
    for j in range(n_sel):
        kpos = jnp.where(choice == j, sel_ref[b, h * n_sel + j] * MOBA_BLOCK, kpos)
    kpos = kpos + (col - choice * MOBA_BLOCK)
    dist = ((past + row) - kpos).astype(F32)
    logits = jnp.where((choice // topk) == row, s * scale - slope * dist, NEG)
    row1 = lax.broadcasted_iota(jnp.int32, (8, 1), 0)
    knew, vnew = knew_ref[0, 0], vnew_ref[0, 0]
    own = []
    for t in range(lq):
        so = jnp.sum(q8 * knew[t:t + 1, :], axis=1, keepdims=True) * scale - slope * (row1 - t).astype(F32)
        own.append(jnp.where(row1 >= t, so, NEG))
    m = jnp.max(logits, axis=1, keepdims=True)
    for so in own:
        m = jnp.maximum(m, so)
    p = jnp.exp(logits - m)
    l = jnp.sum(p, axis=1, keepdims=True)
    acc = _dot(p.astype(BF16), vbuf[slot].astype(BF16))
    for t, so in enumerate(own):
        po = jnp.exp(so - m)
        l = l + po
        acc = acc + po * vnew[t:t + 1, :]
    o_ref[0, 0] = acc / l


def moba_sample(q, k_new, v_new, cache_k, cache_v, layer, page_table):
    b, lq, w = q.shape
    heads, dh = A_HEADS, A_HEAD_DIM
    n_pages = page_table.shape[1]
    past = n_pages * PAGE_SIZE
    nb = past // MOBA_BLOCK
    assert past % MOBA_BLOCK == 0 and nb >= MOBA_TOPK and lq <= 8
    r_n = heads * lq
    n_u = MOBA_PAGES_PER_STEP
    ppb = MOBA_BLOCK // PAGE_SIZE
    page_specs = [pl.BlockSpec((None, 1, PAGE_SIZE, heads, dh),
                               lambda bi, c, pt, u=u: (layer, pt[bi, c * n_u + u], 0, 0, 0)) for u in range(n_u)]
    bsum = pl.pallas_call(
        functools.partial(_moba_bsum_body, n_u=n_u),
        grid_spec=pltpu.PrefetchScalarGridSpec(
            num_scalar_prefetch=1, grid=(b, n_pages // n_u), in_specs=page_specs,
            out_specs=pl.BlockSpec((1, n_u // ppb, heads, dh), lambda bi, c, pt: (bi, c, 0, 0))),
        out_shape=jax.ShapeDtypeStruct((b, nb, heads, dh), F32),
        compiler_params=_params("parallel", "arbitrary"),
        name="moba_sample_bsum",
    )(page_table, *([cache_k] * n_u))

    head_of_col = jnp.arange(w) // dh
    qbd = q[:, None, :, :] * (head_of_col[None, None, None, :] == jnp.arange(heads)[None, :, None, None]).astype(F32)
    sel = pl.pallas_call(
        functools.partial(_moba_select_body, nb=nb, topk=MOBA_TOPK),
        grid=(b,),
        in_specs=[pl.BlockSpec((1, nb, w), lambda bi: (bi, 0, 0)), pl.BlockSpec((1, r_n, w), lambda bi: (bi, 0, 0))],
        out_specs=pl.BlockSpec((1, r_n, LANES), lambda bi: (bi, 0, 0)),
        out_shape=jax.ShapeDtypeStruct((b, r_n, LANES), jnp.int32),
        compiler_params=_params("parallel"),
        name="moba_sample_select",
    )(bsum.reshape(b, nb, w), qbd.reshape(b, r_n, w))
    sel = sel[:, :, :MOBA_TOPK].reshape(b, r_n * MOBA_TOPK)

    def per_head(a):
        a = a.reshape(b, lq, heads, dh).transpose(0, 2, 1, 3)
        return jnp.pad(a, ((0, 0), (0, 0), (0, 8 - lq), (0, 0)))

    n_sel = lq * MOBA_TOPK
    head_spec = pl.BlockSpec((1, 1, 8, dh), lambda bi, hi, pt, sl: (bi, hi, 0, 0))
    o = pl.pallas_call(
        functools.partial(_moba_gather_body, layer=layer, lq=lq, topk=MOBA_TOPK, heads=heads, dh=dh, past=past),
        grid_spec=pltpu.PrefetchScalarGridSpec(
            num_scalar_prefetch=2, grid=(b, heads),
            in_specs=[head_spec, head_spec, head_spec,
                      pl.BlockSpec(memory_space=pl.ANY), pl.BlockSpec(memory_space=pl.ANY)],
            out_specs=head_spec,
            scratch_shapes=[pltpu.VMEM((2, n_sel * MOBA_BLOCK, dh), F32), pltpu.VMEM((2, n_sel * MOBA_BLOCK, dh), F32),
                            pltpu.SemaphoreType.DMA((2, 2))]),
        out_shape=jax.ShapeDtypeStruct((b, heads, 8, dh), F32),
        compiler_params=_params("arbitrary", "arbitrary"),
        name="moba_sample_gather",
    )(page_table, sel, per_head(q), per_head(k_new), per_head(v_new), cache_k, cache_v)
    return o[:, :, :lq, :].transpose(0, 2, 1, 3).reshape(b, lq, w)


def _log_sigmoid(x):
    return -(jnp.maximum(-x, 0.0) + jnp.log1p(jnp.exp(-jnp.abs(x))))


def _mlstm_body(qk_ref, v_ref, og_ref, gcol_ref, grow_ref, bcol_ref, brow_ref, cw_ref, gh_ref, cbuf_ref,
                c0_ref, n0_ref, m0_ref, h_ref, c_out, n_out, m_out, xbuf, c_s, n_s, m_s, *, ch, valid, heads, dh):
    c = pl.program_id(1)
    width = heads * dh

    @pl.when(c == 0)
    def _():
        xbuf[0:8, :] = cbuf_ref[0]
        c_s[...] = c0_ref[0]
        n_s[...] = n0_ref[0]
        m_s[...] = m0_ref[0]

    xbuf[8:8 + ch, :] = qk_ref[0]
    cw = cw_ref[...]
    back = CONV_W - 1
    y = xbuf[8 - back:8 - back + ch, :] * cw[0:1, :]
    for t in range(1, CONV_W):
        y = y + xbuf[8 - back + t:8 - back + t + ch, :] * cw[t:t + 1, :]
    xbuf[8 - back:8, :] = xbuf[8 + ch - back:8 + ch, :]
    y = y * _sigmoid(y)

    t_col = lax.broadcasted_iota(jnp.int32, (ch, LANES), 0)
    t_row = lax.broadcasted_iota(jnp.int32, (8, ch), 1)
    gcol = gcol_ref[0] + bcol_ref[...]
    grow = grow_ref[0, 0] + brow_ref[:, :ch]
    ig_col = jnp.where(t_col < valid, gcol, NEG)
    lf_col = jnp.where(t_col < valid, _log_sigmoid(gcol), 0.0)
    ig_row = jnp.where(t_row < valid, grow, NEG)
    lf_row = jnp.where(t_row < valid, _log_sigmoid(grow), 0.0)
    ti = lax.broadcasted_iota(jnp.int32, (ch, ch), 0)
    si = lax.broadcasted_iota(jnp.int32, (ch, ch), 1)
    causal = ti >= si
    b_col = _dot(causal.astype(F32), lf_col, HI)
    b_row = _dot(lf_row, (ti <= si).astype(F32), HI)

    v = v_ref[0]
    og = og_ref[0]
    gh = gh_ref[...]
    qhs = [y[:, h * dh:(h + 1) * dh] for h in range(heads)]
    khs = [y[:, width + h * dh:width + (h + 1) * dh] * (dh ** -0.5) for h in range(heads)]
    qk_all = [_nt(qhs[h].astype(BF16), khs[h].astype(BF16)) for h in range(heads)]
    qc_all = [_dot(qhs[h].astype(BF16), c_s[h].astype(BF16)) for h in range(heads)]
    for h in range(heads):
        sl = slice(h * dh, (h + 1) * dh)
        qh, kh = qhs[h], khs[h]
        vh = v[:, sl]
        igc, bc = ig_col[:, h:h + 1], b_col[:, heads + h:heads + h + 1]
        igr, br = ig_row[h:h + 1, :], b_row[heads + h:heads + h + 1, :]
        m_prev = m_s[h][:, 0:1]
        c_prev = c_s[h]
        n_prev = n_s[h]
        log_d = jnp.where(causal, bc - br + igr, NEG)
        inter = bc + m_prev
        mt = jnp.maximum(inter, jnp.max(log_d, axis=1, keepdims=True))
        d = jnp.exp(log_d - mt)
        w_inter = jnp.exp(inter - mt)
        vb = vh.astype(BF16)
        a = qk_all[h] * d
        num = _dot(a.astype(BF16), vb) + w_inter * qc_all[h]
        den = jnp.sum(a, axis=1, keepdims=True) + w_inter * jnp.sum(qh * n_prev, axis=1, keepdims=True)
        hh = num / jnp.maximum(jnp.abs(den), jnp.exp(-mt))
        b_last = bc[ch - 1:ch, :]
        logw = b_last - bc + igc
        m_new = jnp.maximum(b_last + m_prev, jnp.max(logw, axis=0, keepdims=True))
        ws = jnp.exp(logw - m_new)
        decay = jnp.exp(b_last + m_prev - m_new)
        kw = kh * ws
        c_s[h] = decay * c_prev + lax.dot_general(kw.astype(BF16), vb, TN_DIMS, preferred_element_type=F32)
        n_s[h] = decay * n_prev + jnp.sum(kw, axis=0, keepdims=True)
        m_s[h] = jnp.broadcast_to(m_new, (1, LANES))
        hn = _rms(hh, gh[:, sl])
        h_ref[0, :, sl] = hn * _sigmoid(og[:, sl])

    @pl.when(c == pl.num_programs(1) - 1)
    def _():
        c_out[0] = c_s[...]
        n_out[0] = n_s[...]
        m_out[0] = m_s[...]


def mlstm_layer(qk, v, og, gates, b_g, conv_w, g_h, conv_buf, c0, n0, m0, valid):
    b, lp, w2 = qk.shape
    w = w2 // 2
    heads, dh, ch = B_HEADS, B_HEAD_DIM, MLSTM_CHUNK
    nc = lp // ch
    assert valid == lp or nc == 1
    grow = gates[:, :, :8].reshape(b, nc, ch, 8).transpose(0, 1, 3, 2)
    bcol = jnp.pad(b_g, (0, LANES - 8)).reshape(1, LANES)
    brow = jnp.broadcast_to(b_g[:, None], (8, LANES))
    cw = jnp.pad(conv_w, ((0, 8 - CONV_W), (0, 0)))
    cbuf = jnp.pad(conv_buf, ((0, 0), (8 - (CONV_W - 1), 0), (0, 0)))
    n0 = n0.reshape(b, heads, 1, dh)
    m0 = jnp.broadcast_to(m0[:, :, None, None], (b, heads, 1, LANES))
    full = lambda shape: pl.BlockSpec(shape, lambda bi, c: (0,) * len(shape))
    per_b = lambda shape: pl.BlockSpec(shape, lambda bi, c: (bi,) + (0,) * (len(shape) - 1))
    seq = lambda width: pl.BlockSpec((1, ch, width), lambda bi, c: (bi, c, 0))
    h, c_f, n_f, m_f = pl.pallas_call(
        functools.partial(_mlstm_body, ch=ch, valid=valid if nc == 1 else ch, heads=heads, dh=dh),
        grid=(b, nc),
        in_specs=[seq(w2), seq(w), seq(w), seq(LANES),
                  pl.BlockSpec((1, 1, 8, ch), lambda bi, c: (bi, c, 0, 0)),
                  full((1, LANES)), full((8, LANES)), full((8, w2)), full((1, w)),
                  per_b((1, 8, w2)), per_b((1, heads, dh, dh)), per_b((1, heads, 1, dh)), per_b((1, heads, 1, LANES))],
        out_specs=[seq(w), per_b((1, heads, dh, dh)), per_b((1, heads, 1, dh)), per_b((1, heads, 1, LANES))],
        out_shape=[jax.ShapeDtypeStruct((b, lp, w), F32), jax.ShapeDtypeStruct((b, heads, dh, dh), F32),
                   jax.ShapeDtypeStruct((b, heads, 1, dh), F32), jax.ShapeDtypeStruct((b, heads, 1, LANES), F32)],
        scratch_shapes=[pltpu.VMEM((8 + ch, w2), F32), pltpu.VMEM((heads, dh, dh), F32),
                        pltpu.VMEM((heads, 1, dh), F32), pltpu.VMEM((heads, 1, LANES), F32)],
        compiler_params=_params("parallel", "arbitrary"),
        name="mlstm",
    )(qk, v, og, gates, grow, bcol, brow, cw, g_h.reshape(1, w), cbuf, c0, n0, m0)
    return h, c_f, n_f.reshape(b, heads, dh), m_f[:, :, 0, 0]


def _mla_prep_body(x_ref, g_ref, win_ref, gq_ref, gkv_ref, wnt_ref, wat_ref, wbt_ref, wkb_ref,
                   cosk_ref, sink_ref, cost_ref, sint_ref,
                   ckv_ref, kpe_ref, kcat_ref, kvt_ref, qt_ref, *, heads):
    xn = _rms(x_ref[...], g_ref[...]).astype(BF16)
    y = _dot(xn, win_ref[...])
    ckv = _rms(y[:, Q_LORA:Q_LORA + KV_LORA], gkv_ref[...])
    k0 = Q_LORA + KV_LORA
    kpe = y[:, k0:k0 + ROPE_DIM] * cosk_ref[...] + y[:, k0 + ROPE_DIM:k0 + 2 * ROPE_DIM] * sink_ref[...]
    ckv_ref[...] = ckv
    kpe_ref[...] = kpe
    kcat_ref[:, 0:KV_LORA] = ckv.astype(BF16)
    kvt_ref[...] = ckv.T.astype(BF16)
    kcat_ref[:, KV_LORA:KV_LORA + ROPE_DIM] = kpe.astype(BF16)
    cqt = _rms(y[:, :Q_LORA], gq_ref[...]).T.astype(BF16)
    nope_t = _dot(wnt_ref[...], cqt)
    qpe_t = _dot(wat_ref[...], cqt) * cost_ref[...] + _dot(wbt_ref[...], cqt) * sint_ref[...]
    scale = (NOPE_DIM + ROPE_DIM) ** -0.5
    for h in range(heads):
        lat_t = _dot(wkb_ref[h], nope_t[h * NOPE_DIM:(h + 1) * NOPE_DIM].astype(BF16))
        qt_ref[h, 0:KV_LORA, :] = (lat_t * scale).astype(BF16)
        qt_ref[h, KV_LORA:KV_LORA + ROPE_DIM, :] = (qpe_t[h * ROPE_DIM:(h + 1) * ROPE_DIM] * scale).astype(BF16)


def _rot_half(wpe):
    half = wpe.shape[-1] // 2
    return jnp.concatenate([-wpe[..., half:], wpe[..., :half]], axis=-1)


def mla_prep(x, gain, w_in, g_q, g_kv, w_qb, w_kb, pos, tm=512):
    m, d = x.shape
    heads = C_HEADS
    tm = min(tm, m)
    half = ROPE_DIM // 2
    freqs = ROPE_THETA ** (-jnp.arange(half, dtype=F32) / half)
    ang = pos.astype(F32)[:, None] * freqs
    cos_k = jnp.tile(jnp.cos(ang), (1, 2))
    sin_k = jnp.tile(jnp.sin(ang), (1, 2))
    cos_t = jnp.tile(cos_k, (1, heads)).T
    sin_t = jnp.tile(sin_k, (1, heads)).T
    k0 = Q_LORA + KV_LORA
    win = jnp.concatenate([w_in, _rot_half(w_in[:, k0:k0 + ROPE_DIM]),
                           jnp.zeros((d, 512 - k0 - 2 * ROPE_DIM), F32)], axis=1).astype(BF16)
    wq = w_qb.reshape(Q_LORA, heads, NOPE_DIM + ROPE_DIM)
    wnt = wq[:, :, :NOPE_DIM].reshape(Q_LORA, heads * NOPE_DIM).T.astype(BF16)
    wat = wq[:, :, NOPE_DIM:].reshape(Q_LORA, heads * ROPE_DIM).T.astype(BF16)
    wbt = _rot_half(wq[:, :, NOPE_DIM:]).reshape(Q_LORA, heads * ROPE_DIM).T.astype(BF16)
    wkb = w_kb.transpose(1, 0, 2).astype(BF16)
    full = lambda a: pl.BlockSpec(a.shape, lambda i: (0,) * a.ndim)
    rows = lambda width: pl.BlockSpec((tm, width), lambda i: (i, 0))
    cols = lambda height: pl.BlockSpec((height, tm), lambda i: (0, i))
    dcat = KV_LORA + ROPE_DIM
    gq, gkv, gm = g_q.reshape(1, -1), g_kv.reshape(1, -1), gain.reshape(1, -1)
    return pl.pallas_call(
        functools.partial(_mla_prep_body, heads=heads),
        grid=(m // tm,),
        in_specs=[rows(d), full(gm), full(win), full(gq), full(gkv), full(wnt), full(wat), full(wbt), full(wkb),
                  rows(ROPE_DIM), rows(ROPE_DIM), cols(heads * ROPE_DIM), cols(heads * ROPE_DIM)],
        out_specs=[rows(KV_LORA), rows(ROPE_DIM), rows(dcat), cols(KV_LORA),
                   pl.BlockSpec((heads, dcat, tm), lambda i: (0, 0, i))],
        out_shape=[jax.ShapeDtypeStruct((m, KV_LORA), F32), jax.ShapeDtypeStruct((m, ROPE_DIM), F32),
                   jax.ShapeDtypeStruct((m, dcat), BF16), jax.ShapeDtypeStruct((KV_LORA, m), BF16),
                   jax.ShapeDtypeStruct((heads, dcat, m), BF16)],
        compiler_params=_params("parallel"),
        name="mla_prep",
    )(x, gm, win, gq, gkv, wnt, wat, wbt, wkb, cos_k, sin_k, cos_t, sin_t)


def _mla_flash_body(qt_ref, k_ref, kvt_ref, wvbt_ref, o_ref, ml_s, acc_s, ot_s, *, heads, tq):
    i = pl.program_id(1)
    krow = lax.broadcasted_iota(jnp.int32, (tq, tq), 0)
    qcol = lax.broadcasted_iota(jnp.int32, (tq, tq), 1)
    causal = krow <= qcol

    def tile_step(kstart, first):
        k = k_ref[pl.ds(kstart, tq), :]
        vt = kvt_ref[:, pl.ds(kstart, tq)]
        s_all = [_dot(k, qt_ref[h]) for h in range(heads)]
        for h in range(heads):
            m_row, l_row = ml_s.at[2 * h:2 * h + 1, :], ml_s.at[2 * h + 1:2 * h + 2, :]
            if first:
                s = jnp.where(causal, s_all[h], NEG)
                m_new = jnp.max(s, axis=0, keepdims=True)
                p = jnp.exp(s - m_new)
                acc_s[h] = _dot(vt, p.astype(BF16))
                l_row[...] = jnp.sum(p, axis=0, keepdims=True)
            else:
                s = s_all[h]
                m = m_row[...]
                m_new = jnp.maximum(m, jnp.max(s, axis=0, keepdims=True))
                alpha = jnp.exp(m - m_new)
                p = jnp.exp(s - m_new)
                acc_s[h] = alpha * acc_s[h] + _dot(vt, p.astype(BF16))
                l_row[...] = alpha * l_row[...] + jnp.sum(p, axis=0, keepdims=True)
            m_row[...] = m_new

    tile_step(pl.multiple_of(i * tq, tq), True)

    def body(j, carry):
        tile_step(pl.multiple_of(j * tq, tq), False)
        return carry

    lax.fori_loop(0, i, body, 0)
    for h in range(heads):
        o_lat = (acc_s[h] / ml_s[2 * h + 1:2 * h + 2, :]).astype(BF16)
        ot_s[h * V_DIM:(h + 1) * V_DIM, :] = _dot(wvbt_ref[h], o_lat)
    o_ref[...] = ot_s[...].T


def mla_flash_prompt(qt, kcat, kvt, wvbt, batch, tq=256):
    heads, dcat, m = qt.shape
    l = m // batch
    nq = l // tq
    return pl.pallas_call(
        functools.partial(_mla_flash_body, heads=heads, tq=tq),
        grid=(batch, nq),
        in_specs=[pl.BlockSpec((heads, dcat, tq), lambda b, i: (0, 0, b * nq + i)),
                  pl.BlockSpec((l, dcat), lambda b, i: (b, 0)),
                  pl.BlockSpec((KV_LORA, l), lambda b, i: (0, b)),
                  pl.BlockSpec(wvbt.shape, lambda b, i: (0, 0, 0))],
        out_specs=pl.BlockSpec((tq, heads * V_DIM), lambda b, i: (b * nq + i, 0)),
        out_shape=jax.ShapeDtypeStruct((m, heads * V_DIM), F32),
        scratch_shapes=[pltpu.VMEM((2 * heads, tq), F32), pltpu.VMEM((heads, KV_LORA, tq), F32),
                        pltpu.VMEM((heads * V_DIM, tq), F32)],
        compiler_params=_params("parallel", "arbitrary"),
        name="mla_flash_prompt",
    )(qt, kcat, kvt, wvbt)


def _mla_sample_body(pt_ref, q_ref, knew_ref, *rest, n_u, lq):
    kv_refs, pe_refs = rest[:n_u], rest[n_u:2 * n_u]
    o_ref, m_s, l_s, acc_s, kbuf = rest[2 * n_u:]
    c = pl.program_id(1)
    q = q_ref[0]

    @pl.when(c == 0)
    def _():
        kn = knew_ref[0]
        s = _nt(q, kn)
        t = lax.broadcasted_iota(jnp.int32, s.shape, 1)
        qpos = lax.broadcasted_iota(jnp.int32, s.shape, 0) % lq
        s = jnp.where(t <= qpos, s, NEG)
        m0 = jnp.max(s, axis=1, keepdims=True)
        p = jnp.exp(s - m0)
        m_s[...] = m0
        l_s[...] = jnp.sum(p, axis=1, keepdims=True)
        acc_s[...] = _dot(p.astype(BF16), kn[:, :KV_LORA])

    for u in range(n_u):
        rows = slice(u * PAGE_SIZE, (u + 1) * PAGE_SIZE)
        kbuf[rows, 0:KV_LORA] = kv_refs[u][0].astype(BF16)
        kbuf[rows, KV_LORA:KV_LORA + ROPE_DIM] = pe_refs[u][0].astype(BF16)
    s = _nt(q, kbuf[...])
    m_prev = m_s[...]
    m_new = jnp.maximum(m_prev, jnp.max(s, axis=1, keepdims=True))
    alpha = jnp.exp(m_prev - m_new)
    p = jnp.exp(s - m_new)
    l_s[...] = alpha * l_s[...] + jnp.sum(p, axis=1, keepdims=True)
    acc_s[...] = alpha * acc_s[...] + _dot(p.astype(BF16), kbuf[:, 0:KV_LORA])
    m_s[...] = m_new

    @pl.when(c == pl.num_programs(1) - 1)
    def _():
        o_ref[0] = acc_s[...] / l_s[...]


def mla_sample(qt, kcat_new, cache_kv, cache_pe, layer, page_table, lq):
    heads, dcat, m = qt.shape
    b = m // lq
    n_pages = page_table.shape[1]
    n_u = PAGES_PER_STEP
    r_n = heads * lq
    q = qt.reshape(heads, dcat, b, lq).transpose(2, 0, 3, 1).reshape(b, r_n, dcat)
    knew = jnp.pad(kcat_new.reshape(b, lq, dcat), ((0, 0), (0, PAGE_SIZE - lq), (0, 0)))
    kv_specs = [pl.BlockSpec((None, 1, PAGE_SIZE, KV_LORA), lambda bi, c, pt, u=u: (layer, pt[bi, c * n_u + u], 0, 0))
                for u in range(n_u)]
    pe_specs = [pl.BlockSpec((None, 1, PAGE_SIZE, ROPE_DIM), lambda bi, c, pt, u=u: (layer, pt[bi, c * n_u + u], 0, 0))
                for u in range(n_u)]
    o = pl.pallas_call(
        functools.partial(_mla_sample_body, n_u=n_u, lq=lq),
        grid_spec=pltpu.PrefetchScalarGridSpec(
            num_scalar_prefetch=1, grid=(b, n_pages // n_u),
            in_specs=[pl.BlockSpec((1, r_n, dcat), lambda bi, c, pt: (bi, 0, 0)),
                      pl.BlockSpec((1, PAGE_SIZE, dcat), lambda bi, c, pt: (bi, 0, 0))] + kv_specs + pe_specs,
            out_specs=pl.BlockSpec((1, r_n, KV_LORA), lambda bi, c, pt: (bi, 0, 0)),
            scratch_shapes=[pltpu.VMEM((r_n, 1), F32), pltpu.VMEM((r_n, 1), F32), pltpu.VMEM((r_n, KV_LORA), F32),
                            pltpu.VMEM((n_u * PAGE_SIZE, dcat), BF16)]),
        out_shape=jax.ShapeDtypeStruct((b, r_n, KV_LORA), F32),
        compiler_params=_params("parallel", "arbitrary"),
        name="mla_sample",
    )(page_table, q, knew, *([cache_kv] * n_u), *([cache_pe] * n_u))
    return o.reshape(b, heads, lq, KV_LORA).transpose(1, 0, 2, 3).reshape(heads, m, KV_LORA)


def _headproj_body(x_ref, w_ref, o_ref, *, heads):
    for h in range(heads):
        o_ref[:, h * V_DIM:(h + 1) * V_DIM] = _dot(x_ref[h].astype(BF16), w_ref[h])


def head_value_proj(o_lat, wvb):
    heads, m, _ = o_lat.shape
    return pl.pallas_call(
        functools.partial(_headproj_body, heads=heads),
        out_shape=jax.ShapeDtypeStruct((m, heads * V_DIM), F32),
        compiler_params=pltpu.CompilerParams(vmem_limit_bytes=VMEM_LIMIT),
        name="mla_value_proj",
    )(o_lat, wvb)


def _cross_body(q_ref, k_ref, v_ref, o_ref, *, heads, dh):
    q = q_ref[0]
    scale = dh ** -0.5
    for h in range(heads):
        sl = slice(h * dh, (h + 1) * dh)
        s = _nt((q[:, sl] * scale).astype(BF16), k_ref[0, :, sl].astype(BF16))
        m = jnp.max(s, axis=1, keepdims=True)
        p = jnp.exp(s - m)
        l = jnp.sum(p, axis=1, keepdims=True)
        o_ref[0, :, sl] = _dot(p.astype(BF16), v_ref[0, :, sl].astype(BF16)) / l


def cross_core(q, mk, mv, tq=512):
    b, lq, w = q.shape
    n_mem = mk.shape[1]
    tq = min(tq, lq)
    return pl.pallas_call(
        functools.partial(_cross_body, heads=X_HEADS, dh=X_HEAD_DIM),
        grid=(b, lq // tq),
        in_specs=[pl.BlockSpec((1, tq, w), lambda bi, i: (bi, i, 0)),
                  pl.BlockSpec((1, n_mem, w), lambda bi, i: (bi, 0, 0)),
                  pl.BlockSpec((1, n_mem, w), lambda bi, i: (bi, 0, 0))],
        out_specs=pl.BlockSpec((1, tq, w), lambda bi, i: (bi, i, 0)),
        out_shape=jax.ShapeDtypeStruct((b, lq, w), F32),
        compiler_params=_params("parallel", "parallel"),
        name="cross_core",
    )(q, mk, mv)


def _moe_gates(logits):
    lane = lax.broadcasted_iota(jnp.int32, logits.shape, 1).astype(F32)
    big = 1e9
    is_g = lane < N_GROUPS
    gl = jnp.where(is_g, logits, -jnp.inf)
    gmax = jnp.max(gl, axis=1, keepdims=True)
    grp = jnp.min(jnp.where(is_g & (gl == gmax), lane, big), axis=1, keepdims=True)
    p_grp = 1.0 / jnp.sum(jnp.where(is_g, jnp.exp(gl - gmax), 0.0), axis=1, keepdims=True)
    e_idx = lane - N_GROUPS
    in_grp = (e_idx >= grp * EXPERTS_PER_GROUP) & (e_idx < (grp + 1) * EXPERTS_PER_GROUP)
    el = jnp.where(in_grp, logits, -jnp.inf)
    t1 = jnp.max(el, axis=1, keepdims=True)
    i1 = jnp.min(jnp.where(in_grp & (el == t1), lane, big), axis=1, keepdims=True)
    el2 = jnp.where(lane == i1, -jnp.inf, el)
    t2 = jnp.max(el2, axis=1, keepdims=True)
    i2 = jnp.min(jnp.where(in_grp & (lane != i1) & (el2 == t2), lane, big), axis=1, keepdims=True)
    e2 = jnp.exp(t2 - t1)
    w1 = 1.0 / (1.0 + e2)
    w2 = e2 / (1.0 + e2)
    return p_grp * (jnp.where(lane == i1, w1, 0.0) + jnp.where(lane == i2, w2, 0.0))


def _moe_body(x_ref, g_ref, wr_ref, br_ref, wg_ref, wu_ref, wd_ref, o_ref, xn_s, gate_s, acc_s):
    e = pl.program_id(1)

    @pl.when(e == 0)
    def _():
        x = x_ref[...]
        xn = _rms(x, g_ref[...])
        xn_s[...] = xn.astype(BF16)
        gate_s[...] = _moe_gates(_dot(xn, wr_ref[...], HI) + br_ref[...])
        acc_s[...] = x

    xn = xn_s[...]
    lane = lax.broadcasted_iota(jnp.int32, gate_s.shape, 1)
    ge = jnp.sum(jnp.where(lane == e + N_GROUPS, gate_s[...], 0.0), axis=1, keepdims=True)
    a = _dot(xn, wg_ref[0])
    u = _dot(xn, wu_ref[0])
    hid = (a * _sigmoid(a)) * u * ge
    acc_s[...] += _dot(hid.astype(BF16), wd_ref[0])

    @pl.when(e == pl.num_programs(1) - 1)
    def _():
        o_ref[...] = acc_s[...]


def moe_layer(x, gain, w_group, b_group, w_router, b_router, w_up, w_gate, w_down, tm=1024):
    m, d = x.shape
    tm = min(tm, m)
    n_e, _, f = w_up.shape
    wr = jnp.concatenate([w_group, w_router, jnp.zeros((d, LANES - N_GROUPS - N_EXPERTS), F32)], axis=1)
    br = jnp.concatenate([b_group, b_router, jnp.zeros((LANES - N_GROUPS - N_EXPERTS,), F32)]).reshape(1, LANES)
    return pl.pallas_call(
        _moe_body,
        grid=(m // tm, n_e),
        in_specs=[pl.BlockSpec((tm, d), lambda i, e: (i, 0)),
                  pl.BlockSpec((1, d), lambda i, e: (0, 0)),
                  pl.BlockSpec((d, LANES), lambda i, e: (0, 0)),
                  pl.BlockSpec((1, LANES), lambda i, e: (0, 0)),
                  pl.BlockSpec((1, d, f), lambda i, e: (e, 0, 0)),
                  pl.BlockSpec((1, d, f), lambda i, e: (e, 0, 0)),
                  pl.BlockSpec((1, f, d), lambda i, e: (e, 0, 0))],
        out_specs=pl.BlockSpec((tm, d), lambda i, e: (i, 0)),
        out_shape=jax.ShapeDtypeStruct((m, d), F32),
        scratch_shapes=[pltpu.VMEM((tm, d), BF16), pltpu.VMEM((tm, LANES), F32), pltpu.VMEM((tm, d), F32)],
        compiler_params=_params("parallel", "arbitrary"),
        name="moe",
    )(x, gain.reshape(1, d), wr, br, w_gate, w_up, w_down)


AB_SPLITS = [(0, A_WIDTH), (A_WIDTH, A_WIDTH), (2 * A_WIDTH, A_WIDTH), (3 * A_WIDTH, 2 * B_WIDTH),
             (3 * A_WIDTH + 2 * B_WIDTH, B_WIDTH), (3 * A_WIDTH + 3 * B_WIDTH, B_WIDTH),
             (3 * A_WIDTH + 4 * B_WIDTH, LANES)]


def _even_projection(x, gain, w_in_bf):
    return fused_linear([x], [w_in_bf], splits=AB_SPLITS, gain=gain, name="in_proj_ab")


def _pad_seq(a, lp):
    return jnp.pad(a, ((0, 0), (0, lp - a.shape[1]), (0, 0)))


def kernel(x_prompt, x_sample, mem_prompt, cache_moba_k, cache_moba_v, cache_mla_kv, cache_mla_pe, state_mlstm_c, state_mlstm_n, state_mlstm_m, state_conv, cache_mem_k, cache_mem_v, page_table, norm_mix, norm_cross, norm_mem, norm_ffn, norm_final, w_in_ab, b_gates, conv_w, norm_mlstm, w_out_ab, w_in_c, norm_q_lat, norm_kv_lat, w_qb, w_kb, w_vb, w_out_c, w_cq, w_ck, w_cv, w_co, w_group, b_group, w_router, b_router, w_up, w_gate, w_down):
    bp, lp, d = x_prompt.shape
    bs, ls, _ = x_sample.shape
    depth = norm_mix.shape[0]
    mp, ms = bp * lp, bs * ls
    n_mem = mem_prompt.shape[1]
    past = page_table.shape[1] * PAGE_SIZE
    pos_p = jnp.tile(jnp.arange(lp, dtype=jnp.int32), bp)
    pos_s = jnp.tile(past + jnp.arange(ls, dtype=jnp.int32), bs)

    xp = x_prompt.reshape(mp, d)
    xs = x_sample.reshape(ms, d)
    mem = mem_prompt.reshape(bp * n_mem, d)
    ak_p, av_p, kv_p, pe_p, mc_p, mn_p, mm_p, cv_p, mk_p, mv_p = [], [], [], [], [], [], [], [], [], []
    ak_s, av_s, kv_s, pe_s, mc_s, mn_s, mm_s, cv_s = [], [], [], [], [], [], [], []
    back = CONV_W - 1
    ch = MLSTM_CHUNK
    ls_pad = -(-ls // ch) * ch
    for l in range(depth):
        j = l // 2
        if l % 2 == 0:
            n_ab = w_in_ab.shape[2]
            n_pad = AB_SPLITS[-1][0] + LANES
            w_in_bf = jnp.pad(w_in_ab[j], ((0, 0), (0, n_pad - n_ab))).astype(BF16)
            w_out_bf = w_out_ab[j].astype(BF16)
            qa, ka, va, qkb, vb, ob, gt = _even_projection(xp, norm_mix[l], w_in_bf)
            o_a = moba_prompt(qa.reshape(bp, lp, A_WIDTH), ka.reshape(bp, lp, A_WIDTH), va.reshape(bp, lp, A_WIDTH))
            qkb3 = qkb.reshape(bp, lp, 2 * B_WIDTH)
            hb, mc, mn, mm = mlstm_layer(
                qkb3, vb.reshape(bp, lp, B_WIDTH), ob.reshape(bp, lp, B_WIDTH), gt.reshape(bp, lp, LANES),
                b_gates[j], conv_w[j], norm_mlstm[j], jnp.zeros((bp, back, 2 * B_WIDTH), F32),
                jnp.zeros((bp, B_HEADS, B_HEAD_DIM, B_HEAD_DIM), F32), jnp.zeros((bp, B_HEADS, B_HEAD_DIM), F32),
                jnp.zeros((bp, B_HEADS), F32), valid=lp)
            (xp,) = fused_linear([o_a.reshape(mp, A_WIDTH), hb.reshape(mp, B_WIDTH)],
                                 [w_out_bf[:A_WIDTH], w_out_bf[A_WIDTH:]], residual=xp, name="out_proj_ab")
            ak_p.append(ka.reshape(bp, lp, A_HEADS, A_HEAD_DIM)); av_p.append(va.reshape(bp, lp, A_HEADS, A_HEAD_DIM))
            cv_p.append(qkb3[:, lp - back:, :]); mc_p.append(mc); mn_p.append(mn); mm_p.append(mm)
            qa, ka, va, qkb, vb, ob, gt = _even_projection(xs, norm_mix[l], w_in_bf)
            o_a = moba_sample(qa.reshape(bs, ls, A_WIDTH), ka.reshape(bs, ls, A_WIDTH), va.reshape(bs, ls, A_WIDTH),
                              cache_moba_k, cache_moba_v, j, page_table)
            qkb3 = qkb.reshape(bs, ls, 2 * B_WIDTH)
            hb, mc, mn, mm = mlstm_layer(
                _pad_seq(qkb3, ls_pad), _pad_seq(vb.reshape(bs, ls, B_WIDTH), ls_pad),
                _pad_seq(ob.reshape(bs, ls, B_WIDTH), ls_pad), _pad_seq(gt.reshape(bs, ls, LANES), ls_pad),
                b_gates[j], conv_w[j], norm_mlstm[j], state_conv[j],
                state_mlstm_c[j], state_mlstm_n[j], state_mlstm_m[j], valid=ls)
            (xs,) = fused_linear([o_a.reshape(ms, A_WIDTH), hb[:, :ls].reshape(ms, B_WIDTH)],
                                 [w_out_bf[:A_WIDTH], w_out_bf[A_WIDTH:]], residual=xs, name="out_proj_ab")
            ak_s.append(ka.reshape(bs, ls, A_HEADS, A_HEAD_DIM)); av_s.append(va.reshape(bs, ls, A_HEADS, A_HEAD_DIM))
            conv_all = jnp.concatenate([state_conv[j], qkb3], axis=1)
            cv_s.append(conv_all[:, ls:, :]); mc_s.append(mc); mn_s.append(mn); mm_s.append(mm)
        else:
            wvb = w_vb[j].transpose(1, 0, 2).astype(BF16)
            w_out_bf = w_out_c[j].astype(BF16)
            ckv, kpe, kcat, kvt, qt = mla_prep(xp, norm_mix[l], w_in_c[j], norm_q_lat[j], norm_kv_lat[j], w_qb[j], w_kb[j], pos_p)
            o = mla_flash_prompt(qt, kcat, kvt, w_vb[j].transpose(1, 2, 0).astype(BF16), bp)
            (xp,) = fused_linear([o], [w_out_bf], residual=xp, name="out_proj_c")
            kv_p.append(ckv.reshape(bp, lp, KV_LORA)); pe_p.append(kpe.reshape(bp, lp, ROPE_DIM))
            ckv, kpe, kcat, _, qt = mla_prep(xs, norm_mix[l], w_in_c[j], norm_q_lat[j], norm_kv_lat[j], w_qb[j], w_kb[j], pos_s)
            o_lat = mla_sample(qt, kcat, cache_mla_kv, cache_mla_pe, j, page_table, ls)
            o = head_value_proj(o_lat, wvb)
            (xs,) = fused_linear([o], [w_out_bf], residual=xs, name="out_proj_c")
            kv_s.append(ckv.reshape(bs, ls, KV_LORA)); pe_s.append(kpe.reshape(bs, ls, ROPE_DIM))
        w_ckv = jnp.concatenate([w_ck[l], w_cv[l]], axis=1).astype(BF16)
        mk, mv = fused_linear([mem], [w_ckv], splits=[(0, X_WIDTH), (X_WIDTH, X_WIDTH)], gain=norm_mem[l], name="mem_kv")
        mk_p.append(mk.reshape(bp, n_mem, X_HEADS, X_HEAD_DIM)); mv_p.append(mv.reshape(bp, n_mem, X_HEADS, X_HEAD_DIM))
        w_cq_bf, w_co_bf = w_cq[l].astype(BF16), w_co[l].astype(BF16)
        (q,) = fused_linear([xp], [w_cq_bf], gain=norm_cross[l], name="cross_q")
        o = cross_core(q.reshape(bp, lp, X_WIDTH), mk.reshape(bp, n_mem, X_WIDTH), mv.reshape(bp, n_mem, X_WIDTH))
        (xp,) = fused_linear([o.reshape(mp, X_WIDTH)], [w_co_bf], residual=xp, name="cross_out")
        (q,) = fused_linear([xs], [w_cq_bf], gain=norm_cross[l], name="cross_q")
        q8 = _pad_seq(q.reshape(bs, ls, X_WIDTH), 8)
        o = cross_core(q8, cache_mem_k[l].reshape(bs, n_mem, X_WIDTH), cache_mem_v[l].reshape(bs, n_mem, X_WIDTH))
        (xs,) = fused_linear([o[:, :ls].reshape(ms, X_WIDTH)], [w_co_bf], residual=xs, name="cross_out")
        wu, wg, wd = w_up[l].astype(BF16), w_gate[l].astype(BF16), w_down[l].astype(BF16)
        xp = moe_layer(xp, norm_ffn[l], w_group[l], b_group[l], w_router[l], b_router[l], wu, wg, wd)
        xs = moe_layer(xs, norm_ffn[l], w_group[l], b_group[l], w_router[l], b_router[l], wu, wg, wd)
    y_prompt = rmsnorm_rows(xp, norm_final).reshape(bp, lp, d)
    y_sample = rmsnorm_rows(xs, norm_final).reshape(bs, ls, d)
    return (y_prompt, y_sample,
            jnp.stack(ak_p), jnp.stack(av_p), jnp.stack(kv_p), jnp.stack(pe_p),
            jnp.stack(mc_p), jnp.stack(mn_p), jnp.stack(mm_p), jnp.stack(cv_p), jnp.stack(mk_p), jnp.stack(mv_p),
            jnp.stack(ak_s), jnp.stack(av_s), jnp.stack(kv_s), jnp.stack(pe_s),
            jnp.stack(mc_s), jnp.stack(mn_s), jnp.stack(mm_s), jnp.stack(cv_s))
```

```python
import functools
import math

import jax
import jax.numpy as jnp
from jax import lax
from jax.experimental import pallas as pl
from jax.experimental.pallas import tpu as pltpu

F32 = jnp.float32
BF16 = jnp.bfloat16
HI = lax.Precision.HIGHEST
EPS = 1e-6
NEG = -1e30
VMEM_LIMIT = 56 * 1024 * 1024
LANES = 128

PAGE_SIZE = 128
A_HEADS, A_HEAD_DIM = 8, 64
A_WIDTH = A_HEADS * A_HEAD_DIM
MOBA_BLOCK, MOBA_TOPK = 256, 3
B_HEADS, B_HEAD_DIM = 4, 128
B_WIDTH = B_HEADS * B_HEAD_DIM
CONV_W = 4
MLSTM_CHUNK = 64
C_HEADS, Q_LORA, KV_LORA, NOPE_DIM, ROPE_DIM, V_DIM = 16, 256, 128, 64, 32, 64
ROPE_THETA = 10000.0
X_HEADS, X_HEAD_DIM = 4, 128
X_WIDTH = X_HEADS * X_HEAD_DIM
N_GROUPS, EXPERTS_PER_GROUP = 4, 4
N_EXPERTS = N_GROUPS * EXPERTS_PER_GROUP
PAGES_PER_STEP = 16
MOBA_PAGES_PER_STEP = 16

NT_DIMS = (((1,), (1,)), ((), ()))
TN_DIMS = (((0,), (0,)), ((), ()))


def _params(*sem):
    return pltpu.CompilerParams(dimension_semantics=sem, vmem_limit_bytes=VMEM_LIMIT)


def _nt(a, b, precision=None):
    return lax.dot_general(a, b, NT_DIMS, precision=precision, preferred_element_type=F32)


def _dot(a, b, precision=None):
    return jnp.dot(a, b, precision=precision, preferred_element_type=F32)


def _rms(x, g):
    return x * lax.rsqrt(jnp.mean(x * x, axis=-1, keepdims=True) + EPS) * g


def _sigmoid(x):
    return 1.0 / (1.0 + jnp.exp(-x))


def _linear_body(*refs, n_in, has_gain, has_res, splits):
    x_refs, w_refs = refs[:n_in], refs[n_in:2 * n_in]
    p = 2 * n_in
    g_ref = refs[p] if has_gain else None
    p += int(has_gain)
    r_ref = refs[p] if has_res else None
    p += int(has_res)
    o_refs = refs[p:]
    xs = []
    for xr in x_refs:
        x = xr[...]
        if has_gain:
            x = _rms(x, g_ref[...])
        xs.append(x.astype(BF16))
    for (off, width), o_ref in zip(splits, o_refs):
        acc = None
        for x, wr in zip(xs, w_refs):
            y = _dot(x, wr[:, off:off + width])
            acc = y if acc is None else acc + y
        if has_res:
            acc = acc + r_ref[...]
        o_ref[...] = acc.astype(o_ref.dtype)


def fused_linear(xs, ws, splits=None, gain=None, residual=None, tm=512, name="linear"):
    m, n = xs[0].shape[0], ws[0].shape[1]
    splits = splits or [(0, n)]
    tm = min(tm, m)
    assert m % tm == 0
    in_specs = [pl.BlockSpec((tm, x.shape[1]), lambda i: (i, 0)) for x in xs]
    in_specs += [pl.BlockSpec(w.shape, lambda i: (0, 0)) for w in ws]
    args = list(xs) + list(ws)
    if gain is not None:
        in_specs.append(pl.BlockSpec((1, gain.shape[-1]), lambda i: (0, 0)))
        args.append(gain.reshape(1, -1))
    if residual is not None:
        assert len(splits) == 1
        in_specs.append(pl.BlockSpec((tm, n), lambda i: (i, 0)))
        args.append(residual)
    outs = pl.pallas_call(
        functools.partial(_linear_body, n_in=len(xs), has_gain=gain is not None, has_res=residual is not None,
                          splits=tuple(splits)),
        grid=(m // tm,),
        in_specs=in_specs,
        out_specs=[pl.BlockSpec((tm, w), lambda i: (i, 0)) for _, w in splits],
        out_shape=[jax.ShapeDtypeStruct((m, w), F32) for _, w in splits],
        compiler_params=_params("parallel"),
        name=name,
    )(*args)
    return outs


def _rmsnorm_body(x_ref, g_ref, o_ref):
    o_ref[...] = _rms(x_ref[...], g_ref[...])


def rmsnorm_rows(x, g, tm=1024):
    m, d = x.shape
    tm = min(tm, m)
    return pl.pallas_call(
        _rmsnorm_body,
        grid=(m // tm,),
        in_specs=[pl.BlockSpec((tm, d), lambda i: (i, 0)), pl.BlockSpec((1, d), lambda i: (0, 0))],
        out_specs=pl.BlockSpec((tm, d), lambda i: (i, 0)),
        out_shape=jax.ShapeDtypeStruct((m, d), F32),
        compiler_params=_params("parallel"),
        name="final_norm",
    )(x, g.reshape(1, d))


def _topk_mask(g, valid, n_iota, nb, topk):
    gm = jnp.where(valid, g, -jnp.inf)
    rank = jnp.zeros(g.shape, jnp.int32)
    for m in range(nb):
        gc = gm[:, m:m + 1]
        beats = (gc > gm) | ((gc == gm) & (m < n_iota))
        rank = rank + beats.astype(jnp.int32)
    return ((rank < topk) & valid).astype(F32)


def _topk_rows(g, valid, n_iota, nb, topk):
    gm = jnp.where(valid, g, -jnp.inf)
    rank = jnp.zeros(g.shape, jnp.int32)
    for m in range(nb):
        gr = gm[m:m + 1, :]
        beats = (gr > gm) | ((gr == gm) & (m < n_iota))
        rank = rank + beats.astype(jnp.int32)
    return ((rank < topk) & valid).astype(F32)


def _moba_prompt_body(q_ref, k_ref, v_ref, o_ref, kmean_s, kh_s, vt_s, qt_s, sel_s, ml_s, acc_s, ot_s, *,
                      nb, blk, heads, dh, topk):
    i = pl.program_id(1)

    @pl.when(i == 0)
    def _():
        for n in range(nb):
            rows = slice(n * blk, (n + 1) * blk)
            kmean_s[n:n + 1, :] = jnp.mean(k_ref[0, rows, :], axis=0, keepdims=True)
            vt_s[:, rows] = v_ref[0, rows, :].T.astype(BF16)
        for h in range(heads):
            kh_s[h] = k_ref[0, :, h * dh:(h + 1) * dh].astype(BF16)

    qt = q_ref[0].T
    krow = lax.broadcasted_iota(jnp.int32, (blk, blk), 0)
    qcol = lax.broadcasted_iota(jnp.int32, (blk, blk), 1)
    dmat = (qcol - krow).astype(F32)
    n_iota = lax.broadcasted_iota(jnp.int32, (nb, blk), 0)
    slopes = [2.0 ** (-8.0 * (h + 1) / heads) for h in range(heads)]
    head_rows = [slice(h * dh, (h + 1) * dh) for h in range(heads)]
    for h in range(heads):
        gate_t = _dot(kmean_s[:, head_rows[h]], qt[head_rows[h]], HI)
        sel_s[h] = _topk_rows(gate_t, n_iota < i, n_iota, nb, topk)
    qt_s[...] = (qt * dh ** -0.5).astype(BF16)

    def block_step(kstart, logits_fn, first):
        s_all = [_dot(kh_s[h, pl.ds(kstart, blk), :], qt_s[head_rows[h], :]) for h in range(heads)]
        for h in range(heads):
            logits = logits_fn(h, s_all[h])
            vt = vt_s[head_rows[h], pl.ds(kstart, blk)]
            m_row, l_row = ml_s.at[2 * h:2 * h + 1, :], ml_s.at[2 * h + 1:2 * h + 2, :]
            if first:
                m_new = jnp.max(logits, axis=0, keepdims=True)
                p = jnp.exp(logits - m_new)
                acc_s[h] = _dot(vt, p.astype(BF16))
                l_row[...] = jnp.sum(p, axis=0, keepdims=True)
            else:
                m = m_row[...]
                m_new = jnp.maximum(m, jnp.max(logits, axis=0, keepdims=True))
                alpha = jnp.exp(m - m_new)
                p = jnp.exp(logits - m_new)
                acc_s[h] = alpha * acc_s[h] + _dot(vt, p.astype(BF16))
                l_row[...] = alpha * l_row[...] + jnp.sum(p, axis=0, keepdims=True)
            m_row[...] = m_new

    block_step(pl.multiple_of(i * blk, blk), lambda h, s: jnp.where(dmat >= 0, s - slopes[h] * dmat, NEG), True)

    def body(j, carry):
        dist = dmat + ((i - j) * blk).astype(F32)
        pick = n_iota == j

        def logits_fn(h, s):
            selrow = jnp.sum(jnp.where(pick, sel_s[h], 0.0), axis=0, keepdims=True)
            return jnp.where(selrow > 0.5, s - slopes[h] * dist, NEG)

        block_step(pl.multiple_of(j * blk, blk), logits_fn, False)
        return carry

    lax.fori_loop(0, i, body, 0)
    for h in range(heads):
        ot_s[head_rows[h], :] = acc_s[h] / ml_s[2 * h + 1:2 * h + 2, :]
    o_ref[0] = ot_s[...].T


def moba_prompt(q, k, v):
    b, l, w = q.shape
    blk = MOBA_BLOCK
    nb = l // blk
    heads, dh = A_HEADS, A_HEAD_DIM
    return pl.pallas_call(
        functools.partial(_moba_prompt_body, nb=nb, blk=blk, heads=heads, dh=dh, topk=MOBA_TOPK),
        grid=(b, nb),
        in_specs=[pl.BlockSpec((1, blk, w), lambda bi, i: (bi, i, 0)),
                  pl.BlockSpec((1, l, w), lambda bi, i: (bi, 0, 0)),
                  pl.BlockSpec((1, l, w), lambda bi, i: (bi, 0, 0))],
        out_specs=pl.BlockSpec((1, blk, w), lambda bi, i: (bi, i, 0)),
        out_shape=jax.ShapeDtypeStruct((b, l, w), F32),
        scratch_shapes=[pltpu.VMEM((nb, w), F32), pltpu.VMEM((heads, l, dh), BF16), pltpu.VMEM((w, l), BF16),
                        pltpu.VMEM((w, blk), BF16), pltpu.VMEM((heads, nb, blk), F32),
                        pltpu.VMEM((2 * heads, blk), F32), pltpu.VMEM((heads, dh, blk), F32),
                        pltpu.VMEM((w, blk), F32)],
        compiler_params=_params("parallel", "arbitrary"),
        name="moba_prompt",
    )(q, k, v)


def _moba_scores_body(pt_ref, q_ref, *rest, n_u, heads):
    k_refs, (s_ref, gate_ref) = rest[:n_u], rest[n_u:]
    c = pl.program_id(1)

    @pl.when(c == 0)
    def _():
        gate_ref[...] = jnp.zeros(gate_ref.shape, F32)

    ppb = MOBA_BLOCK // PAGE_SIZE
    lane = lax.broadcasted_iota(jnp.int32, (8, LANES), 1)
    for h in range(heads):
        qh = q_ref[0, h].astype(BF16)
        g = gate_ref[0, h]
        for n in range(n_u // ppb):
            bsum = None
            for u in range(n * ppb, (n + 1) * ppb):
                su = _dot(qh, k_refs[u][0, h].astype(BF16))
                s_ref[0, h, :, u * PAGE_SIZE:(u + 1) * PAGE_SIZE] = su
                bsum = su if bsum is None else bsum + su
            g = g + jnp.where(lane == c * (n_u // ppb) + n, jnp.sum(bsum, axis=1, keepdims=True), 0.0)
        gate_ref[0, h] = g


def _moba_select_body(gate_ref, idx_ref, *, nb, topk):
    gate = gate_ref[0][:, :nb]
    r_n = gate.shape[0]
    n_iota = lax.broadcasted_iota(jnp.int32, (r_n, nb), 1)
    rank = jnp.zeros(gate.shape, jnp.int32)
    for m in range(nb):
        gc = gate[:, m:m + 1]
        rank = rank + ((gc > gate) | ((gc == gate) & (m < n_iota))).astype(jnp.int32)
    nf = n_iota.astype(F32)
    lane = lax.broadcasted_iota(jnp.int32, (r_n, LANES), 1)
    out = jnp.zeros((r_n, LANES), F32)
    for k in range(topk):
        idx_k = jnp.sum(jnp.where(rank == k, nf, 0.0), axis=1, keepdims=True)
        out = out + jnp.where(lane == k, idx_k, 0.0)
    idx_ref[0] = out.astype(jnp.int32)


def _moba_gather_body(pt_ref, sel_ref, s_ref, q_ref, knew_ref, vnew_ref, v_hbm, o_ref, vbuf, sem, *,
                      layer, lq, topk, heads, dh, past):
    b, h = pl.program_id(0), pl.program_id(1)
    n_h = pl.num_programs(1)
    step = b * n_h + h
    n_steps = pl.num_programs(0) * n_h
    slot = step % 2
    ppb = MOBA_BLOCK // PAGE_SIZE
    n_sel = lq * topk
    t_all = n_sel * MOBA_BLOCK

    def copies(bb, hh, sl):
        out = []
        for j in range(n_sel):
            blk = sel_ref[bb, hh * n_sel + j]
            for pg in range(ppb):
                page = pt_ref[bb, blk * ppb + pg]
                cols = pl.ds((j * ppb + pg) * PAGE_SIZE, PAGE_SIZE)
                out.append(pltpu.make_async_copy(v_hbm.at[layer, page, hh], vbuf.at[sl, :, cols], sem.at[sl]))
        return out

    @pl.when(step == 0)
    def _():
        for c in copies(b, h, slot):
            c.start()

    @pl.when(step + 1 < n_steps)
    def _():
        nxt = step + 1
        for c in copies(nxt // n_h, nxt % n_h, 1 - slot):
            c.start()

    scale = dh ** -0.5
    slope = jnp.exp2((-8.0 / heads) * (h + 1).astype(F32))
    row = lax.broadcasted_iota(jnp.int32, (8, t_all), 0)
    col = lax.broadcasted_iota(jnp.int32, (8, t_all), 1)
    choice = col // MOBA_BLOCK
    blocks = [sel_ref[b, h * n_sel + j] for j in range(n_sel)]
    s = jnp.concatenate([s_ref[0, 0, :, pl.ds(pl.multiple_of(blk * MOBA_BLOCK, MOBA_BLOCK), MOBA_BLOCK)]
                         for blk in blocks], axis=1)
    kpos = jnp.zeros((8, t_all), jnp.int32)
    for j in range(n_sel):
        kpos = jnp.where(choice == j, blocks[j] * MOBA_BLOCK, kpos)
    kpos = kpos + (col - choice * MOBA_BLOCK)
    dist = ((past + row) - kpos).astype(F32)
    logits = jnp.where((choice // topk) == row, s * scale - slope * dist, NEG)
    row1 = lax.broadcasted_iota(jnp.int32, (8, 1), 0)
    q8, knew, vnew = q_ref[0, 0], knew_ref[0, 0], vnew_ref[0, 0]
    own = []
    for t in range(lq):
        so = jnp.sum(q8 * knew[t:t + 1, :], axis=1, keepdims=True) * scale - slope * (row1 - t).astype(F32)
        own.append(jnp.where(row1 >= t, so, NEG))
    m = jnp.max(logits, axis=1, keepdims=True)
    for so in own:
        m = jnp.maximum(m, so)
    p = jnp.exp(logits - m)
    l = jnp.sum(p, axis=1, keepdims=True)

    for c in copies(b, h, slot):
        c.wait()
    acc = _nt(p.astype(BF16), vbuf[slot].astype(BF16))
    for t, so in enumerate(own):
        po = jnp.exp(so - m)
        l = l + po
        acc = acc + po * vnew[t:t + 1, :]
    o_ref[0, 0] = acc / l


def moba_sample(q, k_new, v_new, cache_k, cache_v, layer, page_table):
    b, lq, w = q.shape
    heads, dh = A_HEADS, A_HEAD_DIM
    n_pages = page_table.shape[1]
    past = n_pages * PAGE_SIZE
    nb = past // MOBA_BLOCK
    assert past % MOBA_BLOCK == 0 and MOBA_TOPK <= nb <= LANES and lq <= 8
    n_u = MOBA_PAGES_PER_STEP
    ppb = MOBA_BLOCK // PAGE_SIZE
    ck_t = cache_k.transpose(0, 1, 3, 4, 2)
    cv_t = cache_v.transpose(0, 1, 3, 4, 2)

    def per_head(a):
        a = a.reshape(b, lq, heads, dh).transpose(0, 2, 1, 3)
        return jnp.pad(a, ((0, 0), (0, 0), (0, 8 - lq), (0, 0)))

    qh = per_head(q)
    page_specs = [pl.BlockSpec((None, 1, heads, dh, PAGE_SIZE),
                               lambda bi, c, pt, u=u: (layer, pt[bi, c * n_u + u], 0, 0, 0)) for u in range(n_u)]
    scores, gate = pl.pallas_call(
        functools.partial(_moba_scores_body, n_u=n_u, heads=heads),
        grid_spec=pltpu.PrefetchScalarGridSpec(
            num_scalar_prefetch=1, grid=(b, n_pages // n_u),
            in_specs=[pl.BlockSpec((1, heads, 8, dh), lambda bi, c, pt: (bi, 0, 0, 0))] + page_specs,
            out_specs=[pl.BlockSpec((1, heads, 8, n_u * PAGE_SIZE), lambda bi, c, pt: (bi, 0, 0, c)),
                       pl.BlockSpec((1, heads, 8, LANES), lambda bi, c, pt: (bi, 0, 0, 0))]),
        out_shape=[jax.ShapeDtypeStruct((b, heads, 8, past), F32), jax.ShapeDtypeStruct((b, heads, 8, LANES), F32)],
        compiler_params=_params("parallel", "arbitrary"),
        name="moba_sample_scores",
    )(page_table, qh, *([ck_t] * n_u))

    r_n = heads * 8
    sel = pl.pallas_call(
        functools.partial(_moba_select_body, nb=nb, topk=MOBA_TOPK),
        grid=(b,),
        in_specs=[pl.BlockSpec((1, r_n, LANES), lambda bi: (bi, 0, 0))],
        out_specs=pl.BlockSpec((1, r_n, LANES), lambda bi: (bi, 0, 0)),
        out_shape=jax.ShapeDtypeStruct((b, r_n, LANES), jnp.int32),
        compiler_params=_params("parallel"),
        name="moba_sample_select",
    )(gate.reshape(b, r_n, LANES))
    n_sel = lq * MOBA_TOPK
    sel = sel.reshape(b, heads, 8, LANES)[:, :, :lq, :MOBA_TOPK].reshape(b, heads * n_sel)

    head_spec = pl.BlockSpec((1, 1, 8, dh), lambda bi, hi, pt, sl: (bi, hi, 0, 0))
    o = pl.pallas_call(
        functools.partial(_moba_gather_body, layer=layer, lq=lq, topk=MOBA_TOPK, heads=heads, dh=dh, past=past),
        grid_spec=pltpu.PrefetchScalarGridSpec(
            num_scalar_prefetch=2, grid=(b, heads),
            in_specs=[pl.BlockSpec((1, 1, 8, past), lambda bi, hi, pt, sl: (bi, hi, 0, 0)),
                      head_spec, head_spec, head_spec, pl.BlockSpec(memory_space=pl.ANY)],
            out_specs=head_spec,
            scratch_shapes=[pltpu.VMEM((2, dh, n_sel * MOBA_BLOCK), F32), pltpu.SemaphoreType.DMA((2,))]),
        out_shape=jax.ShapeDtypeStruct((b, heads, 8, dh), F32),
        compiler_params=_params("arbitrary", "arbitrary"),
        name="moba_sample_gather",
    )(page_table, sel, scores, qh, per_head(k_new), per_head(v_new), cv_t)
    return o[:, :, :lq, :].transpose(0, 2, 1, 3).reshape(b, lq, w)


def _log_sigmoid(x):
    return -(jnp.maximum(-x, 0.0) + jnp.log1p(jnp.exp(-jnp.abs(x))))


def _mlstm_body(qk_ref, v_ref, og_ref, gcol_ref, grow_ref, bcol_ref, brow_ref, cw_ref, gh_ref, cbuf_ref,
                c0_ref, n0_ref, m0_ref, h_ref, c_out, n_out, m_out, xbuf, c_s, n_s, m_s, *, ch, valid, heads, dh):
    c = pl.program_id(1)
    width = heads * dh

    @pl.when(c == 0)
    def _():
        xbuf[0:8, :] = cbuf_ref[0]
        c_s[...] = c0_ref[0]
        n_s[...] = n0_ref[0]
        m_s[...] = m0_ref[0]

    xbuf[8:8 + ch, :] = qk_ref[0]
    cw = cw_ref[...]
    back = CONV_W - 1
    y = xbuf[8 - back:8 - back + ch, :] * cw[0:1, :]
    for t in range(1, CONV_W):
        y = y + xbuf[8 - back + t:8 - back + t + ch, :] * cw[t:t + 1, :]
    xbuf[8 - back:8, :] = xbuf[8 + ch - back:8 + ch, :]
    y = y * _sigmoid(y)

    t_col = lax.broadcasted_iota(jnp.int32, (ch, LANES), 0)
    t_row = lax.broadcasted_iota(jnp.int32, (8, ch), 1)
    gcol = gcol_ref[0] + bcol_ref[...]
    grow = grow_ref[0, 0] + brow_ref[:, :ch]
    ig_col = jnp.where(t_col < valid, gcol, NEG)
    lf_col = jnp.where(t_col < valid, _log_sigmoid(gcol), 0.0)
    ig_row = jnp.where(t_row < valid, grow, NEG)
    lf_row = jnp.where(t_row < valid, _log_sigmoid(grow), 0.0)
    ti = lax.broadcasted_iota(jnp.int32, (ch, ch), 0)
    si = lax.broadcasted_iota(jnp.int32, (ch, ch), 1)
    causal = ti >= si
    b_col = _dot(causal.astype(F32), lf_col, HI)
    b_row = _dot(lf_row, (ti <= si).astype(F32), HI)

    v = v_ref[0]
    og = og_ref[0]
    gh = gh_ref[...]
    qhs = [y[:, h * dh:(h + 1) * dh] for h in range(heads)]
    khs = [y[:, width + h * dh:width + (h + 1) * dh] * (dh ** -0.5) for h in range(heads)]
    qk_all = [_nt(qhs[h].astype(BF16), khs[h].astype(BF16)) for h in range(heads)]
    qc_all = [_dot(qhs[h].astype(BF16), c_s[h].astype(BF16)) for h in range(heads)]
    for h in range(heads):
        sl = slice(h * dh, (h + 1) * dh)
        qh, kh = qhs[h], khs[h]
        vh = v[:, sl]
        igc, bc = ig_col[:, h:h + 1], b_col[:, heads + h:heads + h + 1]
        igr, br = ig_row[h:h + 1, :], b_row[heads + h:heads + h + 1, :]
        m_prev = m_s[h][:, 0:1]
        c_prev = c_s[h]
        n_prev = n_s[h]
        log_d = jnp.where(causal, bc - br + igr, NEG)
        inter = bc + m_prev
        mt = jnp.maximum(inter, jnp.max(log_d, axis=1, keepdims=True))
        d = jnp.exp(log_d - mt)
        w_inter = jnp.exp(inter - mt)
        vb = vh.astype(BF16)
        a = qk_all[h] * d
        num = _dot(a.astype(BF16), vb) + w_inter * qc_all[h]
        den = jnp.sum(a, axis=1, keepdims=True) + w_inter * jnp.sum(qh * n_prev, axis=1, keepdims=True)
        hh = num / jnp.maximum(jnp.abs(den), jnp.exp(-mt))
        b_last = bc[ch - 1:ch, :]
        logw = b_last - bc + igc
        m_new = jnp.maximum(b_last + m_prev, jnp.max(logw, axis=0, keepdims=True))
        ws = jnp.exp(logw - m_new)
        decay = jnp.exp(b_last + m_prev - m_new)
        kw = kh * ws
        c_s[h] = decay * c_prev + lax.dot_general(kw.astype(BF16), vb, TN_DIMS, preferred_element_type=F32)
        n_s[h] = decay * n_prev + jnp.sum(kw, axis=0, keepdims=True)
        m_s[h] = jnp.broadcast_to(m_new, (1, LANES))
        hn = _rms(hh, gh[:, sl])
        h_ref[0, :, sl] = hn * _sigmoid(og[:, sl])

    @pl.when(c == pl.num_programs(1) - 1)
    def _():
        c_out[0] = c_s[...]
        n_out[0] = n_s[...]
        m_out[0] = m_s[...]


def mlstm_layer(qk, v, og, gates, b_g, conv_w, g_h, conv_buf, c0, n0, m0, valid):
    b, lp, w2 = qk.shape
    w = w2 // 2
    heads, dh, ch = B_HEADS, B_HEAD_DIM, MLSTM_CHUNK
    nc = lp // ch
    assert valid == lp or nc == 1
    grow = gates[:, :, :8].reshape(b, nc, ch, 8).transpose(0, 1, 3, 2)
    bcol = jnp.pad(b_g, (0, LANES - 8)).reshape(1, LANES)
    brow = jnp.broadcast_to(b_g[:, None], (8, LANES))
    cw = jnp.pad(conv_w, ((0, 8 - CONV_W), (0, 0)))
    cbuf = jnp.pad(conv_buf, ((0, 0), (8 - (CONV_W - 1), 0), (0, 0)))
    n0 = n0.reshape(b, heads, 1, dh)
    m0 = jnp.broadcast_to(m0[:, :, None, None], (b, heads, 1, LANES))
    full = lambda shape: pl.BlockSpec(shape, lambda bi, c: (0,) * len(shape))
    per_b = lambda shape: pl.BlockSpec(shape, lambda bi, c: (bi,) + (0,) * (len(shape) - 1))
    seq = lambda width: pl.BlockSpec((1, ch, width), lambda bi, c: (bi, c, 0))
    h, c_f, n_f, m_f = pl.pallas_call(
        functools.partial(_mlstm_body, ch=ch, valid=valid if nc == 1 else ch, heads=heads, dh=dh),
        grid=(b, nc),
        in_specs=[seq(w2), seq(w), seq(w), seq(LANES),
                  pl.BlockSpec((1, 1, 8, ch), lambda bi, c: (bi, c, 0, 0)),
                  full((1, LANES)), full((8, LANES)), full((8, w2)), full((1, w)),
                  per_b((1, 8, w2)), per_b((1, heads, dh, dh)), per_b((1, heads, 1, dh)), per_b((1, heads, 1, LANES))],
        out_specs=[seq(w), per_b((1, heads, dh, dh)), per_b((1, heads, 1, dh)), per_b((1, heads, 1, LANES))],
        out_shape=[jax.ShapeDtypeStruct((b, lp, w), F32), jax.ShapeDtypeStruct((b, heads, dh, dh), F32),
                   jax.ShapeDtypeStruct((b, heads, 1, dh), F32), jax.ShapeDtypeStruct((b, heads, 1, LANES), F32)],
        scratch_shapes=[pltpu.VMEM((8 + ch, w2), F32), pltpu.VMEM((heads, dh, dh), F32),
                        pltpu.VMEM((heads, 1, dh), F32), pltpu.VMEM((heads, 1, LANES), F32)],
        compiler_params=_params("parallel", "arbitrary"),
        name="mlstm",
    )(qk, v, og, gates, grow, bcol, brow, cw, g_h.reshape(1, w), cbuf, c0, n0, m0)
    return h, c_f, n_f.reshape(b, heads, dh), m_f[:, :, 0, 0]


def _mla_prep_body(x_ref, g_ref, win_ref, gq_ref, gkv_ref, wnt_ref, wat_ref, wbt_ref, wkb_ref,
                   cosk_ref, sink_ref, cost_ref, sint_ref,
                   ckv_ref, kpe_ref, kcat_ref, kvt_ref, qt_ref, *, heads):
    xn = _rms(x_ref[...], g_ref[...]).astype(BF16)
    y = _dot(xn, win_ref[...])
    ckv = _rms(y[:, Q_LORA:Q_LORA + KV_LORA], gkv_ref[...])
    k0 = Q_LORA + KV_LORA
    kpe = y[:, k0:k0 + ROPE_DIM] * cosk_ref[...] + y[:, k0 + ROPE_DIM:k0 + 2 * ROPE_DIM] * sink_ref[...]
    ckv_ref[...] = ckv
    kpe_ref[...] = kpe
    kcat_ref[:, 0:KV_LORA] = ckv.astype(BF16)
    kvt_ref[...] = ckv.T.astype(BF16)
    kcat_ref[:, KV_LORA:KV_LORA + ROPE_DIM] = kpe.astype(BF16)
    cqt = _rms(y[:, :Q_LORA], gq_ref[...]).T.astype(BF16)
    nope_t = _dot(wnt_ref[...], cqt)
    qpe_t = _dot(wat_ref[...], cqt) * cost_ref[...] + _dot(wbt_ref[...], cqt) * sint_ref[...]
    scale = (NOPE_DIM + ROPE_DIM) ** -0.5
    for h in range(heads):
        lat_t = _dot(wkb_ref[h], nope_t[h * NOPE_DIM:(h + 1) * NOPE_DIM].astype(BF16))
        qt_ref[h, 0:KV_LORA, :] = (lat_t * scale).astype(BF16)
        qt_ref[h, KV_LORA:KV_LORA + ROPE_DIM, :] = (qpe_t[h * ROPE_DIM:(h + 1) * ROPE_DIM] * scale).astype(BF16)


def _rot_half(wpe):
    half = wpe.shape[-1] // 2
    return jnp.concatenate([-wpe[..., half:], wpe[..., :half]], axis=-1)


def mla_prep(x, gain, w_in, g_q, g_kv, w_qb, w_kb, pos, tm=512):
    m, d = x.shape
    heads = C_HEADS
    tm = min(tm, m)
    half = ROPE_DIM // 2
    freqs = ROPE_THETA ** (-jnp.arange(half, dtype=F32) / half)
    ang = pos.astype(F32)[:, None] * freqs
    cos_k = jnp.tile(jnp.cos(ang), (1, 2))
    sin_k = jnp.tile(jnp.sin(ang), (1, 2))
    cos_t = jnp.tile(cos_k, (1, heads)).T
    sin_t = jnp.tile(sin_k, (1, heads)).T
    k0 = Q_LORA + KV_LORA
    win = jnp.concatenate([w_in, _rot_half(w_in[:, k0:k0 + ROPE_DIM]),
                           jnp.zeros((d, 512 - k0 - 2 * ROPE_DIM), F32)], axis=1).astype(BF16)
    wq = w_qb.reshape(Q_LORA, heads, NOPE_DIM + ROPE_DIM)
    wnt = wq[:, :, :NOPE_DIM].reshape(Q_LORA, heads * NOPE_DIM).T.astype(BF16)
    wat = wq[:, :, NOPE_DIM:].reshape(Q_LORA, heads * ROPE_DIM).T.astype(BF16)
    wbt = _rot_half(wq[:, :, NOPE_DIM:]).reshape(Q_LORA, heads * ROPE_DIM).T.astype(BF16)
    wkb = w_kb.transpose(1, 0, 2).astype(BF16)
    full = lambda a: pl.BlockSpec(a.shape, lambda i: (0,) * a.ndim)
    rows = lambda width: pl.BlockSpec((tm, width), lambda i: (i, 0))
    cols = lambda height: pl.BlockSpec((height, tm), lambda i: (0, i))
    dcat = KV_LORA + ROPE_DIM
    gq, gkv, gm = g_q.reshape(1, -1), g_kv.reshape(1, -1), gain.reshape(1, -1)
    return pl.pallas_call(
        functools.partial(_mla_prep_body, heads=heads),
        grid=(m // tm,),
        in_specs=[rows(d), full(gm), full(win), full(gq), full(gkv), full(wnt), full(wat), full(wbt), full(wkb),
                  rows(ROPE_DIM), rows(ROPE_DIM), cols(heads * ROPE_DIM), cols(heads * ROPE_DIM)],
        out_specs=[rows(KV_LORA), rows(ROPE_DIM), rows(dcat), cols(KV_LORA),
                   pl.BlockSpec((heads, dcat, tm), lambda i: (0, 0, i))],
        out_shape=[jax.ShapeDtypeStruct((m, KV_LORA), F32), jax.ShapeDtypeStruct((m, ROPE_DIM), F32),
                   jax.ShapeDtypeStruct((m, dcat), BF16), jax.ShapeDtypeStruct((KV_LORA, m), BF16),
                   jax.ShapeDtypeStruct((heads, dcat, m), BF16)],
        compiler_params=_params("parallel"),
        name="mla_prep",
    )(x, gm, win, gq, gkv, wnt, wat, wbt, wkb, cos_k, sin_k, cos_t, sin_t)


def _mla_flash_body(qt_ref, k_ref, kvt_ref, wvbt_ref, o_ref, ml_s, acc_s, ot_s, *, heads, tq):
    i = pl.program_id(1)
    krow = lax.broadcasted_iota(jnp.int32, (tq, tq), 0)
    qcol = lax.broadcasted_iota(jnp.int32, (tq, tq), 1)
    causal = krow <= qcol

    def tile_step(kstart, first):
        k = k_ref[pl.ds(kstart, tq), :]
        vt = kvt_ref[:, pl.ds(kstart, tq)]
        s_all = [_dot(k, qt_ref[h]) for h in range(heads)]
        for h in range(heads):
            m_row, l_row = ml_s.at[2 * h:2 * h + 1, :], ml_s.at[2 * h + 1:2 * h + 2, :]
            if first:
                s = jnp.where(causal, s_all[h], NEG)
                m_new = jnp.max(s, axis=0, keepdims=True)
                p = jnp.exp(s - m_new)
                acc_s[h] = _dot(vt, p.astype(BF16))
                l_row[...] = jnp.sum(p, axis=0, keepdims=True)
            else:
                s = s_all[h]
                m = m_row[...]
                m_new = jnp.maximum(m, jnp.max(s, axis=0, keepdims=True))
                alpha = jnp.exp(m - m_new)
                p = jnp.exp(s - m_new)
                acc_s[h] = alpha * acc_s[h] + _dot(vt, p.astype(BF16))
                l_row[...] = alpha * l_row[...] + jnp.sum(p, axis=0, keepdims=True)
            m_row[...] = m_new

    tile_step(pl.multiple_of(i * tq, tq), True)

    def body(j, carry):
        tile_step(pl.multiple_of(j * tq, tq), False)
        return carry

    lax.fori_loop(0, i, body, 0)
    for h in range(heads):
        o_lat = (acc_s[h] / ml_s[2 * h + 1:2 * h + 2, :]).astype(BF16)
        ot_s[h * V_DIM:(h + 1) * V_DIM, :] = _dot(wvbt_ref[h], o_lat)
    o_ref[...] = ot_s[...].T


def mla_flash_prompt(qt, kcat, kvt, wvbt, batch, tq=256):
    heads, dcat, m = qt.shape
    l = m // batch
    nq = l // tq
    return pl.pallas_call(
        functools.partial(_mla_flash_body, heads=heads, tq=tq),
        grid=(batch, nq),
        in_specs=[pl.BlockSpec((heads, dcat, tq), lambda b, i: (0, 0, b * nq + i)),
                  pl.BlockSpec((l, dcat), lambda b, i: (b, 0)),
                  pl.BlockSpec((KV_LORA, l), lambda b, i: (0, b)),
                  pl.BlockSpec(wvbt.shape, lambda b, i: (0, 0, 0))],
        out_specs=pl.BlockSpec((tq, heads * V_DIM), lambda b, i: (b * nq + i, 0)),
        out_shape=jax.ShapeDtypeStruct((m, heads * V_DIM), F32),
        scratch_shapes=[pltpu.VMEM((2 * heads, tq), F32), pltpu.VMEM((heads, KV_LORA, tq), F32),
                        pltpu.VMEM((heads * V_DIM, tq), F32)],
        compiler_params=_params("parallel", "arbitrary"),
        name="mla_flash_prompt",
    )(qt, kcat, kvt, wvbt)


def _mla_sample_body(pt_ref, q_ref, knew_ref, *rest, n_u, lq):
    kv_refs, pe_refs = rest[:n_u], rest[n_u:2 * n_u]
    o_ref, m_s, l_s, acc_s, kbuf, pbuf = rest[2 * n_u:]
    c = pl.program_id(1)
    q = q_ref[0]

    @pl.when(c == 0)
    def _():
        kn = knew_ref[0]
        s = _nt(q, kn)
        t = lax.broadcasted_iota(jnp.int32, s.shape, 1)
        qpos = lax.broadcasted_iota(jnp.int32, s.shape, 0) % lq
        s = jnp.where(t <= qpos, s, NEG)
        m0 = jnp.max(s, axis=1, keepdims=True)
        p = jnp.exp(s - m0)
        m_s[...] = m0
        l_s[...] = jnp.sum(p, axis=1, keepdims=True)
        acc_s[...] = _dot(p.astype(BF16), kn[:, :KV_LORA])

    for u in range(n_u):
        rows = slice(u * PAGE_SIZE, (u + 1) * PAGE_SIZE)
        kbuf[rows, :] = kv_refs[u][0].astype(BF16)
        pbuf[:, rows] = pe_refs[u][0].astype(BF16)
    s = _nt(q[:, :KV_LORA], kbuf[...]) + _dot(q[:, KV_LORA:], pbuf[...])
    m_prev = m_s[...]
    m_new = jnp.maximum(m_prev, jnp.max(s, axis=1, keepdims=True))
    alpha = jnp.exp(m_prev - m_new)
    p = jnp.exp(s - m_new)
    l_s[...] = alpha * l_s[...] + jnp.sum(p, axis=1, keepdims=True)
    acc_s[...] = alpha * acc_s[...] + _dot(p.astype(BF16), kbuf[...])
    m_s[...] = m_new

    @pl.when(c == pl.num_programs(1) - 1)
    def _():
        o_ref[0] = acc_s[...] / l_s[...]


def mla_sample(qt, kcat_new, cache_kv, cache_pe, layer, page_table, lq):
    heads, dcat, m = qt.shape
    b = m // lq
    n_pages = page_table.shape[1]
    n_u = PAGES_PER_STEP
    r_n = heads * lq
    q = qt.reshape(heads, dcat, b, lq).transpose(2, 0, 3, 1).reshape(b, r_n, dcat)
    knew = jnp.pad(kcat_new.reshape(b, lq, dcat), ((0, 0), (0, PAGE_SIZE - lq), (0, 0)))
    kv_specs = [pl.BlockSpec((None, 1, PAGE_SIZE, KV_LORA), lambda bi, c, pt, u=u: (layer, pt[bi, c * n_u + u], 0, 0))
                for u in range(n_u)]
    pe_t = cache_pe.transpose(0, 1, 3, 2)
    pe_specs = [pl.BlockSpec((None, 1, ROPE_DIM, PAGE_SIZE), lambda bi, c, pt, u=u: (layer, pt[bi, c * n_u + u], 0, 0))
                for u in range(n_u)]
    o = pl.pallas_call(
        functools.partial(_mla_sample_body, n_u=n_u, lq=lq),
        grid_spec=pltpu.PrefetchScalarGridSpec(
            num_scalar_prefetch=1, grid=(b, n_pages // n_u),
            in_specs=[pl.BlockSpec((1, r_n, dcat), lambda bi, c, pt: (bi, 0, 0)),
                      pl.BlockSpec((1, PAGE_SIZE, dcat), lambda bi, c, pt: (bi, 0, 0))] + kv_specs + pe_specs,
            out_specs=pl.BlockSpec((1, r_n, KV_LORA), lambda bi, c, pt: (bi, 0, 0)),
            scratch_shapes=[pltpu.VMEM((r_n, 1), F32), pltpu.VMEM((r_n, 1), F32), pltpu.VMEM((r_n, KV_LORA), F32),
                            pltpu.VMEM((n_u * PAGE_SIZE, KV_LORA), BF16), pltpu.VMEM((ROPE_DIM, n_u * PAGE_SIZE), BF16)]),
        out_shape=jax.ShapeDtypeStruct((b, r_n, KV_LORA), F32),
        compiler_params=_params("parallel", "arbitrary"),
        name="mla_sample",
    )(page_table, q, knew, *([cache_kv] * n_u), *([pe_t] * n_u))
    return o.reshape(b, heads, lq, KV_LORA).transpose(1, 0, 2, 3).reshape(heads, m, KV_LORA)


def _headproj_body(x_ref, w_ref, o_ref, *, heads):
    for h in range(heads):
        o_ref[:, h * V_DIM:(h + 1) * V_DIM] = _dot(x_ref[h].astype(BF16), w_ref[h])


def head_value_proj(o_lat, wvb):
    heads, m, _ = o_lat.shape
    return pl.pallas_call(
        functools.partial(_headproj_body, heads=heads),
        out_shape=jax.ShapeDtypeStruct((m, heads * V_DIM), F32),
        compiler_params=pltpu.CompilerParams(vmem_limit_bytes=VMEM_LIMIT),
        name="mla_value_proj",
    )(o_lat, wvb)


def _cross_body(q_ref, k_ref, v_ref, o_ref, *, heads, dh):
    q = q_ref[0]
    scale = dh ** -0.5
    for h in range(heads):
        sl = slice(h * dh, (h + 1) * dh)
        s = _nt((q[:, sl] * scale).astype(BF16), k_ref[0, :, sl].astype(BF16))
        m = jnp.max(s, axis=1, keepdims=True)
        p = jnp.exp(s - m)
        l = jnp.sum(p, axis=1, keepdims=True)
        o_ref[0, :, sl] = _dot(p.astype(BF16), v_ref[0, :, sl].astype(BF16)) / l


def cross_core(q, mk, mv, tq=512):
    b, lq, w = q.shape
    n_mem = mk.shape[1]
    tq = min(tq, lq)
    return pl.pallas_call(
        functools.partial(_cross_body, heads=X_HEADS, dh=X_HEAD_DIM),
        grid=(b, lq // tq),
        in_specs=[pl.BlockSpec((1, tq, w), lambda bi, i: (bi, i, 0)),
                  pl.BlockSpec((1, n_mem, w), lambda bi, i: (bi, 0, 0)),
                  pl.BlockSpec((1, n_mem, w), lambda bi, i: (bi, 0, 0))],
        out_specs=pl.BlockSpec((1, tq, w), lambda bi, i: (bi, i, 0)),
        out_shape=jax.ShapeDtypeStruct((b, lq, w), F32),
        compiler_params=_params("parallel", "parallel"),
        name="cross_core",
    )(q, mk, mv)


def _moe_gates(logits):
    lane = lax.broadcasted_iota(jnp.int32, logits.shape, 1).astype(F32)
    big = 1e9
    is_g = lane < N_GROUPS
    gl = jnp.where(is_g, logits, -jnp.inf)
    gmax = jnp.max(gl, axis=1, keepdims=True)
    grp = jnp.min(jnp.where(is_g & (gl == gmax), lane, big), axis=1, keepdims=True)
    p_grp = 1.0 / jnp.sum(jnp.where(is_g, jnp.exp(gl - gmax), 0.0), axis=1, keepdims=True)
    e_idx = lane - N_GROUPS
    in_grp = (e_idx >= grp * EXPERTS_PER_GROUP) & (e_idx < (grp + 1) * EXPERTS_PER_GROUP)
    el = jnp.where(in_grp, logits, -jnp.inf)
    t1 = jnp.max(el, axis=1, keepdims=True)
    i1 = jnp.min(jnp.where(in_grp & (el == t1), lane, big), axis=1, keepdims=True)
    el2 = jnp.where(lane == i1, -jnp.inf, el)
    t2 = jnp.max(el2, axis=1, keepdims=True)
    i2 = jnp.min(jnp.where(in_grp & (lane != i1) & (el2 == t2), lane, big), axis=1, keepdims=True)
    e2 = jnp.exp(t2 - t1)
    w1 = 1.0 / (1.0 + e2)
    w2 = e2 / (1.0 + e2)
    return p_grp * (jnp.where(lane == i1, w1, 0.0) + jnp.where(lane == i2, w2, 0.0))


def _moe_body(x_ref, g_ref, wr_ref, br_ref, wg_ref, wu_ref, wd_ref, o_ref, xn_s, gate_s, acc_s):
    e = pl.program_id(1)

    @pl.when(e == 0)
    def _():
        x = x_ref[...]
        xn = _rms(x, g_ref[...])
        xn_s[...] = xn.astype(BF16)
        gate_s[...] = _moe_gates(_dot(xn, wr_ref[...], HI) + br_ref[...])
        acc_s[...] = x

    xn = xn_s[...]
    lane = lax.broadcasted_iota(jnp.int32, gate_s.shape, 1)
    ge = jnp.sum(jnp.where(lane == e + N_GROUPS, gate_s[...], 0.0), axis=1, keepdims=True)
    a = _dot(xn, wg_ref[0])
    u = _dot(xn, wu_ref[0])
    hid = (a * _sigmoid(a)) * u * ge
    acc_s[...] += _dot(hid.astype(BF16), wd_ref[0])

    @pl.when(e == pl.num_programs(1) - 1)
    def _():
        o_ref[...] = acc_s[...]


def moe_layer(x, gain, w_group, b_group, w_router, b_router, w_up, w_gate, w_down, tm=1024):
    m, d = x.shape
    tm = min(tm, m)
    n_e, _, f = w_up.shape
    wr = jnp.concatenate([w_group, w_router, jnp.zeros((d, LANES - N_GROUPS - N_EXPERTS), F32)], axis=1)
    br = jnp.concatenate([b_group, b_router, jnp.zeros((LANES - N_GROUPS - N_EXPERTS,), F32)]).reshape(1, LANES)
    return pl.pallas_call(
        _moe_body,
        grid=(m // tm, n_e),
        in_specs=[pl.BlockSpec((tm, d), lambda i, e: (i, 0)),
                  pl.BlockSpec((1, d), lambda i, e: (0, 0)),
                  pl.BlockSpec((d, LANES), lambda i, e: (0, 0)),
                  pl.BlockSpec((1, LANES), lambda i, e: (0, 0)),
                  pl.BlockSpec((1, d, f), lambda i, e: (e, 0, 0)),
                  pl.BlockSpec((1, d, f), lambda i, e: (e, 0, 0)),
                  pl.BlockSpec((1, f, d), lambda i, e: (e, 0, 0))],
        out_specs=pl.BlockSpec((tm, d), lambda i, e: (i, 0)),
        out_shape=jax.ShapeDtypeStruct((m, d), F32),
        scratch_shapes=[pltpu.VMEM((tm, d), BF16), pltpu.VMEM((tm, LANES), F32), pltpu.VMEM((tm, d), F32)],
        compiler_params=_params("parallel", "arbitrary"),
        name="moe",
    )(x, gain.reshape(1, d), wr, br, w_gate, w_up, w_down)


AB_SPLITS = [(0, A_WIDTH), (A_WIDTH, A_WIDTH), (2 * A_WIDTH, A_WIDTH), (3 * A_WIDTH, 2 * B_WIDTH),
             (3 * A_WIDTH + 2 * B_WIDTH, B_WIDTH), (3 * A_WIDTH + 3 * B_WIDTH, B_WIDTH),
             (3 * A_WIDTH + 4 * B_WIDTH, LANES)]


def _even_projection(x, gain, w_in_bf):
    return fused_linear([x], [w_in_bf], splits=AB_SPLITS, gain=gain, name="in_proj_ab")


def _pad_seq(a, lp):
    return jnp.pad(a, ((0, 0), (0, lp - a.shape[1]), (0, 0)))


def kernel(x_prompt, x_sample, mem_prompt, cache_moba_k, cache_moba_v, cache_mla_kv, cache_mla_pe, state_mlstm_c, state_mlstm_n, state_mlstm_m, state_conv, cache_mem_k, cache_mem_v, page_table, norm_mix, norm_cross, norm_mem, norm_ffn, norm_final, w_in_ab, b_gates, conv_w, norm_mlstm, w_out_ab, w_in_c, norm_q_lat, norm_kv_lat, w_qb, w_kb, w_vb, w_out_c, w_cq, w_ck, w_cv, w_co, w_group, b_group, w_router, b_router, w_up, w_gate, w_down):
    bp, lp, d = x_prompt.shape
    bs, ls, _ = x_sample.shape
    depth = norm_mix.shape[0]
    mp, ms = bp * lp, bs * ls
    n_mem = mem_prompt.shape[1]
    past = page_table.shape[1] * PAGE_SIZE
    pos_p = jnp.tile(jnp.arange(lp, dtype=jnp.int32), bp)
    pos_s = jnp.tile(past + jnp.arange(ls, dtype=jnp.int32), bs)

    xp = x_prompt.reshape(mp, d)
    xs = x_sample.reshape(ms, d)
    mem = mem_prompt.reshape(bp * n_mem, d)
    ak_p, av_p, kv_p, pe_p, mc_p, mn_p, mm_p, cv_p, mk_p, mv_p = [], [], [], [], [], [], [], [], [], []
    ak_s, av_s, kv_s, pe_s, mc_s, mn_s, mm_s, cv_s = [], [], [], [], [], [], [], []
    back = CONV_W - 1
    ch = MLSTM_CHUNK
    ls_pad = -(-ls // ch) * ch
    for l in range(depth):
        j = l // 2
        if l % 2 == 0:
            n_ab = w_in_ab.shape[2]
            n_pad = AB_SPLITS[-1][0] + LANES
            w_in_bf = jnp.pad(w_in_ab[j], ((0, 0), (0, n_pad - n_ab))).astype(BF16)
            w_out_bf = w_out_ab[j].astype(BF16)
            qa, ka, va, qkb, vb, ob, gt = _even_projection(xp, norm_mix[l], w_in_bf)
            o_a = moba_prompt(qa.reshape(bp, lp, A_WIDTH), ka.reshape(bp, lp, A_WIDTH), va.reshape(bp, lp, A_WIDTH))
            qkb3 = qkb.reshape(bp, lp, 2 * B_WIDTH)
            hb, mc, mn, mm = mlstm_layer(
                qkb3, vb.reshape(bp, lp, B_WIDTH), ob.reshape(bp, lp, B_WIDTH), gt.reshape(bp, lp, LANES),
                b_gates[j], conv_w[j], norm_mlstm[j], jnp.zeros((bp, back, 2 * B_WIDTH), F32),
                jnp.zeros((bp, B_HEADS, B_HEAD_DIM, B_HEAD_DIM), F32), jnp.zeros((bp, B_HEADS, B_HEAD_DIM), F32),
                jnp.zeros((bp, B_HEADS), F32), valid=lp)
            (xp,) = fused_linear([o_a.reshape(mp, A_WIDTH), hb.reshape(mp, B_WIDTH)],
                                 [w_out_bf[:A_WIDTH], w_out_bf[A_WIDTH:]], residual=xp, name="out_proj_ab")
            ak_p.append(ka.reshape(bp, lp, A_HEADS, A_HEAD_DIM)); av_p.append(va.reshape(bp, lp, A_HEADS, A_HEAD_DIM))
            cv_p.append(qkb3[:, lp - back:, :]); mc_p.append(mc); mn_p.append(mn); mm_p.append(mm)
            qa, ka, va, qkb, vb, ob, gt = _even_projection(xs, norm_mix[l], w_in_bf)
            o_a = moba_sample(qa.reshape(bs, ls, A_WIDTH), ka.reshape(bs, ls, A_WIDTH), va.reshape(bs, ls, A_WIDTH),
                              cache_moba_k, cache_moba_v, j, page_table)
            qkb3 = qkb.reshape(bs, ls, 2 * B_WIDTH)
            hb, mc, mn, mm = mlstm_layer(
                _pad_seq(qkb3, ls_pad), _pad_seq(vb.reshape(bs, ls, B_WIDTH), ls_pad),
                _pad_seq(ob.reshape(bs, ls, B_WIDTH), ls_pad), _pad_seq(gt.reshape(bs, ls, LANES), ls_pad),
                b_gates[j], conv_w[j], norm_mlstm[j], state_conv[j],
                state_mlstm_c[j], state_mlstm_n[j], state_mlstm_m[j], valid=ls)
            (xs,) = fused_linear([o_a.reshape(ms, A_WIDTH), hb[:, :ls].reshape(ms, B_WIDTH)],
                                 [w_out_bf[:A_WIDTH], w_out_bf[A_WIDTH:]], residual=xs, name="out_proj_ab")
            ak_s.append(ka.reshape(bs, ls, A_HEADS, A_HEAD_DIM)); av_s.append(va.reshape(bs, ls, A_HEADS, A_HEAD_DIM))
            conv_all = jnp.concatenate([state_conv[j], qkb3], axis=1)
            cv_s.append(conv_all[:, ls:, :]); mc_s.append(mc); mn_s.append(mn); mm_s.append(mm)
        else:
            wvb = w_vb[j].transpose(1, 0, 2).astype(BF16)
            w_out_bf = w_out_c[j].astype(BF16)
            ckv, kpe, kcat, kvt, qt = mla_prep(xp, norm_mix[l], w_in_c[j], norm_q_lat[j], norm_kv_lat[j], w_qb[j], w_kb[j], pos_p)
            o = mla_flash_prompt(qt, kcat, kvt, w_vb[j].transpose(1, 2, 0).astype(BF16), bp)
            (xp,) = fused_linear([o], [w_out_bf], residual=xp, name="out_proj_c")
            kv_p.append(ckv.reshape(bp, lp, KV_LORA)); pe_p.append(kpe.reshape(bp, lp, ROPE_DIM))
            ckv, kpe, kcat, _, qt = mla_prep(xs, norm_mix[l], w_in_c[j], norm_q_lat[j], norm_kv_lat[j], w_qb[j], w_kb[j], pos_s)
            o_lat = mla_sample(qt, kcat, cache_mla_kv, cache_mla_pe, j, page_table, ls)
            o = head_value_proj(o_lat, wvb)
            (xs,) = fused_linear([o], [w_out_bf], residual=xs, name="out_proj_c")
            kv_s.append(ckv.reshape(bs, ls, KV_LORA)); pe_s.append(kpe.reshape(bs, ls, ROPE_DIM))
        w_ckv = jnp.concatenate([w_ck[l], w_cv[l]], axis=1).astype(BF16)
        mk, mv = fused_linear([mem], [w_ckv], splits=[(0, X_WIDTH), (X_WIDTH, X_WIDTH)], gain=norm_mem[l], name="mem_kv")
        mk_p.append(mk.reshape(bp, n_mem, X_HEADS, X_HEAD_DIM)); mv_p.append(mv.reshape(bp, n_mem, X_HEADS, X_HEAD_DIM))
        w_cq_bf, w_co_bf = w_cq[l].astype(BF16), w_co[l].astype(BF16)
        (q,) = fused_linear([xp], [w_cq_bf], gain=norm_cross[l], name="cross_q")
        o = cross_core(q.reshape(bp, lp, X_WIDTH), mk.reshape(bp, n_mem, X_WIDTH), mv.reshape(bp, n_mem, X_WIDTH))
        (xp,) = fused_linear([o.reshape(mp, X_WIDTH)], [w_co_bf], residual=xp, name="cross_out")
        (q,) = fused_linear([xs], [w_cq_bf], gain=norm_cross[l], name="cross_q")
        q8 = _pad_seq(q.reshape(bs, ls, X_WIDTH), 8)
        o = cross_core(q8, cache_mem_k[l].reshape(bs, n_mem, X_WIDTH), cache_mem_v[l].reshape(bs, n_mem, X_WIDTH))
        (xs,) = fused_linear([o[:, :ls].reshape(ms, X_WIDTH)], [w_co_bf], residual=xs, name="cross_out")
        wu, wg, wd = w_up[l].astype(BF16), w_gate[l].astype(BF16), w_down[l].astype(BF16)
        xp = moe_layer(xp, norm_ffn[l], w_group[l], b_group[l], w_router[l], b_router[l], wu, wg, wd)
        xs = moe_layer(xs, norm_ffn[l], w_group[l], b_group[l], w_router[l], b_router[l], wu, wg, wd)
    y_prompt = rmsnorm_rows(xp, norm_final).reshape(bp, lp, d)
    y_sample = rmsnorm_rows(xs, norm_final).reshape(bs, ls, d)
    return (y_prompt, y_sample,
            jnp.stack(ak_p), jnp.stack(av_p), jnp.stack(kv_p), jnp.stack(pe_p),
            jnp.stack(mc_p), jnp.stack(mn_p), jnp.stack(mm_p), jnp.stack(cv_p), jnp.stack(mk_p), jnp.stack(mv_p),
            jnp.stack(ak_s), jnp.stack(av_s), jnp.stack(kv_s), jnp.stack(pe_s),
            jnp.stack(mc_s), jnp.stack(mn_s), jnp.stack(mm_s), jnp.stack(cv_s))
```

```python
import functools
import math

import jax
import jax.numpy as jnp
from jax import lax
from jax.experimental import pallas as pl
from jax.experimental.pallas import tpu as pltpu

F32 = jnp.float32
BF16 = jnp.bfloat16
HI = lax.Precision.HIGHEST
EPS = 1e-6
NEG = -1e30
VMEM_LIMIT = 56 * 1024 * 1024
LANES = 128

PAGE_SIZE = 128
A_HEADS, A_HEAD_DIM = 8, 64
A_WIDTH = A_HEADS * A_HEAD_DIM
MOBA_BLOCK, MOBA_TOPK = 256, 3
B_HEADS, B_HEAD_DIM = 4, 128
B_WIDTH = B_HEADS * B_HEAD_DIM
CONV_W = 4
MLSTM_CHUNK = 64
C_HEADS, Q_LORA, KV_LORA, NOPE_DIM, ROPE_DIM, V_DIM = 16, 256, 128, 64, 32, 64
ROPE_THETA = 10000.0
X_HEADS, X_HEAD_DIM = 4, 128
X_WIDTH = X_HEADS * X_HEAD_DIM
N_GROUPS, EXPERTS_PER_GROUP = 4, 4
N_EXPERTS = N_GROUPS * EXPERTS_PER_GROUP
PAGES_PER_STEP = 16
MOBA_PAGES_PER_STEP = 16

NT_DIMS = (((1,), (1,)), ((), ()))
TN_DIMS = (((0,), (0,)), ((), ()))


def _params(*sem):
    return pltpu.CompilerParams(dimension_semantics=sem, vmem_limit_bytes=VMEM_LIMIT)


def _nt(a, b, precision=None):
    return lax.dot_general(a, b, NT_DIMS, precision=precision, preferred_element_type=F32)


def _dot(a, b, precision=None):
    return jnp.dot(a, b, precision=precision, preferred_element_type=F32)


def _rms(x, g):
    return x * lax.rsqrt(jnp.mean(x * x, axis=-1, keepdims=True) + EPS) * g


def _sigmoid(x):
    return 1.0 / (1.0 + jnp.exp(-x))


def _linear_body(*refs, n_in, has_gain, has_res, splits):
    x_refs, w_refs = refs[:n_in], refs[n_in:2 * n_in]
    p = 2 * n_in
    g_ref = refs[p] if has_gain else None
    p += int(has_gain)
    r_ref = refs[p] if has_res else None
    p += int(has_res)
    o_refs = refs[p:]
    xs = []
    for xr in x_refs:
        x = xr[...]
        if has_gain:
            x = _rms(x, g_ref[...])
        xs.append(x.astype(BF16))
    for (off, width), o_ref in zip(splits, o_refs):
        acc = None
        for x, wr in zip(xs, w_refs):
            y = _dot(x, wr[:, off:off + width])
            acc = y if acc is None else acc + y
        if has_res:
            acc = acc + r_ref[...]
        o_ref[...] = acc.astype(o_ref.dtype)


def fused_linear(xs, ws, splits=None, gain=None, residual=None, tm=512, name="linear"):
    m, n = xs[0].shape[0], ws[0].shape[1]
    splits = splits or [(0, n)]
    tm = min(tm, m)
    assert m % tm == 0
    in_specs = [pl.BlockSpec((tm, x.shape[1]), lambda i: (i, 0)) for x in xs]
    in_specs += [pl.BlockSpec(w.shape, lambda i: (0, 0)) for w in ws]
    args = list(xs) + list(ws)
    if gain is not None:
        in_specs.append(pl.BlockSpec((1, gain.shape[-1]), lambda i: (0, 0)))
        args.append(gain.reshape(1, -1))
    if residual is not None:
        assert len(splits) == 1
        in_specs.append(pl.BlockSpec((tm, n), lambda i: (i, 0)))
        args.append(residual)
    outs = pl.pallas_call(
        functools.partial(_linear_body, n_in=len(xs), has_gain=gain is not None, has_res=residual is not None,
                          splits=tuple(splits)),
        grid=(m // tm,),
        in_specs=in_specs,
        out_specs=[pl.BlockSpec((tm, w), lambda i: (i, 0)) for _, w in splits],
        out_shape=[jax.ShapeDtypeStruct((m, w), F32) for _, w in splits],
        compiler_params=_params("parallel"),
        name=name,
    )(*args)
    return outs


def _rmsnorm_body(x_ref, g_ref, o_ref):
    o_ref[...] = _rms(x_ref[...], g_ref[...])


def rmsnorm_rows(x, g, tm=1024):
    m, d = x.shape
    tm = min(tm, m)
    return pl.pallas_call(
        _rmsnorm_body,
        grid=(m // tm,),
        in_specs=[pl.BlockSpec((tm, d), lambda i: (i, 0)), pl.BlockSpec((1, d), lambda i: (0, 0))],
        out_specs=pl.BlockSpec((tm, d), lambda i: (i, 0)),
        out_shape=jax.ShapeDtypeStruct((m, d), F32),
        compiler_params=_params("parallel"),
        name="final_norm",
    )(x, g.reshape(1, d))


def _topk_mask(g, valid, n_iota, nb, topk):
    gm = jnp.where(valid, g, -jnp.inf)
    rank = jnp.zeros(g.shape, jnp.int32)
    for m in range(nb):
        gc = gm[:, m:m + 1]
        beats = (gc > gm) | ((gc == gm) & (m < n_iota))
        rank = rank + beats.astype(jnp.int32)
    return ((rank < topk) & valid).astype(F32)


def _topk_rows(g, valid, n_iota, nb, topk):
    gm = jnp.where(valid, g, -jnp.inf)
    rank = jnp.zeros(g.shape, jnp.int32)
    for m in range(nb):
        gr = gm[m:m + 1, :]
        beats = (gr > gm) | ((gr == gm) & (m < n_iota))
        rank = rank + beats.astype(jnp.int32)
    return ((rank < topk) & valid).astype(F32)


def _moba_prompt_body(q_ref, k_ref, v_ref, o_ref, kmean_s, kh_s, vt_s, qt_s, sel_s, ml_s, acc_s, ot_s, *,
                      nb, blk, heads, dh, topk):
    i = pl.program_id(1)

    @pl.when(i == 0)
    def _():
        for n in range(nb):
            rows = slice(n * blk, (n + 1) * blk)
            kmean_s[n:n + 1, :] = jnp.mean(k_ref[0, rows, :], axis=0, keepdims=True)
            vt_s[:, rows] = v_ref[0, rows, :].T.astype(BF16)
        for h in range(heads):
            kh_s[h] = k_ref[0, :, h * dh:(h + 1) * dh].astype(BF16)

    qt = q_ref[0].T
    krow = lax.broadcasted_iota(jnp.int32, (blk, blk), 0)
    qcol = lax.broadcasted_iota(jnp.int32, (blk, blk), 1)
    dmat = (qcol - krow).astype(F32)
    n_iota = lax.broadcasted_iota(jnp.int32, (nb, blk), 0)
    slopes = [2.0 ** (-8.0 * (h + 1) / heads) for h in range(heads)]
    head_rows = [slice(h * dh, (h + 1) * dh) for h in range(heads)]
    for h in range(heads):
        gate_t = _dot(kmean_s[:, head_rows[h]].astype(BF16), qt[head_rows[h]].astype(BF16))
        sel_s[h] = _topk_rows(gate_t, n_iota < i, n_iota, nb, topk)
    qt_s[...] = (qt * dh ** -0.5).astype(BF16)

    def block_step(kstart, logits_fn, first):
        s_all = [_dot(kh_s[h, pl.ds(kstart, blk), :], qt_s[head_rows[h], :]) for h in range(heads)]
        for h in range(heads):
            logits = logits_fn(h, s_all[h])
            vt = vt_s[head_rows[h], pl.ds(kstart, blk)]
            m_row, l_row = ml_s.at[2 * h:2 * h + 1, :], ml_s.at[2 * h + 1:2 * h + 2, :]
            if first:
                m_new = jnp.max(logits, axis=0, keepdims=True)
                p = jnp.exp(logits - m_new)
                acc_s[h] = _dot(vt, p.astype(BF16))
                l_row[...] = jnp.sum(p, axis=0, keepdims=True)
            else:
                m = m_row[...]
                m_new = jnp.maximum(m, jnp.max(logits, axis=0, keepdims=True))
                alpha = jnp.exp(m - m_new)
                p = jnp.exp(logits - m_new)
                acc_s[h] = alpha * acc_s[h] + _dot(vt, p.astype(BF16))
                l_row[...] = alpha * l_row[...] + jnp.sum(p, axis=0, keepdims=True)
            m_row[...] = m_new

    block_step(pl.multiple_of(i * blk, blk), lambda h, s: jnp.where(dmat >= 0, s - slopes[h] * dmat, NEG), True)

    def body(j, carry):
        dist = dmat + ((i - j) * blk).astype(F32)
        pick = n_iota == j

        def logits_fn(h, s):
            selrow = jnp.sum(jnp.where(pick, sel_s[h], 0.0), axis=0, keepdims=True)
            return jnp.where(selrow > 0.5, s - slopes[h] * dist, NEG)

        block_step(pl.multiple_of(j * blk, blk), logits_fn, False)
        return carry

    lax.fori_loop(0, i, body, 0)
    for h in range(heads):
        ot_s[head_rows[h], :] = acc_s[h] / ml_s[2 * h + 1:2 * h + 2, :]
    o_ref[0] = ot_s[...].T


def moba_prompt(q, k, v):
    b, l, w = q.shape
    blk = MOBA_BLOCK
    nb = l // blk
    heads, dh = A_HEADS, A_HEAD_DIM
    return pl.pallas_call(
        functools.partial(_moba_prompt_body, nb=nb, blk=blk, heads=heads, dh=dh, topk=MOBA_TOPK),
        grid=(b, nb),
        in_specs=[pl.BlockSpec((1, blk, w), lambda bi, i: (bi, i, 0)),
                  pl.BlockSpec((1, l, w), lambda bi, i: (bi, 0, 0)),
                  pl.BlockSpec((1, l, w), lambda bi, i: (bi, 0, 0))],
        out_specs=pl.BlockSpec((1, blk, w), lambda bi, i: (bi, i, 0)),
        out_shape=jax.ShapeDtypeStruct((b, l, w), F32),
        scratch_shapes=[pltpu.VMEM((nb, w), F32), pltpu.VMEM((heads, l, dh), BF16), pltpu.VMEM((w, l), BF16),
                        pltpu.VMEM((w, blk), BF16), pltpu.VMEM((heads, nb, blk), F32),
                        pltpu.VMEM((2 * heads, blk), F32), pltpu.VMEM((heads, dh, blk), F32),
                        pltpu.VMEM((w, blk), F32)],
        compiler_params=_params("parallel", "arbitrary"),
        name="moba_prompt",
    )(q, k, v)


def _moba_scores_body(pt_ref, q_ref, *rest, n_u, heads):
    k_refs, (s_ref, ksum_ref) = rest[:n_u], rest[n_u:]
    c = pl.program_id(1)

    @pl.when(c == 0)
    def _():
        ksum_ref[...] = jnp.zeros(ksum_ref.shape, F32)

    ppb = MOBA_BLOCK // PAGE_SIZE
    dh = q_ref.shape[-1]
    lane = lax.broadcasted_iota(jnp.int32, (dh, LANES), 1)
    for h in range(heads):
        qh = q_ref[0, h].astype(BF16)
        ks = ksum_ref[0, h]
        for n in range(n_u // ppb):
            bsum = None
            for u in range(n * ppb, (n + 1) * ppb):
                kt = k_refs[u][0, h]
                s_ref[0, h, :, u * PAGE_SIZE:(u + 1) * PAGE_SIZE] = _dot(qh, kt.astype(BF16))
                bsum = kt if bsum is None else bsum + kt
            ks = jnp.where(lane == c * (n_u // ppb) + n, jnp.sum(bsum, axis=1, keepdims=True), ks)
        ksum_ref[0, h] = ks


def _moba_select_body(ksum_ref, q_ref, idx_ref, *, nb, topk, heads):
    lane = lax.broadcasted_iota(jnp.int32, (8, LANES), 1)
    valid = lane < nb
    lane_f = lane.astype(F32)
    for h in range(heads):
        kmean_t = (ksum_ref[0, h] * (1.0 / MOBA_BLOCK)).astype(BF16)
        gate = jnp.where(valid, _dot(q_ref[0, h].astype(BF16), kmean_t), -jnp.inf)
        rank = jnp.zeros(gate.shape, jnp.int32)
        for m in range(nb):
            gc = gate[:, m:m + 1]
            rank = rank + ((gc > gate) | ((gc == gate) & (m < lane))).astype(jnp.int32)
        out = jnp.zeros((8, LANES), F32)
        for k in range(topk):
            idx_k = jnp.sum(jnp.where((rank == k) & valid, lane_f, 0.0), axis=1, keepdims=True)
            out = out + jnp.where(lane == k, idx_k, 0.0)
        idx_ref[0, h] = out.astype(jnp.int32)


def _moba_gather_body(pt_ref, sel_ref, s_ref, q_ref, knew_ref, vnew_ref, v_hbm, o_ref, vbuf, sem, *,
                      layer, lq, topk, heads, dh, past):
    b, h = pl.program_id(0), pl.program_id(1)
    n_h = pl.num_programs(1)
    step = b * n_h + h
    n_steps = pl.num_programs(0) * n_h
    slot = step % 2
    ppb = MOBA_BLOCK // PAGE_SIZE
    n_sel = lq * topk
    t_all = n_sel * MOBA_BLOCK

    def copies(bb, hh, sl):
        out = []
        for j in range(n_sel):
            blk = sel_ref[bb, hh * n_sel + j]
            for pg in range(ppb):
                page = pt_ref[bb, blk * ppb + pg]
                cols = pl.ds((j * ppb + pg) * PAGE_SIZE, PAGE_SIZE)
                out.append(pltpu.make_async_copy(v_hbm.at[layer, page, hh], vbuf.at[sl, :, cols], sem.at[sl]))
        return out

    @pl.when(step == 0)
    def _():
        for c in copies(b, h, slot):
            c.start()

    @pl.when(step + 1 < n_steps)
    def _():
        nxt = step + 1
        for c in copies(nxt // n_h, nxt % n_h, 1 - slot):
            c.start()

    scale = dh ** -0.5
    slope = jnp.exp2((-8.0 / heads) * (h + 1).astype(F32))
    row = lax.broadcasted_iota(jnp.int32, (8, t_all), 0)
    col = lax.broadcasted_iota(jnp.int32, (8, t_all), 1)
    choice = col // MOBA_BLOCK
    blocks = [sel_ref[b, h * n_sel + j] for j in range(n_sel)]
    s = jnp.concatenate([s_ref[0, 0, :, pl.ds(pl.multiple_of(blk * MOBA_BLOCK, MOBA_BLOCK), MOBA_BLOCK)]
                         for blk in blocks], axis=1)
    kpos = jnp.zeros((8, t_all), jnp.int32)
    for j in range(n_sel):
        kpos = jnp.where(choice == j, blocks[j] * MOBA_BLOCK, kpos)
    kpos = kpos + (col - choice * MOBA_BLOCK)
    dist = ((past + row) - kpos).astype(F32)
    logits = jnp.where((choice // topk) == row, s * scale - slope * dist, NEG)
    row1 = lax.broadcasted_iota(jnp.int32, (8, 1), 0)
    q8, knew, vnew = q_ref[0, 0], knew_ref[0, 0], vnew_ref[0, 0]
    own = []
    for t in range(lq):
        so = jnp.sum(q8 * knew[t:t + 1, :], axis=1, keepdims=True) * scale - slope * (row1 - t).astype(F32)
        own.append(jnp.where(row1 >= t, so, NEG))
    m = jnp.max(logits, axis=1, keepdims=True)
    for so in own:
        m = jnp.maximum(m, so)
    p = jnp.exp(logits - m)
    l = jnp.sum(p, axis=1, keepdims=True)

    for c in copies(b, h, slot):
        c.wait()
    acc = _nt(p.astype(BF16), vbuf[slot].astype(BF16))
    for t, so in enumerate(own):
        po = jnp.exp(so - m)
        l = l + po
        acc = acc + po * vnew[t:t + 1, :]
    o_ref[0, 0] = acc / l


def moba_sample(q, k_new, v_new, cache_k, cache_v, layer, page_table):
    b, lq, w = q.shape
    heads, dh = A_HEADS, A_HEAD_DIM
    n_pages = page_table.shape[1]
    past = n_pages * PAGE_SIZE
    nb = past // MOBA_BLOCK
    assert past % MOBA_BLOCK == 0 and MOBA_TOPK <= nb <= LANES and lq <= 8
    n_u = MOBA_PAGES_PER_STEP
    ppb = MOBA_BLOCK // PAGE_SIZE
    ck_t = cache_k.transpose(0, 1, 3, 4, 2)
    cv_t = cache_v.transpose(0, 1, 3, 4, 2)

    def per_head(a):
        a = a.reshape(b, lq, heads, dh).transpose(0, 2, 1, 3)
        return jnp.pad(a, ((0, 0), (0, 0), (0, 8 - lq), (0, 0)))

    qh = per_head(q)
    page_specs = [pl.BlockSpec((None, 1, heads, dh, PAGE_SIZE),
                               lambda bi, c, pt, u=u: (layer, pt[bi, c * n_u + u], 0, 0, 0)) for u in range(n_u)]
    scores, ksum = pl.pallas_call(
        functools.partial(_moba_scores_body, n_u=n_u, heads=heads),
        grid_spec=pltpu.PrefetchScalarGridSpec(
            num_scalar_prefetch=1, grid=(b, n_pages // n_u),
            in_specs=[pl.BlockSpec((1, heads, 8, dh), lambda bi, c, pt: (bi, 0, 0, 0))] + page_specs,
            out_specs=[pl.BlockSpec((1, heads, 8, n_u * PAGE_SIZE), lambda bi, c, pt: (bi, 0, 0, c)),
                       pl.BlockSpec((1, heads, dh, LANES), lambda bi, c, pt: (bi, 0, 0, 0))]),
        out_shape=[jax.ShapeDtypeStruct((b, heads, 8, past), F32), jax.ShapeDtypeStruct((b, heads, dh, LANES), F32)],
        compiler_params=_params("parallel", "arbitrary"),
        name="moba_sample_scores",
    )(page_table, qh, *([ck_t] * n_u))

    sel = pl.pallas_call(
        functools.partial(_moba_select_body, nb=nb, topk=MOBA_TOPK, heads=heads),
        grid=(b,),
        in_specs=[pl.BlockSpec((1, heads, dh, LANES), lambda bi: (bi, 0, 0, 0)),
                  pl.BlockSpec((1, heads, 8, dh), lambda bi: (bi, 0, 0, 0))],
        out_specs=pl.BlockSpec((1, heads, 8, LANES), lambda bi: (bi, 0, 0, 0)),
        out_shape=jax.ShapeDtypeStruct((b, heads, 8, LANES), jnp.int32),
        compiler_params=_params("parallel"),
        name="moba_sample_select",
    )(ksum, qh)
    n_sel = lq * MOBA_TOPK
    sel = sel[:, :, :lq, :MOBA_TOPK].reshape(b, heads * n_sel)

    head_spec = pl.BlockSpec((1, 1, 8, dh), lambda bi, hi, pt, sl: (bi, hi, 0, 0))
    o = pl.pallas_call(
        functools.partial(_moba_gather_body, layer=layer, lq=lq, topk=MOBA_TOPK, heads=heads, dh=dh, past=past),
        grid_spec=pltpu.PrefetchScalarGridSpec(
            num_scalar_prefetch=2, grid=(b, heads),
            in_specs=[pl.BlockSpec((1, 1, 8, past), lambda bi, hi, pt, sl: (bi, hi, 0, 0)),
                      head_spec, head_spec, head_spec, pl.BlockSpec(memory_space=pl.ANY)],
            out_specs=head_spec,
            scratch_shapes=[pltpu.VMEM((2, dh, n_sel * MOBA_BLOCK), F32), pltpu.SemaphoreType.DMA((2,))]),
        out_shape=jax.ShapeDtypeStruct((b, heads, 8, dh), F32),
        compiler_params=_params("arbitrary", "arbitrary"),
        name="moba_sample_gather",
    )(page_table, sel, scores, qh, per_head(k_new), per_head(v_new), cv_t)
    return o[:, :, :lq, :].transpose(0, 2, 1, 3).reshape(b, lq, w)


def _log_sigmoid(x):
    return -(jnp.maximum(-x, 0.0) + jnp.log1p(jnp.exp(-jnp.abs(x))))


def _mlstm_body(qk_ref, v_ref, og_ref, gcol_ref, grow_ref, bcol_ref, brow_ref, cw_ref, gh_ref, cbuf_ref,
                c0_ref, n0_ref, m0_ref, h_ref, c_out, n_out, m_out, xbuf, c_s, n_s, m_s, *, ch, valid, heads, dh):
    c = pl.program_id(1)
    width = heads * dh

    @pl.when(c == 0)
    def _():
        xbuf[0:8, :] = cbuf_ref[0]
        c_s[...] = c0_ref[0]
        n_s[...] = n0_ref[0]
        m_s[...] = m0_ref[0]

    xbuf[8:8 + ch, :] = qk_ref[0]
    cw = cw_ref[...]
    back = CONV_W - 1
    y = xbuf[8 - back:8 - back + ch, :] * cw[0:1, :]
    for t in range(1, CONV_W):
        y = y + xbuf[8 - back + t:8 - back + t + ch, :] * cw[t:t + 1, :]
    xbuf[8 - back:8, :] = xbuf[8 + ch - back:8 + ch, :]
    y = y * _sigmoid(y)

    t_col = lax.broadcasted_iota(jnp.int32, (ch, LANES), 0)
    t_row = lax.broadcasted_iota(jnp.int32, (8, ch), 1)
    gcol = gcol_ref[0] + bcol_ref[...]
    grow = grow_ref[0, 0] + brow_ref[:, :ch]
    ig_col = jnp.where(t_col < valid, gcol, NEG)
    lf_col = jnp.where(t_col < valid, _log_sigmoid(gcol), 0.0)
    ig_row = jnp.where(t_row < valid, grow, NEG)
    lf_row = jnp.where(t_row < valid, _log_sigmoid(grow), 0.0)
    ti = lax.broadcasted_iota(jnp.int32, (ch, ch), 0)
    si = lax.broadcasted_iota(jnp.int32, (ch, ch), 1)
    causal = ti >= si
    b_col = _dot(causal.astype(F32), lf_col, HI)
    b_row = _dot(lf_row, (ti <= si).astype(F32), HI)

    v = v_ref[0]
    og = og_ref[0]
    gh = gh_ref[...]
    qhs = [y[:, h * dh:(h + 1) * dh] for h in range(heads)]
    khs = [y[:, width + h * dh:width + (h + 1) * dh] * (dh ** -0.5) for h in range(heads)]
    qk_all = [_nt(qhs[h].astype(BF16), khs[h].astype(BF16)) for h in range(heads)]
    qc_all = [_dot(qhs[h].astype(BF16), c_s[h].astype(BF16)) for h in range(heads)]
    for h in range(heads):
        sl = slice(h * dh, (h + 1) * dh)
        qh, kh = qhs[h], khs[h]
        vh = v[:, sl]
        igc, bc = ig_col[:, h:h + 1], b_col[:, heads + h:heads + h + 1]
        igr, br = ig_row[h:h + 1, :], b_row[heads + h:heads + h + 1, :]
        m_prev = m_s[h][:, 0:1]
        c_prev = c_s[h]
        n_prev = n_s[h]
        log_d = jnp.where(causal, bc - br + igr, NEG)
        inter = bc + m_prev
        mt = jnp.maximum(inter, jnp.max(log_d, axis=1, keepdims=True))
        d = jnp.exp(log_d - mt)
        w_inter = jnp.exp(inter - mt)
        vb = vh.astype(BF16)
        a = qk_all[h] * d
        num = _dot(a.astype(BF16), vb) + w_inter * qc_all[h]
        den = jnp.sum(a, axis=1, keepdims=True) + w_inter * jnp.sum(qh * n_prev, axis=1, keepdims=True)
        hh = num / jnp.maximum(jnp.abs(den), jnp.exp(-mt))
        b_last = bc[ch - 1:ch, :]
        logw = b_last - bc + igc
        m_new = jnp.maximum(b_last + m_prev, jnp.max(logw, axis=0, keepdims=True))
        ws = jnp.exp(logw - m_new)
        decay = jnp.exp(b_last + m_prev - m_new)
        kw = kh * ws
        c_s[h] = decay * c_prev + lax.dot_general(kw.astype(BF16), vb, TN_DIMS, preferred_element_type=F32)
        n_s[h] = decay * n_prev + jnp.sum(kw, axis=0, keepdims=True)
        m_s[h] = jnp.broadcast_to(m_new, (1, LANES))
        hn = _rms(hh, gh[:, sl])
        h_ref[0, :, sl] = hn * _sigmoid(og[:, sl])

    @pl.when(c == pl.num_programs(1) - 1)
    def _():
        c_out[0] = c_s[...]
        n_out[0] = n_s[...]
        m_out[0] = m_s[...]


def mlstm_layer(qk, v, og, gates, b_g, conv_w, g_h, conv_buf, c0, n0, m0, valid):
    b, lp, w2 = qk.shape
    w = w2 // 2
    heads, dh, ch = B_HEADS, B_HEAD_DIM, MLSTM_CHUNK
    nc = lp // ch
    assert valid == lp or nc == 1
    grow = gates[:, :, :8].reshape(b, nc, ch, 8).transpose(0, 1, 3, 2)
    bcol = jnp.pad(b_g, (0, LANES - 8)).reshape(1, LANES)
    brow = jnp.broadcast_to(b_g[:, None], (8, LANES))
    cw = jnp.pad(conv_w, ((0, 8 - CONV_W), (0, 0)))
    cbuf = jnp.pad(conv_buf, ((0, 0), (8 - (CONV_W - 1), 0), (0, 0)))
    n0 = n0.reshape(b, heads, 1, dh)
    m0 = jnp.broadcast_to(m0[:, :, None, None], (b, heads, 1, LANES))
    full = lambda shape: pl.BlockSpec(shape, lambda bi, c: (0,) * len(shape))
    per_b = lambda shape: pl.BlockSpec(shape, lambda bi, c: (bi,) + (0,) * (len(shape) - 1))
    seq = lambda width: pl.BlockSpec((1, ch, width), lambda bi, c: (bi, c, 0))
    h, c_f, n_f, m_f = pl.pallas_call(
        functools.partial(_mlstm_body, ch=ch, valid=valid if nc == 1 else ch, heads=heads, dh=dh),
        grid=(b, nc),
        in_specs=[seq(w2), seq(w), seq(w), seq(LANES),
                  pl.BlockSpec((1, 1, 8, ch), lambda bi, c: (bi, c, 0, 0)),
                  full((1, LANES)), full((8, LANES)), full((8, w2)), full((1, w)),
                  per_b((1, 8, w2)), per_b((1, heads, dh, dh)), per_b((1, heads, 1, dh)), per_b((1, heads, 1, LANES))],
        out_specs=[seq(w), per_b((1, heads, dh, dh)), per_b((1, heads, 1, dh)), per_b((1, heads, 1, LANES))],
        out_shape=[jax.ShapeDtypeStruct((b, lp, w), F32), jax.ShapeDtypeStruct((b, heads, dh, dh), F32),
                   jax.ShapeDtypeStruct((b, heads, 1, dh), F32), jax.ShapeDtypeStruct((b, heads, 1, LANES), F32)],
        scratch_shapes=[pltpu.VMEM((8 + ch, w2), F32), pltpu.VMEM((heads, dh, dh), F32),
                        pltpu.VMEM((heads, 1, dh), F32), pltpu.VMEM((heads, 1, LANES), F32)],
        compiler_params=_params("parallel", "arbitrary"),
        name="mlstm",
    )(qk, v, og, gates, grow, bcol, brow, cw, g_h.reshape(1, w), cbuf, c0, n0, m0)
    return h, c_f, n_f.reshape(b, heads, dh), m_f[:, :, 0, 0]


def _mla_prep_body(x_ref, g_ref, win_ref, gq_ref, gkv_ref, wnt_ref, wat_ref, wbt_ref, wkb_ref,
                   cosk_ref, sink_ref, cost_ref, sint_ref,
                   ckv_ref, kpe_ref, kcat_ref, kvt_ref, qt_ref, *, heads):
    xn = _rms(x_ref[...], g_ref[...]).astype(BF16)
    y = _dot(xn, win_ref[...])
    ckv = _rms(y[:, Q_LORA:Q_LORA + KV_LORA], gkv_ref[...])
    k0 = Q_LORA + KV_LORA
    kpe = y[:, k0:k0 + ROPE_DIM] * cosk_ref[...] + y[:, k0 + ROPE_DIM:k0 + 2 * ROPE_DIM] * sink_ref[...]
    ckv_ref[...] = ckv
    kpe_ref[...] = kpe
    kcat_ref[:, 0:KV_LORA] = ckv.astype(BF16)
    kvt_ref[...] = ckv.T.astype(BF16)
    kcat_ref[:, KV_LORA:KV_LORA + ROPE_DIM] = kpe.astype(BF16)
    cqt = _rms(y[:, :Q_LORA], gq_ref[...]).T.astype(BF16)
    nope_t = _dot(wnt_ref[...], cqt)
    qpe_t = _dot(wat_ref[...], cqt) * cost_ref[...] + _dot(wbt_ref[...], cqt) * sint_ref[...]
    scale = (NOPE_DIM + ROPE_DIM) ** -0.5
    for h in range(heads):
        lat_t = _dot(wkb_ref[h], nope_t[h * NOPE_DIM:(h + 1) * NOPE_DIM].astype(BF16))
        qt_ref[h, 0:KV_LORA, :] = (lat_t * scale).astype(BF16)
        qt_ref[h, KV_LORA:KV_LORA + ROPE_DIM, :] = (qpe_t[h * ROPE_DIM:(h + 1) * ROPE_DIM] * scale).astype(BF16)


def _rot_half(wpe):
    half = wpe.shape[-1] // 2
    return jnp.concatenate([-wpe[..., half:], wpe[..., :half]], axis=-1)


def mla_prep(x, gain, w_in, g_q, g_kv, w_qb, w_kb, pos, tm=512):
    m, d = x.shape
    heads = C_HEADS
    tm = min(tm, m)
    half = ROPE_DIM // 2
    freqs = ROPE_THETA ** (-jnp.arange(half, dtype=F32) / half)
    ang = pos.astype(F32)[:, None] * freqs
    cos_k = jnp.tile(jnp.cos(ang), (1, 2))
    sin_k = jnp.tile(jnp.sin(ang), (1, 2))
    cos_t = jnp.tile(cos_k, (1, heads)).T
    sin_t = jnp.tile(sin_k, (1, heads)).T
    k0 = Q_LORA + KV_LORA
    win = jnp.concatenate([w_in, _rot_half(w_in[:, k0:k0 + ROPE_DIM]),
                           jnp.zeros((d, 512 - k0 - 2 * ROPE_DIM), F32)], axis=1).astype(BF16)
    wq = w_qb.reshape(Q_LORA, heads, NOPE_DIM + ROPE_DIM)
    wnt = wq[:, :, :NOPE_DIM].reshape(Q_LORA, heads * NOPE_DIM).T.astype(BF16)
    wat = wq[:, :, NOPE_DIM:].reshape(Q_LORA, heads * ROPE_DIM).T.astype(BF16)
    wbt = _rot_half(wq[:, :, NOPE_DIM:]).reshape(Q_LORA, heads * ROPE_DIM).T.astype(BF16)
    wkb = w_kb.transpose(1, 0, 2).astype(BF16)
    full = lambda a: pl.BlockSpec(a.shape, lambda i: (0,) * a.ndim)
    rows = lambda width: pl.BlockSpec((tm, width), lambda i: (i, 0))
    cols = lambda height: pl.BlockSpec((height, tm), lambda i: (0, i))
    dcat = KV_LORA + ROPE_DIM
    gq, gkv, gm = g_q.reshape(1, -1), g_kv.reshape(1, -1), gain.reshape(1, -1)
    return pl.pallas_call(
        functools.partial(_mla_prep_body, heads=heads),
        grid=(m // tm,),
        in_specs=[rows(d), full(gm), full(win), full(gq), full(gkv), full(wnt), full(wat), full(wbt), full(wkb),
                  rows(ROPE_DIM), rows(ROPE_DIM), cols(heads * ROPE_DIM), cols(heads * ROPE_DIM)],
        out_specs=[rows(KV_LORA), rows(ROPE_DIM), rows(dcat), cols(KV_LORA),
                   pl.BlockSpec((heads, dcat, tm), lambda i: (0, 0, i))],
        out_shape=[jax.ShapeDtypeStruct((m, KV_LORA), F32), jax.ShapeDtypeStruct((m, ROPE_DIM), F32),
                   jax.ShapeDtypeStruct((m, dcat), BF16), jax.ShapeDtypeStruct((KV_LORA, m), BF16),
                   jax.ShapeDtypeStruct((heads, dcat, m), BF16)],
        compiler_params=_params("parallel"),
        name="mla_prep",
    )(x, gm, win, gq, gkv, wnt, wat, wbt, wkb, cos_k, sin_k, cos_t, sin_t)


def _mla_flash_body(qt_ref, k_ref, kvt_ref, wvbt_ref, o_ref, ml_s, acc_s, ot_s, *, heads, tq):
    i = pl.program_id(1)
    krow = lax.broadcasted_iota(jnp.int32, (tq, tq), 0)
    qcol = lax.broadcasted_iota(jnp.int32, (tq, tq), 1)
    causal = krow <= qcol

    def tile_step(kstart, first):
        k = k_ref[pl.ds(kstart, tq), :]
        vt = kvt_ref[:, pl.ds(kstart, tq)]
        s_all = [_dot(k, qt_ref[h]) for h in range(heads)]
        for h in range(heads):
            m_row, l_row = ml_s.at[2 * h:2 * h + 1, :], ml_s.at[2 * h + 1:2 * h + 2, :]
            if first:
                s = jnp.where(causal, s_all[h], NEG)
                m_new = jnp.max(s, axis=0, keepdims=True)
                p = jnp.exp(s - m_new)
                acc_s[h] = _dot(vt, p.astype(BF16))
                l_row[...] = jnp.sum(p, axis=0, keepdims=True)
            else:
                s = s_all[h]
                m = m_row[...]
                m_new = jnp.maximum(m, jnp.max(s, axis=0, keepdims=True))
                alpha = jnp.exp(m - m_new)
                p = jnp.exp(s - m_new)
                acc_s[h] = alpha * acc_s[h] + _dot(vt, p.astype(BF16))
                l_row[...] = alpha * l_row[...] + jnp.sum(p, axis=0, keepdims=True)
            m_row[...] = m_new

    tile_step(pl.multiple_of(i * tq, tq), True)

    def body(j, carry):
        tile_step(pl.multiple_of(j * tq, tq), False)
        return carry

    lax.fori_loop(0, i, body, 0)
    for h in range(heads):
        o_lat = (acc_s[h] / ml_s[2 * h + 1:2 * h + 2, :]).astype(BF16)
        ot_s[h * V_DIM:(h + 1) * V_DIM, :] = _dot(wvbt_ref[h], o_lat)
    o_ref[...] = ot_s[...].T


def mla_flash_prompt(qt, kcat, kvt, wvbt, batch, tq=256):
    heads, dcat, m = qt.shape
    l = m // batch
    nq = l // tq
    return pl.pallas_call(
        functools.partial(_mla_flash_body, heads=heads, tq=tq),
        grid=(batch, nq),
        in_specs=[pl.BlockSpec((heads, dcat, tq), lambda b, i: (0, 0, b * nq + i)),
                  pl.BlockSpec((l, dcat), lambda b, i: (b, 0)),
                  pl.BlockSpec((KV_LORA, l), lambda b, i: (0, b)),
                  pl.BlockSpec(wvbt.shape, lambda b, i: (0, 0, 0))],
        out_specs=pl.BlockSpec((tq, heads * V_DIM), lambda b, i: (b * nq + i, 0)),
        out_shape=jax.ShapeDtypeStruct((m, heads * V_DIM), F32),
        scratch_shapes=[pltpu.VMEM((2 * heads, tq), F32), pltpu.VMEM((heads, KV_LORA, tq), F32),
                        pltpu.VMEM((heads * V_DIM, tq), F32)],
        compiler_params=_params("parallel", "arbitrary"),
        name="mla_flash_prompt",
    )(qt, kcat, kvt, wvbt)


def _mla_sample_body(pt_ref, q_ref, knew_ref, *rest, n_u, lq):
    kv_refs, pe_refs = rest[:n_u], rest[n_u:2 * n_u]
    o_ref, m_s, l_s, acc_s, kbuf, pbuf = rest[2 * n_u:]
    c = pl.program_id(1)
    q = q_ref[0]

    @pl.when(c == 0)
    def _():
        kn = knew_ref[0]
        s = _nt(q, kn)
        t = lax.broadcasted_iota(jnp.int32, s.shape, 1)
        qpos = lax.broadcasted_iota(jnp.int32, s.shape, 0) % lq
        s = jnp.where(t <= qpos, s, NEG)
        m0 = jnp.max(s, axis=1, keepdims=True)
        p = jnp.exp(s - m0)
        m_s[...] = m0
        l_s[...] = jnp.sum(p, axis=1, keepdims=True)
        acc_s[...] = _dot(p.astype(BF16), kn[:, :KV_LORA])

    for u in range(n_u):
        rows = slice(u * PAGE_SIZE, (u + 1) * PAGE_SIZE)
        kbuf[rows, :] = kv_refs[u][0].astype(BF16)
        pbuf[:, rows] = pe_refs[u][0].astype(BF16)
    s = _nt(q[:, :KV_LORA], kbuf[...]) + _dot(q[:, KV_LORA:], pbuf[...])
    m_prev = m_s[...]
    m_new = jnp.maximum(m_prev, jnp.max(s, axis=1, keepdims=True))
    alpha = jnp.exp(m_prev - m_new)
    p = jnp.exp(s - m_new)
    l_s[...] = alpha * l_s[...] + jnp.sum(p, axis=1, keepdims=True)
    acc_s[...] = alpha * acc_s[...] + _dot(p.astype(BF16), kbuf[...])
    m_s[...] = m_new

    @pl.when(c == pl.num_programs(1) - 1)
    def _():
        o_ref[0] = acc_s[...] / l_s[...]


def mla_sample(qt, kcat_new, cache_kv, cache_pe, layer, page_table, lq):
    heads, dcat, m = qt.shape
    b = m // lq
    n_pages = page_table.shape[1]
    n_u = PAGES_PER_STEP
    r_n = heads * lq
    q = qt.reshape(heads, dcat, b, lq).transpose(2, 0, 3, 1).reshape(b, r_n, dcat)
    knew = jnp.pad(kcat_new.reshape(b, lq, dcat), ((0, 0), (0, PAGE_SIZE - lq), (0, 0)))
    kv_specs = [pl.BlockSpec((None, 1, PAGE_SIZE, KV_LORA), lambda bi, c, pt, u=u: (layer, pt[bi, c * n_u + u], 0, 0))
                for u in range(n_u)]
    pe_t = cache_pe.transpose(0, 1, 3, 2)
    pe_specs = [pl.BlockSpec((None, 1, ROPE_DIM, PAGE_SIZE), lambda bi, c, pt, u=u: (layer, pt[bi, c * n_u + u], 0, 0))
                for u in range(n_u)]
    o = pl.pallas_call(
        functools.partial(_mla_sample_body, n_u=n_u, lq=lq),
        grid_spec=pltpu.PrefetchScalarGridSpec(
            num_scalar_prefetch=1, grid=(b, n_pages // n_u),
            in_specs=[pl.BlockSpec((1, r_n, dcat), lambda bi, c, pt: (bi, 0, 0)),
                      pl.BlockSpec((1, PAGE_SIZE, dcat), lambda bi, c, pt: (bi, 0, 0))] + kv_specs + pe_specs,
            out_specs=pl.BlockSpec((1, r_n, KV_LORA), lambda bi, c, pt: (bi, 0, 0)),
            scratch_shapes=[pltpu.VMEM((r_n, 1), F32), pltpu.VMEM((r_n, 1), F32), pltpu.VMEM((r_n, KV_LORA), F32),
                            pltpu.VMEM((n_u * PAGE_SIZE, KV_LORA), BF16), pltpu.VMEM((ROPE_DIM, n_u * PAGE_SIZE), BF16)]),
        out_shape=jax.ShapeDtypeStruct((b, r_n, KV_LORA), F32),
        compiler_params=_params("parallel", "arbitrary"),
        name="mla_sample",
    )(page_table, q, knew, *([cache_kv] * n_u), *([pe_t] * n_u))
    return o.reshape(b, heads, lq, KV_LORA).transpose(1, 0, 2, 3).reshape(heads, m, KV_LORA)


def _headproj_body(x_ref, w_ref, o_ref, *, heads):
    for h in range(heads):
        o_ref[:, h * V_DIM:(h + 1) * V_DIM] = _dot(x_ref[h].astype(BF16), w_ref[h])


def head_value_proj(o_lat, wvb):
    heads, m, _ = o_lat.shape
    return pl.pallas_call(
        functools.partial(_headproj_body, heads=heads),
        out_shape=jax.ShapeDtypeStruct((m, heads * V_DIM), F32),
        compiler_params=pltpu.CompilerParams(vmem_limit_bytes=VMEM_LIMIT),
        name="mla_value_proj",
    )(o_lat, wvb)


def _cross_body(q_ref, k_ref, v_ref, o_ref, *, heads, dh, interleaved):
    q = q_ref[0]
    scale = dh ** -0.5
    for h in range(heads):
        sl = slice(h * dh, (h + 1) * dh)
        if interleaved:
            rows = pl.ds(h, k_ref.shape[1] // heads, stride=heads)
            kh, vh = k_ref[0, rows, :], v_ref[0, rows, :]
        else:
            kh, vh = k_ref[0, :, sl], v_ref[0, :, sl]
        s = _nt((q[:, sl] * scale).astype(BF16), kh.astype(BF16))
        m = jnp.max(s, axis=1, keepdims=True)
        p = jnp.exp(s - m)
        l = jnp.sum(p, axis=1, keepdims=True)
        o_ref[0, :, sl] = _dot(p.astype(BF16), vh.astype(BF16)) / l


def cross_core(q, mk, mv, tq=512, layer=None):
    b, lq, w = q.shape
    tq = min(tq, lq)
    if layer is None:
        kv_spec = pl.BlockSpec((1,) + mk.shape[1:], lambda bi, i: (bi, 0, 0))
    else:
        kv_spec = pl.BlockSpec((None, 1) + mk.shape[2:], lambda bi, i: (layer, bi, 0, 0))
    return pl.pallas_call(
        functools.partial(_cross_body, heads=X_HEADS, dh=X_HEAD_DIM, interleaved=layer is not None),
        grid=(b, lq // tq),
        in_specs=[pl.BlockSpec((1, tq, w), lambda bi, i: (bi, i, 0)), kv_spec, kv_spec],
        out_specs=pl.BlockSpec((1, tq, w), lambda bi, i: (bi, i, 0)),
        out_shape=jax.ShapeDtypeStruct((b, lq, w), F32),
        compiler_params=_params("parallel", "parallel"),
        name="cross_core",
    )(q, mk, mv)


def _moe_gates(logits):
    lane = lax.broadcasted_iota(jnp.int32, logits.shape, 1).astype(F32)
    big = 1e9
    is_g = lane < N_GROUPS
    gl = jnp.where(is_g, logits, -jnp.inf)
    gmax = jnp.max(gl, axis=1, keepdims=True)
    grp = jnp.min(jnp.where(is_g & (gl == gmax), lane, big), axis=1, keepdims=True)
    p_grp = 1.0 / jnp.sum(jnp.where(is_g, jnp.exp(gl - gmax), 0.0), axis=1, keepdims=True)
    e_idx = lane - N_GROUPS
    in_grp = (e_idx >= grp * EXPERTS_PER_GROUP) & (e_idx < (grp + 1) * EXPERTS_PER_GROUP)
    el = jnp.where(in_grp, logits, -jnp.inf)
    t1 = jnp.max(el, axis=1, keepdims=True)
    i1 = jnp.min(jnp.where(in_grp & (el == t1), lane, big), axis=1, keepdims=True)
    el2 = jnp.where(lane == i1, -jnp.inf, el)
    t2 = jnp.max(el2, axis=1, keepdims=True)
    i2 = jnp.min(jnp.where(in_grp & (lane != i1) & (el2 == t2), lane, big), axis=1, keepdims=True)
    e2 = jnp.exp(t2 - t1)
    w1 = 1.0 / (1.0 + e2)
    w2 = e2 / (1.0 + e2)
    return p_grp * (jnp.where(lane == i1, w1, 0.0) + jnp.where(lane == i2, w2, 0.0))


def _moe_body(x_ref, g_ref, wr_ref, br_ref, wg_ref, wu_ref, wd_ref, o_ref, xn_s, gate_s, acc_s):
    e = pl.program_id(1)

    @pl.when(e == 0)
    def _():
        x = x_ref[...]
        xn = _rms(x, g_ref[...])
        xn_s[...] = xn.astype(BF16)
        gate_s[...] = _moe_gates(_dot(xn.astype(BF16), wr_ref[...]) + br_ref[...])
        acc_s[...] = x

    xn = xn_s[...]
    lane = lax.broadcasted_iota(jnp.int32, gate_s.shape, 1)
    ge = jnp.sum(jnp.where(lane == e + N_GROUPS, gate_s[...], 0.0), axis=1, keepdims=True)
    a = _dot(xn, wg_ref[0])
    u = _dot(xn, wu_ref[0])
    hid = (a * _sigmoid(a)) * u * ge
    acc_s[...] += _dot(hid.astype(BF16), wd_ref[0])

    @pl.when(e == pl.num_programs(1) - 1)
    def _():
        o_ref[...] = acc_s[...]


def moe_layer(x, gain, w_group, b_group, w_router, b_router, w_up, w_gate, w_down, layer, tm=1024):
    m, d = x.shape
    tm = min(tm, m)
    _, n_e, _, f = w_up.shape
    wr = jnp.concatenate([w_group, w_router, jnp.zeros((d, LANES - N_GROUPS - N_EXPERTS), F32)], axis=1).astype(BF16)
    br = jnp.concatenate([b_group, b_router, jnp.zeros((LANES - N_GROUPS - N_EXPERTS,), F32)]).reshape(1, LANES)
    return pl.pallas_call(
        _moe_body,
        grid=(m // tm, n_e),
        in_specs=[pl.BlockSpec((tm, d), lambda i, e: (i, 0)),
                  pl.BlockSpec((1, d), lambda i, e: (0, 0)),
                  pl.BlockSpec((d, LANES), lambda i, e: (0, 0)),
                  pl.BlockSpec((1, LANES), lambda i, e: (0, 0)),
                  pl.BlockSpec((None, 1, d, f), lambda i, e: (layer, e, 0, 0)),
                  pl.BlockSpec((None, 1, d, f), lambda i, e: (layer, e, 0, 0)),
                  pl.BlockSpec((None, 1, f, d), lambda i, e: (layer, e, 0, 0))],
        out_specs=pl.BlockSpec((tm, d), lambda i, e: (i, 0)),
        out_shape=jax.ShapeDtypeStruct((m, d), F32),
        scratch_shapes=[pltpu.VMEM((tm, d), BF16), pltpu.VMEM((tm, LANES), F32), pltpu.VMEM((tm, d), F32)],
        compiler_params=_params("parallel", "arbitrary"),
        name="moe",
    )(x, gain.reshape(1, d), wr, br, w_gate, w_up, w_down)


AB_SPLITS = [(0, A_WIDTH), (A_WIDTH, A_WIDTH), (2 * A_WIDTH, A_WIDTH), (3 * A_WIDTH, 2 * B_WIDTH),
             (3 * A_WIDTH + 2 * B_WIDTH, B_WIDTH), (3 * A_WIDTH + 3 * B_WIDTH, B_WIDTH),
             (3 * A_WIDTH + 4 * B_WIDTH, LANES)]


def _even_projection(x, gain, w_in_bf):
    return fused_linear([x], [w_in_bf], splits=AB_SPLITS, gain=gain, name="in_proj_ab")


def _pad_seq(a, lp):
    return jnp.pad(a, ((0, 0), (0, lp - a.shape[1]), (0, 0)))


def kernel(x_prompt, x_sample, mem_prompt, cache_moba_k, cache_moba_v, cache_mla_kv, cache_mla_pe, state_mlstm_c, state_mlstm_n, state_mlstm_m, state_conv, cache_mem_k, cache_mem_v, page_table, norm_mix, norm_cross, norm_mem, norm_ffn, norm_final, w_in_ab, b_gates, conv_w, norm_mlstm, w_out_ab, w_in_c, norm_q_lat, norm_kv_lat, w_qb, w_kb, w_vb, w_out_c, w_cq, w_ck, w_cv, w_co, w_group, b_group, w_router, b_router, w_up, w_gate, w_down):
    bp, lp, d = x_prompt.shape
    bs, ls, _ = x_sample.shape
    depth = norm_mix.shape[0]
    mp, ms = bp * lp, bs * ls
    n_mem = mem_prompt.shape[1]
    past = page_table.shape[1] * PAGE_SIZE
    pos_p = jnp.tile(jnp.arange(lp, dtype=jnp.int32), bp)
    pos_s = jnp.tile(past + jnp.arange(ls, dtype=jnp.int32), bs)

    xp = x_prompt.reshape(mp, d)
    xs = x_sample.reshape(ms, d)
    mem = mem_prompt.reshape(bp * n_mem, d)
    ak_p, av_p, kv_p, pe_p, mc_p, mn_p, mm_p, cv_p, mk_p, mv_p = [], [], [], [], [], [], [], [], [], []
    ak_s, av_s, kv_s, pe_s, mc_s, mn_s, mm_s, cv_s = [], [], [], [], [], [], [], []
    back = CONV_W - 1
    ch = MLSTM_CHUNK
    wu_bf, wg_bf, wd_bf = w_up.astype(BF16), w_gate.astype(BF16), w_down.astype(BF16)
    mem_k_rows = cache_mem_k.reshape(depth, bs, n_mem * X_HEADS, X_HEAD_DIM)
    mem_v_rows = cache_mem_v.reshape(depth, bs, n_mem * X_HEADS, X_HEAD_DIM)
    ls_pad = -(-ls // ch) * ch
    for l in range(depth):
        j = l // 2
        if l % 2 == 0:
            n_ab = w_in_ab.shape[2]
            n_pad = AB_SPLITS[-1][0] + LANES
            w_in_bf = jnp.pad(w_in_ab[j], ((0, 0), (0, n_pad - n_ab))).astype(BF16)
            w_out_bf = w_out_ab[j].astype(BF16)
            qa, ka, va, qkb, vb, ob, gt = _even_projection(xp, norm_mix[l], w_in_bf)
            o_a = moba_prompt(qa.reshape(bp, lp, A_WIDTH), ka.reshape(bp, lp, A_WIDTH), va.reshape(bp, lp, A_WIDTH))
            qkb3 = qkb.reshape(bp, lp, 2 * B_WIDTH)
            hb, mc, mn, mm = mlstm_layer(
                qkb3, vb.reshape(bp, lp, B_WIDTH), ob.reshape(bp, lp, B_WIDTH), gt.reshape(bp, lp, LANES),
                b_gates[j], conv_w[j], norm_mlstm[j], jnp.zeros((bp, back, 2 * B_WIDTH), F32),
                jnp.zeros((bp, B_HEADS, B_HEAD_DIM, B_HEAD_DIM), F32), jnp.zeros((bp, B_HEADS, B_HEAD_DIM), F32),
                jnp.zeros((bp, B_HEADS), F32), valid=lp)
            (xp,) = fused_linear([o_a.reshape(mp, A_WIDTH), hb.reshape(mp, B_WIDTH)],
                                 [w_out_bf[:A_WIDTH], w_out_bf[A_WIDTH:]], residual=xp, name="out_proj_ab")
            ak_p.append(ka.reshape(bp, lp, A_HEADS, A_HEAD_DIM)); av_p.append(va.reshape(bp, lp, A_HEADS, A_HEAD_DIM))
            cv_p.append(qkb3[:, lp - back:, :]); mc_p.append(mc); mn_p.append(mn); mm_p.append(mm)
            qa, ka, va, qkb, vb, ob, gt = _even_projection(xs, norm_mix[l], w_in_bf)
            o_a = moba_sample(qa.reshape(bs, ls, A_WIDTH), ka.reshape(bs, ls, A_WIDTH), va.reshape(bs, ls, A_WIDTH),
                              cache_moba_k, cache_moba_v, j, page_table)
            qkb3 = qkb.reshape(bs, ls, 2 * B_WIDTH)
            hb, mc, mn, mm = mlstm_layer(
                _pad_seq(qkb3, ls_pad), _pad_seq(vb.reshape(bs, ls, B_WIDTH), ls_pad),
                _pad_seq(ob.reshape(bs, ls, B_WIDTH), ls_pad), _pad_seq(gt.reshape(bs, ls, LANES), ls_pad),
                b_gates[j], conv_w[j], norm_mlstm[j], state_conv[j],
                state_mlstm_c[j], state_mlstm_n[j], state_mlstm_m[j], valid=ls)
            (xs,) = fused_linear([o_a.reshape(ms, A_WIDTH), hb[:, :ls].reshape(ms, B_WIDTH)],
                                 [w_out_bf[:A_WIDTH], w_out_bf[A_WIDTH:]], residual=xs, name="out_proj_ab")
            ak_s.append(ka.reshape(bs, ls, A_HEADS, A_HEAD_DIM)); av_s.append(va.reshape(bs, ls, A_HEADS, A_HEAD_DIM))
            conv_all = jnp.concatenate([state_conv[j], qkb3], axis=1)
            cv_s.append(conv_all[:, ls:, :]); mc_s.append(mc); mn_s.append(mn); mm_s.append(mm)
        else:
            wvb = w_vb[j].transpose(1, 0, 2).astype(BF16)
            w_out_bf = w_out_c[j].astype(BF16)
            ckv, kpe, kcat, kvt, qt = mla_prep(xp, norm_mix[l], w_in_c[j], norm_q_lat[j], norm_kv_lat[j], w_qb[j], w_kb[j], pos_p)
            o = mla_flash_prompt(qt, kcat, kvt, w_vb[j].transpose(1, 2, 0).astype(BF16), bp)
            (xp,) = fused_linear([o], [w_out_bf], residual=xp, name="out_proj_c")
            kv_p.append(ckv.reshape(bp, lp, KV_LORA)); pe_p.append(kpe.reshape(bp, lp, ROPE_DIM))
            ckv, kpe, kcat, _, qt = mla_prep(xs, norm_mix[l], w_in_c[j], norm_q_lat[j], norm_kv_lat[j], w_qb[j], w_kb[j], pos_s)
            o_lat = mla_sample(qt, kcat, cache_mla_kv, cache_mla_pe, j, page_table, ls)
            o = head_value_proj(o_lat, wvb)
            (xs,) = fused_linear([o], [w_out_bf], residual=xs, name="out_proj_c")
            kv_s.append(ckv.reshape(bs, ls, KV_LORA)); pe_s.append(kpe.reshape(bs, ls, ROPE_DIM))
        w_ckv = jnp.concatenate([w_ck[l], w_cv[l]], axis=1).astype(BF16)
        mk, mv = fused_linear([mem], [w_ckv], splits=[(0, X_WIDTH), (X_WIDTH, X_WIDTH)], gain=norm_mem[l], name="mem_kv")
        mk_p.append(mk.reshape(bp, n_mem, X_HEADS, X_HEAD_DIM)); mv_p.append(mv.reshape(bp, n_mem, X_HEADS, X_HEAD_DIM))
        w_cq_bf, w_co_bf = w_cq[l].astype(BF16), w_co[l].astype(BF16)
        (q,) = fused_linear([xp], [w_cq_bf], gain=norm_cross[l], name="cross_q")
        o = cross_core(q.reshape(bp, lp, X_WIDTH), mk.reshape(bp, n_mem, X_WIDTH), mv.reshape(bp, n_mem, X_WIDTH))
        (xp,) = fused_linear([o.reshape(mp, X_WIDTH)], [w_co_bf], residual=xp, name="cross_out")
        (q,) = fused_linear([xs], [w_cq_bf], gain=norm_cross[l], name="cross_q")
        q8 = _pad_seq(q.reshape(bs, ls, X_WIDTH), 8)
        o = cross_core(q8, mem_k_rows, mem_v_rows, layer=l)
        (xs,) = fused_linear([o[:, :ls].reshape(ms, X_WIDTH)], [w_co_bf], residual=xs, name="cross_out")
        xp = moe_layer(xp, norm_ffn[l], w_group[l], b_group[l], w_router[l], b_router[l], wu_bf, wg_bf, wd_bf, l)
        xs = moe_layer(xs, norm_ffn[l], w_group[l], b_group[l], w_router[l], b_router[l], wu_bf, wg_bf, wd_bf, l)
    y_prompt = rmsnorm_rows(xp, norm_final).reshape(bp, lp, d)
    y_sample = rmsnorm_rows(xs, norm_final).reshape(bs, ls, d)
    return (y_prompt, y_sample,
            jnp.stack(ak_p), jnp.stack(av_p), jnp.stack(kv_p), jnp.stack(pe_p),
            jnp.stack(mc_p), jnp.stack(mn_p), jnp.stack(mm_p), jnp.stack(cv_p), jnp.stack(mk_p), jnp.stack(mv_p),
            jnp.stack(ak_s), jnp.stack(av_s), jnp.stack(kv_s), jnp.stack(pe_s),
            jnp.stack(mc_s), jnp.stack(mn_s), jnp.stack(mm_s), jnp.stack(cv_s))
```

```python
import functools
import math

import jax
import jax.numpy as jnp
from jax import lax
from jax.experimental import pallas as pl
from jax.experimental.pallas import tpu as pltpu

F32 = jnp.float32
BF16 = jnp.bfloat16
HI = lax.Precision.HIGHEST
EPS = 1e-6
NEG = -1e30
VMEM_LIMIT = 56 * 1024 * 1024
LANES = 128

PAGE_SIZE = 128
A_HEADS, A_HEAD_DIM = 8, 64
A_WIDTH = A_HEADS * A_HEAD_DIM
MOBA_BLOCK, MOBA_TOPK = 256, 3
B_HEADS, B_HEAD_DIM = 4, 128
B_WIDTH = B_HEADS * B_HEAD_DIM
CONV_W = 4
MLSTM_CHUNK = 128
C_HEADS, Q_LORA, KV_LORA, NOPE_DIM, ROPE_DIM, V_DIM = 16, 256, 128, 64, 32, 64
ROPE_THETA = 10000.0
X_HEADS, X_HEAD_DIM = 4, 128
X_WIDTH = X_HEADS * X_HEAD_DIM
N_GROUPS, EXPERTS_PER_GROUP = 4, 4
N_EXPERTS = N_GROUPS * EXPERTS_PER_GROUP
PAGES_PER_STEP = 32
MOBA_PAGES_PER_STEP = 16

NT_DIMS = (((1,), (1,)), ((), ()))
TN_DIMS = (((0,), (0,)), ((), ()))


def _params(*sem):
    return pltpu.CompilerParams(dimension_semantics=sem, vmem_limit_bytes=VMEM_LIMIT)


def _nt(a, b, precision=None):
    return lax.dot_general(a, b, NT_DIMS, precision=precision, preferred_element_type=F32)


def _dot(a, b, precision=None):
    return jnp.dot(a, b, precision=precision, preferred_element_type=F32)


def _rms(x, g):
    return x * lax.rsqrt(jnp.mean(x * x, axis=-1, keepdims=True) + EPS) * g


def _sigmoid(x):
    return 1.0 / (1.0 + jnp.exp(-x))


def _linear_body(*refs, n_in, has_gain, has_res, splits):
    x_refs, w_refs = refs[:n_in], refs[n_in:2 * n_in]
    p = 2 * n_in
    g_ref = refs[p] if has_gain else None
    p += int(has_gain)
    r_ref = refs[p] if has_res else None
    p += int(has_res)
    o_refs = refs[p:]
    xs = []
    for xr in x_refs:
        x = xr[...]
        if has_gain:
            x = _rms(x, g_ref[...])
        xs.append(x.astype(BF16))
    for (off, width), o_ref in zip(splits, o_refs):
        acc = None
        for x, wr in zip(xs, w_refs):
            y = _dot(x, wr[:, off:off + width])
            acc = y if acc is None else acc + y
        if has_res:
            acc = acc + r_ref[...]
        o_ref[...] = acc.astype(o_ref.dtype)


def fused_linear(xs, ws, splits=None, gain=None, residual=None, tm=512, name="linear"):
    m, n = xs[0].shape[0], ws[0].shape[1]
    splits = splits or [(0, n)]
    tm = min(tm, m)
    assert m % tm == 0
    in_specs = [pl.BlockSpec((tm, x.shape[1]), lambda i: (i, 0)) for x in xs]
    in_specs += [pl.BlockSpec(w.shape, lambda i: (0, 0)) for w in ws]
    args = list(xs) + list(ws)
    if gain is not None:
        in_specs.append(pl.BlockSpec((1, gain.shape[-1]), lambda i: (0, 0)))
        args.append(gain.reshape(1, -1))
    if residual is not None:
        assert len(splits) == 1
        in_specs.append(pl.BlockSpec((tm, n), lambda i: (i, 0)))
        args.append(residual)
    outs = pl.pallas_call(
        functools.partial(_linear_body, n_in=len(xs), has_gain=gain is not None, has_res=residual is not None,
                          splits=tuple(splits)),
        grid=(m // tm,),
        in_specs=in_specs,
        out_specs=[pl.BlockSpec((tm, w), lambda i: (i, 0)) for _, w in splits],
        out_shape=[jax.ShapeDtypeStruct((m, w), F32) for _, w in splits],
        compiler_params=_params("parallel"),
        name=name,
    )(*args)
    return outs


def _rmsnorm_body(x_ref, g_ref, o_ref):
    o_ref[...] = _rms(x_ref[...], g_ref[...])


def rmsnorm_rows(x, g, tm=1024):
    m, d = x.shape
    tm = min(tm, m)
    return pl.pallas_call(
        _rmsnorm_body,
        grid=(m // tm,),
        in_specs=[pl.BlockSpec((tm, d), lambda i: (i, 0)), pl.BlockSpec((1, d), lambda i: (0, 0))],
        out_specs=pl.BlockSpec((tm, d), lambda i: (i, 0)),
        out_shape=jax.ShapeDtypeStruct((m, d), F32),
        compiler_params=_params("parallel"),
        name="final_norm",
    )(x, g.reshape(1, d))


def _topk_mask(g, valid, n_iota, nb, topk):
    gm = jnp.where(valid, g, -jnp.inf)
    rank = jnp.zeros(g.shape, jnp.int32)
    for m in range(nb):
        gc = gm[:, m:m + 1]
        beats = (gc > gm) | ((gc == gm) & (m < n_iota))
        rank = rank + beats.astype(jnp.int32)
    return ((rank < topk) & valid).astype(F32)


def _topk_rows(g, valid, n_iota, nb, topk):
    gm = jnp.where(valid, g, -jnp.inf)
    rank = jnp.zeros(g.shape, jnp.int32)
    for m in range(nb):
        gr = gm[m:m + 1, :]
        beats = (gr > gm) | ((gr == gm) & (m < n_iota))
        rank = rank + beats.astype(jnp.int32)
    return ((rank < topk) & valid).astype(F32)


def _moba_prompt_body(q_ref, k_ref, v_ref, o_ref, kmean_s, kh_s, vt_s, qt_s, sel_s, ml_s, acc_s, ot_s, *,
                      nb, blk, heads, dh, topk):
    i = pl.program_id(1)

    @pl.when(i == 0)
    def _():
        for n in range(nb):
            rows = slice(n * blk, (n + 1) * blk)
            kmean_s[n:n + 1, :] = jnp.mean(k_ref[0, rows, :], axis=0, keepdims=True)
            vt_s[:, rows] = v_ref[0, rows, :].T.astype(BF16)
        for h in range(heads):
            kh_s[h] = k_ref[0, :, h * dh:(h + 1) * dh].astype(BF16)

    qt = q_ref[0].T
    krow = lax.broadcasted_iota(jnp.int32, (blk, blk), 0)
    qcol = lax.broadcasted_iota(jnp.int32, (blk, blk), 1)
    dmat = (qcol - krow).astype(F32)
    n_iota = lax.broadcasted_iota(jnp.int32, (nb, blk), 0)
    slopes = [2.0 ** (-8.0 * (h + 1) / heads) for h in range(heads)]
    head_rows = [slice(h * dh, (h + 1) * dh) for h in range(heads)]
    for h in range(heads):
        gate_t = _dot(kmean_s[:, head_rows[h]].astype(BF16), qt[head_rows[h]].astype(BF16))
        sel_s[h] = _topk_rows(gate_t, n_iota < i, n_iota, nb, topk)
    qt_s[...] = (qt * dh ** -0.5).astype(BF16)

    def block_step(kstart, logits_fn, first):
        s_all = [_dot(kh_s[h, pl.ds(kstart, blk), :], qt_s[head_rows[h], :]) for h in range(heads)]
        for h in range(heads):
            logits = logits_fn(h, s_all[h])
            vt = vt_s[head_rows[h], pl.ds(kstart, blk)]
            m_row, l_row = ml_s.at[2 * h:2 * h + 1, :], ml_s.at[2 * h + 1:2 * h + 2, :]
            if first:
                m_new = jnp.max(logits, axis=0, keepdims=True)
                p = jnp.exp(logits - m_new)
                acc_s[h] = _dot(vt, p.astype(BF16))
                l_row[...] = jnp.sum(p, axis=0, keepdims=True)
            else:
                m = m_row[...]
                m_new = jnp.maximum(m, jnp.max(logits, axis=0, keepdims=True))
                alpha = jnp.exp(m - m_new)
                p = jnp.exp(logits - m_new)
                acc_s[h] = alpha * acc_s[h] + _dot(vt, p.astype(BF16))
                l_row[...] = alpha * l_row[...] + jnp.sum(p, axis=0, keepdims=True)
            m_row[...] = m_new

    block_step(pl.multiple_of(i * blk, blk), lambda h, s: jnp.where(dmat >= 0, s - slopes[h] * dmat, NEG), True)

    def body(j, carry):
        dist = dmat + ((i - j) * blk).astype(F32)
        pick = n_iota == j

        def logits_fn(h, s):
            selrow = jnp.sum(jnp.where(pick, sel_s[h], 0.0), axis=0, keepdims=True)
            return jnp.where(selrow > 0.5, s - slopes[h] * dist, NEG)

        block_step(pl.multiple_of(j * blk, blk), logits_fn, False)
        return carry

    lax.fori_loop(0, i, body, 0)
    for h in range(heads):
        ot_s[head_rows[h], :] = acc_s[h] / ml_s[2 * h + 1:2 * h + 2, :]
    o_ref[0] = ot_s[...].T


def moba_prompt(q, k, v):
    b, l, w = q.shape
    blk = MOBA_BLOCK
    nb = l // blk
    heads, dh = A_HEADS, A_HEAD_DIM
    return pl.pallas_call(
        functools.partial(_moba_prompt_body, nb=nb, blk=blk, heads=heads, dh=dh, topk=MOBA_TOPK),
        grid=(b, nb),
        in_specs=[pl.BlockSpec((1, blk, w), lambda bi, i: (bi, i, 0)),
                  pl.BlockSpec((1, l, w), lambda bi, i: (bi, 0, 0)),
                  pl.BlockSpec((1, l, w), lambda bi, i: (bi, 0, 0))],
        out_specs=pl.BlockSpec((1, blk, w), lambda bi, i: (bi, i, 0)),
        out_shape=jax.ShapeDtypeStruct((b, l, w), F32),
        scratch_shapes=[pltpu.VMEM((nb, w), F32), pltpu.VMEM((heads, l, dh), BF16), pltpu.VMEM((w, l), BF16),
                        pltpu.VMEM((w, blk), BF16), pltpu.VMEM((heads, nb, blk), F32),
                        pltpu.VMEM((2 * heads, blk), F32), pltpu.VMEM((heads, dh, blk), F32),
                        pltpu.VMEM((w, blk), F32)],
        compiler_params=_params("parallel", "arbitrary"),
        name="moba_prompt",
    )(q, k, v)


def _moba_scores_body(pt_ref, q_ref, *rest, n_u, heads):
    k_refs, (s_ref, ksum_ref) = rest[:n_u], rest[n_u:]
    c = pl.program_id(1)

    @pl.when(c == 0)
    def _():
        ksum_ref[...] = jnp.zeros(ksum_ref.shape, F32)

    ppb = MOBA_BLOCK // PAGE_SIZE
    dh = q_ref.shape[-1]
    lane = lax.broadcasted_iota(jnp.int32, (dh, LANES), 1)
    for h in range(heads):
        qh = q_ref[0, h].astype(BF16)
        ks = ksum_ref[0, h]
        for n in range(n_u // ppb):
            bsum = None
            for u in range(n * ppb, (n + 1) * ppb):
                kt = k_refs[u][0, h]
                s_ref[0, h, :, u * PAGE_SIZE:(u + 1) * PAGE_SIZE] = _dot(qh, kt.astype(BF16))
                bsum = kt if bsum is None else bsum + kt
            ks = jnp.where(lane == c * (n_u // ppb) + n, jnp.sum(bsum, axis=1, keepdims=True), ks)
        ksum_ref[0, h] = ks


def _moba_select_body(ksum_ref, q_ref, idx_ref, *, nb, topk, heads):
    lane = lax.broadcasted_iota(jnp.int32, (8, LANES), 1)
    valid = lane < nb
    lane_f = lane.astype(F32)
    for h in range(heads):
        kmean_t = (ksum_ref[0, h] * (1.0 / MOBA_BLOCK)).astype(BF16)
        gate = jnp.where(valid, _dot(q_ref[0, h].astype(BF16), kmean_t), -jnp.inf)
        rank = jnp.zeros(gate.shape, jnp.int32)
        for m in range(nb):
            gc = gate[:, m:m + 1]
            rank = rank + ((gc > gate) | ((gc == gate) & (m < lane))).astype(jnp.int32)
        out = jnp.zeros((8, LANES), F32)
        for k in range(topk):
            idx_k = jnp.sum(jnp.where((rank == k) & valid, lane_f, 0.0), axis=1, keepdims=True)
            out = out + jnp.where(lane == k, idx_k, 0.0)
        idx_ref[0, h] = out.astype(jnp.int32)


def _moba_gather_body(pt_ref, sel_ref, s_ref, q_ref, knew_ref, vnew_ref, v_hbm, o_ref, vbuf, sem, *,
                      layer, lq, topk, heads, dh, past):
    b, h = pl.program_id(0), pl.program_id(1)
    n_h = pl.num_programs(1)
    step = b * n_h + h
    n_steps = pl.num_programs(0) * n_h
    slot = step % 2
    ppb = MOBA_BLOCK // PAGE_SIZE
    n_sel = lq * topk
    t_all = n_sel * MOBA_BLOCK

    def copies(bb, hh, sl):
        out = []
        for j in range(n_sel):
            blk = sel_ref[bb, hh * n_sel + j]
            for pg in range(ppb):
                page = pt_ref[bb, blk * ppb + pg]
                cols = pl.ds((j * ppb + pg) * PAGE_SIZE, PAGE_SIZE)
                out.append(pltpu.make_async_copy(v_hbm.at[layer, page, hh], vbuf.at[sl, :, cols], sem.at[sl]))
        return out

    @pl.when(step == 0)
    def _():
        for c in copies(b, h, slot):
            c.start()

    @pl.when(step + 1 < n_steps)
    def _():
        nxt = step + 1
        for c in copies(nxt // n_h, nxt % n_h, 1 - slot):
            c.start()

    scale = dh ** -0.5
    slope = jnp.exp2((-8.0 / heads) * (h + 1).astype(F32))
    row = lax.broadcasted_iota(jnp.int32, (8, t_all), 0)
    col = lax.broadcasted_iota(jnp.int32, (8, t_all), 1)
    choice = col // MOBA_BLOCK
    blocks = [sel_ref[b, h * n_sel + j] for j in range(n_sel)]
    s = jnp.concatenate([s_ref[0, 0, :, pl.ds(pl.multiple_of(blk * MOBA_BLOCK, MOBA_BLOCK), MOBA_BLOCK)]
                         for blk in blocks], axis=1)
    kpos = jnp.zeros((8, t_all), jnp.int32)
    for j in range(n_sel):
        kpos = jnp.where(choice == j, blocks[j] * MOBA_BLOCK, kpos)
    kpos = kpos + (col - choice * MOBA_BLOCK)
    dist = ((past + row) - kpos).astype(F32)
    logits = jnp.where((choice // topk) == row, s * scale - slope * dist, NEG)
    row1 = lax.broadcasted_iota(jnp.int32, (8, 1), 0)
    q8, knew, vnew = q_ref[0, 0], knew_ref[0, 0], vnew_ref[0, 0]
    own = []
    for t in range(lq):
        so = jnp.sum(q8 * knew[t:t + 1, :], axis=1, keepdims=True) * scale - slope * (row1 - t).astype(F32)
        own.append(jnp.where(row1 >= t, so, NEG))
    m = jnp.max(logits, axis=1, keepdims=True)
    for so in own:
        m = jnp.maximum(m, so)
    p = jnp.exp(logits - m)
    l = jnp.sum(p, axis=1, keepdims=True)

    for c in copies(b, h, slot):
        c.wait()
    acc = _nt(p.astype(BF16), vbuf[slot].astype(BF16))
    for t, so in enumerate(own):
        po = jnp.exp(so - m)
        l = l + po
        acc = acc + po * vnew[t:t + 1, :]
    o_ref[0, 0] = acc / l


def moba_sample(q, k_new, v_new, cache_k, cache_v, layer, page_table):
    b, lq, w = q.shape
    heads, dh = A_HEADS, A_HEAD_DIM
    n_pages = page_table.shape[1]
    past = n_pages * PAGE_SIZE
    nb = past // MOBA_BLOCK
    assert past % MOBA_BLOCK == 0 and MOBA_TOPK <= nb <= LANES and lq <= 8
    n_u = MOBA_PAGES_PER_STEP
    ppb = MOBA_BLOCK // PAGE_SIZE
    ck_t = cache_k.transpose(0, 1, 3, 4, 2)
    cv_t = cache_v.transpose(0, 1, 3, 4, 2)

    def per_head(a):
        a = a.reshape(b, lq, heads, dh).transpose(0, 2, 1, 3)
        return jnp.pad(a, ((0, 0), (0, 0), (0, 8 - lq), (0, 0)))

    qh = per_head(q)
    page_specs = [pl.BlockSpec((None, 1, heads, dh, PAGE_SIZE),
                               lambda bi, c, pt, u=u: (layer, pt[bi, c * n_u + u], 0, 0, 0)) for u in range(n_u)]
    scores, ksum = pl.pallas_call(
        functools.partial(_moba_scores_body, n_u=n_u, heads=heads),
        grid_spec=pltpu.PrefetchScalarGridSpec(
            num_scalar_prefetch=1, grid=(b, n_pages // n_u),
            in_specs=[pl.BlockSpec((1, heads, 8, dh), lambda bi, c, pt: (bi, 0, 0, 0))] + page_specs,
            out_specs=[pl.BlockSpec((1, heads, 8, n_u * PAGE_SIZE), lambda bi, c, pt: (bi, 0, 0, c)),
                       pl.BlockSpec((1, heads, dh, LANES), lambda bi, c, pt: (bi, 0, 0, 0))]),
        out_shape=[jax.ShapeDtypeStruct((b, heads, 8, past), F32), jax.ShapeDtypeStruct((b, heads, dh, LANES), F32)],
        compiler_params=_params("parallel", "arbitrary"),
        name="moba_sample_scores",
    )(page_table, qh, *([ck_t] * n_u))

    sel = pl.pallas_call(
        functools.partial(_moba_select_body, nb=nb, topk=MOBA_TOPK, heads=heads),
        grid=(b,),
        in_specs=[pl.BlockSpec((1, heads, dh, LANES), lambda bi: (bi, 0, 0, 0)),
                  pl.BlockSpec((1, heads, 8, dh), lambda bi: (bi, 0, 0, 0))],
        out_specs=pl.BlockSpec((1, heads, 8, LANES), lambda bi: (bi, 0, 0, 0)),
        out_shape=jax.ShapeDtypeStruct((b, heads, 8, LANES), jnp.int32),
        compiler_params=_params("parallel"),
        name="moba_sample_select",
    )(ksum, qh)
    n_sel = lq * MOBA_TOPK
    sel = sel[:, :, :lq, :MOBA_TOPK].reshape(b, heads * n_sel)

    head_spec = pl.BlockSpec((1, 1, 8, dh), lambda bi, hi, pt, sl: (bi, hi, 0, 0))
    o = pl.pallas_call(
        functools.partial(_moba_gather_body, layer=layer, lq=lq, topk=MOBA_TOPK, heads=heads, dh=dh, past=past),
        grid_spec=pltpu.PrefetchScalarGridSpec(
            num_scalar_prefetch=2, grid=(b, heads),
            in_specs=[pl.BlockSpec((1, 1, 8, past), lambda bi, hi, pt, sl: (bi, hi, 0, 0)),
                      head_spec, head_spec, head_spec, pl.BlockSpec(memory_space=pl.ANY)],
            out_specs=head_spec,
            scratch_shapes=[pltpu.VMEM((2, dh, n_sel * MOBA_BLOCK), F32), pltpu.SemaphoreType.DMA((2,))]),
        out_shape=jax.ShapeDtypeStruct((b, heads, 8, dh), F32),
        compiler_params=_params("arbitrary", "arbitrary"),
        name="moba_sample_gather",
    )(page_table, sel, scores, qh, per_head(k_new), per_head(v_new), cv_t)
    return o[:, :, :lq, :].transpose(0, 2, 1, 3).reshape(b, lq, w)


def _log_sigmoid(x):
    return -(jnp.maximum(-x, 0.0) + jnp.log1p(jnp.exp(-jnp.abs(x))))


def _mlstm_body(qk_ref, v_ref, og_ref, gcol_ref, grow_ref, bcol_ref, brow_ref, cw_ref, gh_ref, cbuf_ref,
                c0_ref, n0_ref, m0_ref, h_ref, c_out, n_out, m_out, xbuf, c_s, n_s, m_s, *, ch, valid, heads, dh):
    c = pl.program_id(1)
    width = heads * dh

    @pl.when(c == 0)
    def _():
        xbuf[0:8, :] = cbuf_ref[0]
        c_s[...] = c0_ref[0]
        n_s[...] = n0_ref[0]
        m_s[...] = m0_ref[0]

    xbuf[8:8 + ch, :] = qk_ref[0]
    cw = cw_ref[...]
    back = CONV_W - 1
    y = xbuf[8 - back:8 - back + ch, :] * cw[0:1, :]
    for t in range(1, CONV_W):
        y = y + xbuf[8 - back + t:8 - back + t + ch, :] * cw[t:t + 1, :]
    xbuf[8 - back:8, :] = xbuf[8 + ch - back:8 + ch, :]
    y = y * _sigmoid(y)

    t_col = lax.broadcasted_iota(jnp.int32, (ch, LANES), 0)
    t_row = lax.broadcasted_iota(jnp.int32, (8, ch), 1)
    gcol = gcol_ref[0] + bcol_ref[...]
    grow = grow_ref[0, 0] + brow_ref[:, :ch]
    ig_col = jnp.where(t_col < valid, gcol, NEG)
    lf_col = jnp.where(t_col < valid, _log_sigmoid(gcol), 0.0)
    ig_row = jnp.where(t_row < valid, grow, NEG)
    lf_row = jnp.where(t_row < valid, _log_sigmoid(grow), 0.0)
    ti = lax.broadcasted_iota(jnp.int32, (ch, ch), 0)
    si = lax.broadcasted_iota(jnp.int32, (ch, ch), 1)
    causal = ti >= si
    b_col = _dot(causal.astype(F32), lf_col, HI)
    b_row = _dot(lf_row, (ti <= si).astype(F32), HI)

    v = v_ref[0]
    og = og_ref[0]
    gh = gh_ref[...]
    qhs = [y[:, h * dh:(h + 1) * dh] for h in range(heads)]
    khs = [y[:, width + h * dh:width + (h + 1) * dh] * (dh ** -0.5) for h in range(heads)]
    qk_all = [_nt(qhs[h].astype(BF16), khs[h].astype(BF16)) for h in range(heads)]
    qc_all = [_dot(qhs[h].astype(BF16), c_s[h].astype(BF16)) for h in range(heads)]
    for h in range(heads):
        sl = slice(h * dh, (h + 1) * dh)
        qh, kh = qhs[h], khs[h]
        vh = v[:, sl]
        igc, bc = ig_col[:, h:h + 1], b_col[:, heads + h:heads + h + 1]
        igr, br = ig_row[h:h + 1, :], b_row[heads + h:heads + h + 1, :]
        m_prev = m_s[h][:, 0:1]
        c_prev = c_s[h]
        n_prev = n_s[h]
        log_d = jnp.where(causal, bc - br + igr, NEG)
        inter = bc + m_prev
        mt = jnp.maximum(inter, jnp.max(log_d, axis=1, keepdims=True))
        d = jnp.exp(log_d - mt)
        w_inter = jnp.exp(inter - mt)
        vb = vh.astype(BF16)
        a = qk_all[h] * d
        num = _dot(a.astype(BF16), vb) + w_inter * qc_all[h]
        den = jnp.sum(a, axis=1, keepdims=True) + w_inter * jnp.sum(qh * n_prev, axis=1, keepdims=True)
        hh = num / jnp.maximum(jnp.abs(den), jnp.exp(-mt))
        b_last = bc[ch - 1:ch, :]
        logw = b_last - bc + igc
        m_new = jnp.maximum(b_last + m_prev, jnp.max(logw, axis=0, keepdims=True))
        ws = jnp.exp(logw - m_new)
        decay = jnp.exp(b_last + m_prev - m_new)
        kw = kh * ws
        c_s[h] = decay * c_prev + lax.dot_general(kw.astype(BF16), vb, TN_DIMS, preferred_element_type=F32)
        n_s[h] = decay * n_prev + jnp.sum(kw, axis=0, keepdims=True)
        m_s[h] = jnp.broadcast_to(m_new, (1, LANES))
        hn = _rms(hh, gh[:, sl])
        h_ref[0, :, sl] = hn * _sigmoid(og[:, sl])

    @pl.when(c == pl.num_programs(1) - 1)
    def _():
        c_out[0] = c_s[...]
        n_out[0] = n_s[...]
        m_out[0] = m_s[...]


def mlstm_layer(qk, v, og, gates, b_g, conv_w, g_h, conv_buf, c0, n0, m0, valid):
    b, lp, w2 = qk.shape
    w = w2 // 2
    heads, dh, ch = B_HEADS, B_HEAD_DIM, MLSTM_CHUNK
    nc = lp // ch
    assert valid == lp or nc == 1
    grow = gates[:, :, :8].reshape(b, nc, ch, 8).transpose(0, 1, 3, 2)
    bcol = jnp.pad(b_g, (0, LANES - 8)).reshape(1, LANES)
    brow = jnp.broadcast_to(b_g[:, None], (8, LANES))
    cw = jnp.pad(conv_w, ((0, 8 - CONV_W), (0, 0)))
    cbuf = jnp.pad(conv_buf, ((0, 0), (8 - (CONV_W - 1), 0), (0, 0)))
    n0 = n0.reshape(b, heads, 1, dh)
    m0 = jnp.broadcast_to(m0[:, :, None, None], (b, heads, 1, LANES))
    full = lambda shape: pl.BlockSpec(shape, lambda bi, c: (0,) * len(shape))
    per_b = lambda shape: pl.BlockSpec(shape, lambda bi, c: (bi,) + (0,) * (len(shape) - 1))
    seq = lambda width: pl.BlockSpec((1, ch, width), lambda bi, c: (bi, c, 0))
    h, c_f, n_f, m_f = pl.pallas_call(
        functools.partial(_mlstm_body, ch=ch, valid=valid if nc == 1 else ch, heads=heads, dh=dh),
        grid=(b, nc),
        in_specs=[seq(w2), seq(w), seq(w), seq(LANES),
                  pl.BlockSpec((1, 1, 8, ch), lambda bi, c: (bi, c, 0, 0)),
                  full((1, LANES)), full((8, LANES)), full((8, w2)), full((1, w)),
                  per_b((1, 8, w2)), per_b((1, heads, dh, dh)), per_b((1, heads, 1, dh)), per_b((1, heads, 1, LANES))],
        out_specs=[seq(w), per_b((1, heads, dh, dh)), per_b((1, heads, 1, dh)), per_b((1, heads, 1, LANES))],
        out_shape=[jax.ShapeDtypeStruct((b, lp, w), F32), jax.ShapeDtypeStruct((b, heads, dh, dh), F32),
                   jax.ShapeDtypeStruct((b, heads, 1, dh), F32), jax.ShapeDtypeStruct((b, heads, 1, LANES), F32)],
        scratch_shapes=[pltpu.VMEM((8 + ch, w2), F32), pltpu.VMEM((heads, dh, dh), F32),
                        pltpu.VMEM((heads, 1, dh), F32), pltpu.VMEM((heads, 1, LANES), F32)],
        compiler_params=_params("parallel", "arbitrary"),
        name="mlstm",
    )(qk, v, og, gates, grow, bcol, brow, cw, g_h.reshape(1, w), cbuf, c0, n0, m0)
    return h, c_f, n_f.reshape(b, heads, dh), m_f[:, :, 0, 0]


def _mla_prep_body(x_ref, g_ref, win_ref, gq_ref, gkv_ref, wnt_ref, wat_ref, wbt_ref, wkb_ref,
                   cosk_ref, sink_ref, cost_ref, sint_ref,
                   ckv_ref, kpe_ref, kcat_ref, kvt_ref, qt_ref, *, heads):
    xn = _rms(x_ref[...], g_ref[...]).astype(BF16)
    y = _dot(xn, win_ref[...])
    ckv = _rms(y[:, Q_LORA:Q_LORA + KV_LORA], gkv_ref[...])
    k0 = Q_LORA + KV_LORA
    kpe = y[:, k0:k0 + ROPE_DIM] * cosk_ref[...] + y[:, k0 + ROPE_DIM:k0 + 2 * ROPE_DIM] * sink_ref[...]
    ckv_ref[...] = ckv
    kpe_ref[...] = kpe
    kcat_ref[:, 0:KV_LORA] = ckv.astype(BF16)
    kvt_ref[...] = ckv.T.astype(BF16)
    kcat_ref[:, KV_LORA:KV_LORA + ROPE_DIM] = kpe.astype(BF16)
    cqt = _rms(y[:, :Q_LORA], gq_ref[...]).T.astype(BF16)
    nope_t = _dot(wnt_ref[...], cqt)
    qpe_t = _dot(wat_ref[...], cqt) * cost_ref[...] + _dot(wbt_ref[...], cqt) * sint_ref[...]
    scale = (NOPE_DIM + ROPE_DIM) ** -0.5
    for h in range(heads):
        lat_t = _dot(wkb_ref[h], nope_t[h * NOPE_DIM:(h + 1) * NOPE_DIM].astype(BF16))
        qt_ref[h, 0:KV_LORA, :] = (lat_t * scale).astype(BF16)
        qt_ref[h, KV_LORA:KV_LORA + ROPE_DIM, :] = (qpe_t[h * ROPE_DIM:(h + 1) * ROPE_DIM] * scale).astype(BF16)


def _rot_half(wpe):
    half = wpe.shape[-1] // 2
    return jnp.concatenate([-wpe[..., half:], wpe[..., :half]], axis=-1)


def mla_prep(x, gain, w_in, g_q, g_kv, w_qb, w_kb, pos, tm=512):
    m, d = x.shape
    heads = C_HEADS
    tm = min(tm, m)
    half = ROPE_DIM // 2
    freqs = ROPE_THETA ** (-jnp.arange(half, dtype=F32) / half)
    ang = pos.astype(F32)[:, None] * freqs
    cos_k = jnp.tile(jnp.cos(ang), (1, 2))
    sin_k = jnp.tile(jnp.sin(ang), (1, 2))
    cos_t = jnp.tile(cos_k, (1, heads)).T
    sin_t = jnp.tile(sin_k, (1, heads)).T
    k0 = Q_LORA + KV_LORA
    win = jnp.concatenate([w_in, _rot_half(w_in[:, k0:k0 + ROPE_DIM]),
                           jnp.zeros((d, 512 - k0 - 2 * ROPE_DIM), F32)], axis=1).astype(BF16)
    wq = w_qb.reshape(Q_LORA, heads, NOPE_DIM + ROPE_DIM)
    wnt = wq[:, :, :NOPE_DIM].reshape(Q_LORA, heads * NOPE_DIM).T.astype(BF16)
    wat = wq[:, :, NOPE_DIM:].reshape(Q_LORA, heads * ROPE_DIM).T.astype(BF16)
    wbt = _rot_half(wq[:, :, NOPE_DIM:]).reshape(Q_LORA, heads * ROPE_DIM).T.astype(BF16)
    wkb = w_kb.transpose(1, 0, 2).astype(BF16)
    full = lambda a: pl.BlockSpec(a.shape, lambda i: (0,) * a.ndim)
    rows = lambda width: pl.BlockSpec((tm, width), lambda i: (i, 0))
    cols = lambda height: pl.BlockSpec((height, tm), lambda i: (0, i))
    dcat = KV_LORA + ROPE_DIM
    gq, gkv, gm = g_q.reshape(1, -1), g_kv.reshape(1, -1), gain.reshape(1, -1)
    return pl.pallas_call(
        functools.partial(_mla_prep_body, heads=heads),
        grid=(m // tm,),
        in_specs=[rows(d), full(gm), full(win), full(gq), full(gkv), full(wnt), full(wat), full(wbt), full(wkb),
                  rows(ROPE_DIM), rows(ROPE_DIM), cols(heads * ROPE_DIM), cols(heads * ROPE_DIM)],
        out_specs=[rows(KV_LORA), rows(ROPE_DIM), rows(dcat), cols(KV_LORA),
                   pl.BlockSpec((heads, dcat, tm), lambda i: (0, 0, i))],
        out_shape=[jax.ShapeDtypeStruct((m, KV_LORA), F32), jax.ShapeDtypeStruct((m, ROPE_DIM), F32),
                   jax.ShapeDtypeStruct((m, dcat), BF16), jax.ShapeDtypeStruct((KV_LORA, m), BF16),
                   jax.ShapeDtypeStruct((heads, dcat, m), BF16)],
        compiler_params=_params("parallel"),
        name="mla_prep",
    )(x, gm, win, gq, gkv, wnt, wat, wbt, wkb, cos_k, sin_k, cos_t, sin_t)


def _mla_flash_body(qt_ref, k_ref, kvt_ref, wvbt_ref, o_ref, ml_s, acc_s, ot_s, *, heads, tq):
    i = pl.program_id(1)
    krow = lax.broadcasted_iota(jnp.int32, (tq, tq), 0)
    qcol = lax.broadcasted_iota(jnp.int32, (tq, tq), 1)
    causal = krow <= qcol

    def tile_step(kstart, first):
        k = k_ref[pl.ds(kstart, tq), :]
        vt = kvt_ref[:, pl.ds(kstart, tq)]
        s_all = [_dot(k, qt_ref[h]) for h in range(heads)]
        for h in range(heads):
            m_row, l_row = ml_s.at[2 * h:2 * h + 1, :], ml_s.at[2 * h + 1:2 * h + 2, :]
            if first:
                s = jnp.where(causal, s_all[h], NEG)
                m_new = jnp.max(s, axis=0, keepdims=True)
                p = jnp.exp(s - m_new)
                acc_s[h] = _dot(vt, p.astype(BF16))
                l_row[...] = jnp.sum(p, axis=0, keepdims=True)
            else:
                s = s_all[h]
                m = m_row[...]
                m_new = jnp.maximum(m, jnp.max(s, axis=0, keepdims=True))
                alpha = jnp.exp(m - m_new)
                p = jnp.exp(s - m_new)
                acc_s[h] = alpha * acc_s[h] + _dot(vt, p.astype(BF16))
                l_row[...] = alpha * l_row[...] + jnp.sum(p, axis=0, keepdims=True)
            m_row[...] = m_new

    tile_step(pl.multiple_of(i * tq, tq), True)

    def body(j, carry):
        tile_step(pl.multiple_of(j * tq, tq), False)
        return carry

    lax.fori_loop(0, i, body, 0)
    for h in range(heads):
        o_lat = (acc_s[h] / ml_s[2 * h + 1:2 * h + 2, :]).astype(BF16)
        ot_s[h * V_DIM:(h + 1) * V_DIM, :] = _dot(wvbt_ref[h], o_lat)
    o_ref[...] = ot_s[...].T


def mla_flash_prompt(qt, kcat, kvt, wvbt, batch, tq=256):
    heads, dcat, m = qt.shape
    l = m // batch
    nq = l // tq
    return pl.pallas_call(
        functools.partial(_mla_flash_body, heads=heads, tq=tq),
        grid=(batch, nq),
        in_specs=[pl.BlockSpec((heads, dcat, tq), lambda b, i: (0, 0, b * nq + i)),
                  pl.BlockSpec((l, dcat), lambda b, i: (b, 0)),
                  pl.BlockSpec((KV_LORA, l), lambda b, i: (0, b)),
                  pl.BlockSpec(wvbt.shape, lambda b, i: (0, 0, 0))],
        out_specs=pl.BlockSpec((tq, heads * V_DIM), lambda b, i: (b * nq + i, 0)),
        out_shape=jax.ShapeDtypeStruct((m, heads * V_DIM), F32),
        scratch_shapes=[pltpu.VMEM((2 * heads, tq), F32), pltpu.VMEM((heads, KV_LORA, tq), F32),
                        pltpu.VMEM((heads * V_DIM, tq), F32)],
        compiler_params=_params("parallel", "arbitrary"),
        name="mla_flash_prompt",
    )(qt, kcat, kvt, wvbt)


def _mla_sample_body(pt_ref, q_ref, knew_ref, *rest, n_u, lq):
    kv_refs, pe_refs = rest[:n_u], rest[n_u:2 * n_u]
    o_ref, m_s, l_s, acc_s, kbuf, pbuf = rest[2 * n_u:]
    c = pl.program_id(1)
    q = q_ref[0]

    @pl.when(c == 0)
    def _():
        kn = knew_ref[0]
        s = _nt(q, kn)
        t = lax.broadcasted_iota(jnp.int32, s.shape, 1)
        qpos = lax.broadcasted_iota(jnp.int32, s.shape, 0) % lq
        s = jnp.where(t <= qpos, s, NEG)
        m0 = jnp.max(s, axis=1, keepdims=True)
        p = jnp.exp(s - m0)
        m_s[...] = m0
        l_s[...] = jnp.sum(p, axis=1, keepdims=True)
        acc_s[...] = _dot(p.astype(BF16), kn[:, :KV_LORA])

    for u in range(n_u):
        rows = slice(u * PAGE_SIZE, (u + 1) * PAGE_SIZE)
        kbuf[rows, :] = kv_refs[u][0].astype(BF16)
        pbuf[:, rows] = pe_refs[u][0].astype(BF16)
    s = _nt(q[:, :KV_LORA], kbuf[...]) + _dot(q[:, KV_LORA:], pbuf[...])
    m_prev = m_s[...]
    m_new = jnp.maximum(m_prev, jnp.max(s, axis=1, keepdims=True))
    alpha = jnp.exp(m_prev - m_new)
    p = jnp.exp(s - m_new)
    l_s[...] = alpha * l_s[...] + jnp.sum(p, axis=1, keepdims=True)
    acc_s[...] = alpha * acc_s[...] + _dot(p.astype(BF16), kbuf[...])
    m_s[...] = m_new

    @pl.when(c == pl.num_programs(1) - 1)
    def _():
        o_ref[0] = acc_s[...] / l_s[...]


def mla_sample(qt, kcat_new, cache_kv, cache_pe, layer, page_table, lq):
    heads, dcat, m = qt.shape
    b = m // lq
    n_pages = page_table.shape[1]
    n_u = PAGES_PER_STEP
    r_n = heads * lq
    q = qt.reshape(heads, dcat, b, lq).transpose(2, 0, 3, 1).reshape(b, r_n, dcat)
    knew = jnp.pad(kcat_new.reshape(b, lq, dcat), ((0, 0), (0, PAGE_SIZE - lq), (0, 0)))
    kv_specs = [pl.BlockSpec((None, 1, PAGE_SIZE, KV_LORA), lambda bi, c, pt, u=u: (layer, pt[bi, c * n_u + u], 0, 0))
                for u in range(n_u)]
    pe_t = cache_pe.transpose(0, 1, 3, 2)
    pe_specs = [pl.BlockSpec((None, 1, ROPE_DIM, PAGE_SIZE), lambda bi, c, pt, u=u: (layer, pt[bi, c * n_u + u], 0, 0))
                for u in range(n_u)]
    o = pl.pallas_call(
        functools.partial(_mla_sample_body, n_u=n_u, lq=lq),
        grid_spec=pltpu.PrefetchScalarGridSpec(
            num_scalar_prefetch=1, grid=(b, n_pages // n_u),
            in_specs=[pl.BlockSpec((1, r_n, dcat), lambda bi, c, pt: (bi, 0, 0)),
                      pl.BlockSpec((1, PAGE_SIZE, dcat), lambda bi, c, pt: (bi, 0, 0))] + kv_specs + pe_specs,
            out_specs=pl.BlockSpec((1, r_n, KV_LORA), lambda bi, c, pt: (bi, 0, 0)),
            scratch_shapes=[pltpu.VMEM((r_n, 1), F32), pltpu.VMEM((r_n, 1), F32), pltpu.VMEM((r_n, KV_LORA), F32),
                            pltpu.VMEM((n_u * PAGE_SIZE, KV_LORA), BF16), pltpu.VMEM((ROPE_DIM, n_u * PAGE_SIZE), BF16)]),
        out_shape=jax.ShapeDtypeStruct((b, r_n, KV_LORA), F32),
        compiler_params=_params("parallel", "arbitrary"),
        name="mla_sample",
    )(page_table, q, knew, *([cache_kv] * n_u), *([pe_t] * n_u))
    return o.reshape(b, heads, lq, KV_LORA).transpose(1, 0, 2, 3).reshape(heads, m, KV_LORA)


def _headproj_body(x_ref, w_ref, o_ref, *, heads):
    for h in range(heads):
        o_ref[:, h * V_DIM:(h + 1) * V_DIM] = _dot(x_ref[h].astype(BF16), w_ref[h])


def head_value_proj(o_lat, wvb):
    heads, m, _ = o_lat.shape
    return pl.pallas_call(
        functools.partial(_headproj_body, heads=heads),
        out_shape=jax.ShapeDtypeStruct((m, heads * V_DIM), F32),
        compiler_params=pltpu.CompilerParams(vmem_limit_bytes=VMEM_LIMIT),
        name="mla_value_proj",
    )(o_lat, wvb)


def _cross_body(q_ref, k_ref, v_ref, o_ref, *, heads, dh, interleaved):
    q = q_ref[0]
    scale = dh ** -0.5
    for h in range(heads):
        sl = slice(h * dh, (h + 1) * dh)
        if interleaved:
            rows = pl.ds(h, k_ref.shape[1] // heads, stride=heads)
            kh, vh = k_ref[0, rows, :], v_ref[0, rows, :]
        else:
            kh, vh = k_ref[0, :, sl], v_ref[0, :, sl]
        s = _nt((q[:, sl] * scale).astype(BF16), kh.astype(BF16))
        m = jnp.max(s, axis=1, keepdims=True)
        p = jnp.exp(s - m)
        l = jnp.sum(p, axis=1, keepdims=True)
        o_ref[0, :, sl] = _dot(p.astype(BF16), vh.astype(BF16)) / l


def cross_core(q, mk, mv, tq=512, layer=None):
    b, lq, w = q.shape
    tq = min(tq, lq)
    if layer is None:
        kv_spec = pl.BlockSpec((1,) + mk.shape[1:], lambda bi, i: (bi, 0, 0))
    else:
        kv_spec = pl.BlockSpec((None, 1) + mk.shape[2:], lambda bi, i: (layer, bi, 0, 0))
    return pl.pallas_call(
        functools.partial(_cross_body, heads=X_HEADS, dh=X_HEAD_DIM, interleaved=layer is not None),
        grid=(b, lq // tq),
        in_specs=[pl.BlockSpec((1, tq, w), lambda bi, i: (bi, i, 0)), kv_spec, kv_spec],
        out_specs=pl.BlockSpec((1, tq, w), lambda bi, i: (bi, i, 0)),
        out_shape=jax.ShapeDtypeStruct((b, lq, w), F32),
        compiler_params=_params("parallel", "parallel"),
        name="cross_core",
    )(q, mk, mv)


def _cross_fused_body(x_ref, g_ref, wq_ref, wo_ref, k_ref, v_ref, o_ref, *, heads, dh):
    x = x_ref[...]
    q = _dot(_rms(x, g_ref[...]).astype(BF16), wq_ref[...])
    scale = dh ** -0.5
    outs = []
    for h in range(heads):
        sl = slice(h * dh, (h + 1) * dh)
        s = _nt((q[:, sl] * scale).astype(BF16), k_ref[0, :, sl].astype(BF16))
        m = jnp.max(s, axis=1, keepdims=True)
        p = jnp.exp(s - m)
        l = jnp.sum(p, axis=1, keepdims=True)
        outs.append((_dot(p.astype(BF16), v_ref[0, :, sl].astype(BF16)) / l).astype(BF16))
    o_ref[...] = x + _dot(jnp.concatenate(outs, axis=1), wo_ref[...])


def cross_fused(x, gain, w_q, w_o, mk, mv, tq=512):
    m, d = x.shape
    b, n_mem, w = mk.shape
    l = m // b
    tq = min(tq, l)
    nq = l // tq
    return pl.pallas_call(
        functools.partial(_cross_fused_body, heads=X_HEADS, dh=X_HEAD_DIM),
        grid=(b, nq),
        in_specs=[pl.BlockSpec((tq, d), lambda bi, i: (bi * nq + i, 0)),
                  pl.BlockSpec((1, d), lambda bi, i: (0, 0)),
                  pl.BlockSpec(w_q.shape, lambda bi, i: (0, 0)),
                  pl.BlockSpec(w_o.shape, lambda bi, i: (0, 0)),
                  pl.BlockSpec((1, n_mem, w), lambda bi, i: (bi, 0, 0)),
                  pl.BlockSpec((1, n_mem, w), lambda bi, i: (bi, 0, 0))],
        out_specs=pl.BlockSpec((tq, d), lambda bi, i: (bi * nq + i, 0)),
        out_shape=jax.ShapeDtypeStruct((m, d), F32),
        compiler_params=_params("parallel", "parallel"),
        name="cross_fused",
    )(x, gain.reshape(1, d), w_q, w_o, mk, mv)


def _moe_gates(logits):
    lane = lax.broadcasted_iota(jnp.int32, logits.shape, 1).astype(F32)
    big = 1e9
    is_g = lane < N_GROUPS
    gl = jnp.where(is_g, logits, -jnp.inf)
    gmax = jnp.max(gl, axis=1, keepdims=True)
    grp = jnp.min(jnp.where(is_g & (gl == gmax), lane, big), axis=1, keepdims=True)
    p_grp = 1.0 / jnp.sum(jnp.where(is_g, jnp.exp(gl - gmax), 0.0), axis=1, keepdims=True)
    e_idx = lane - N_GROUPS
    in_grp = (e_idx >= grp * EXPERTS_PER_GROUP) & (e_idx < (grp + 1) * EXPERTS_PER_GROUP)
    el = jnp.where(in_grp, logits, -jnp.inf)
    t1 = jnp.max(el, axis=1, keepdims=True)
    i1 = jnp.min(jnp.where(in_grp & (el == t1), lane, big), axis=1, keepdims=True)
    el2 = jnp.where(lane == i1, -jnp.inf, el)
    t2 = jnp.max(el2, axis=1, keepdims=True)
    i2 = jnp.min(jnp.where(in_grp & (lane != i1) & (el2 == t2), lane, big), axis=1, keepdims=True)
    e2 = jnp.exp(t2 - t1)
    w1 = 1.0 / (1.0 + e2)
    w2 = e2 / (1.0 + e2)
    return p_grp * (jnp.where(lane == i1, w1, 0.0) + jnp.where(lane == i2, w2, 0.0))


def _moe_body(x_ref, g_ref, wr_ref, br_ref, wg_ref, wu_ref, wd_ref, o_ref, xn_s, gate_s, acc_s):
    e = pl.program_id(1)

    @pl.when(e == 0)
    def _():
        x = x_ref[...]
        xn = _rms(x, g_ref[...])
        xn_s[...] = xn.astype(BF16)
        gate_s[...] = _moe_gates(_dot(xn.astype(BF16), wr_ref[...]) + br_ref[...])
        acc_s[...] = x

    xn = xn_s[...]
    lane = lax.broadcasted_iota(jnp.int32, gate_s.shape, 1)
    ge = jnp.sum(jnp.where(lane == e + N_GROUPS, gate_s[...], 0.0), axis=1, keepdims=True)
    a = _dot(xn, wg_ref[0])
    u = _dot(xn, wu_ref[0])
    hid = (a * _sigmoid(a)) * u * ge
    acc_s[...] += _dot(hid.astype(BF16), wd_ref[0])

    @pl.when(e == pl.num_programs(1) - 1)
    def _():
        o_ref[...] = acc_s[...]


def moe_layer(x, gain, w_group, b_group, w_router, b_router, w_up, w_gate, w_down, layer, tm=1024):
    m, d = x.shape
    tm = min(tm, m)
    _, n_e, _, f = w_up.shape
    wr = jnp.concatenate([w_group, w_router, jnp.zeros((d, LANES - N_GROUPS - N_EXPERTS), F32)], axis=1).astype(BF16)
    br = jnp.concatenate([b_group, b_router, jnp.zeros((LANES - N_GROUPS - N_EXPERTS,), F32)]).reshape(1, LANES)
    return pl.pallas_call(
        _moe_body,
        grid=(m // tm, n_e),
        in_specs=[pl.BlockSpec((tm, d), lambda i, e: (i, 0)),
                  pl.BlockSpec((1, d), lambda i, e: (0, 0)),
                  pl.BlockSpec((d, LANES), lambda i, e: (0, 0)),
                  pl.BlockSpec((1, LANES), lambda i, e: (0, 0)),
                  pl.BlockSpec((None, 1, d, f), lambda i, e: (layer, e, 0, 0)),
                  pl.BlockSpec((None, 1, d, f), lambda i, e: (layer, e, 0, 0)),
                  pl.BlockSpec((None, 1, f, d), lambda i, e: (layer, e, 0, 0))],
        out_specs=pl.BlockSpec((tm, d), lambda i, e: (i, 0)),
        out_shape=jax.ShapeDtypeStruct((m, d), F32),
        scratch_shapes=[pltpu.VMEM((tm, d), BF16), pltpu.VMEM((tm, LANES), F32), pltpu.VMEM((tm, d), F32)],
        compiler_params=_params("parallel", "arbitrary"),
        name="moe",
    )(x, gain.reshape(1, d), wr, br, w_gate, w_up, w_down)


AB_SPLITS = [(0, A_WIDTH), (A_WIDTH, A_WIDTH), (2 * A_WIDTH, A_WIDTH), (3 * A_WIDTH, 2 * B_WIDTH),
             (3 * A_WIDTH + 2 * B_WIDTH, B_WIDTH), (3 * A_WIDTH + 3 * B_WIDTH, B_WIDTH),
             (3 * A_WIDTH + 4 * B_WIDTH, LANES)]


def _even_projection(x, gain, w_in_bf):
    return fused_linear([x], [w_in_bf], splits=AB_SPLITS, gain=gain, name="in_proj_ab")


def _pad_seq(a, lp):
    return jnp.pad(a, ((0, 0), (0, lp - a.shape[1]), (0, 0)))


def kernel(x_prompt, x_sample, mem_prompt, cache_moba_k, cache_moba_v, cache_mla_kv, cache_mla_pe, state_mlstm_c, state_mlstm_n, state_mlstm_m, state_conv, cache_mem_k, cache_mem_v, page_table, norm_mix, norm_cross, norm_mem, norm_ffn, norm_final, w_in_ab, b_gates, conv_w, norm_mlstm, w_out_ab, w_in_c, norm_q_lat, norm_kv_lat, w_qb, w_kb, w_vb, w_out_c, w_cq, w_ck, w_cv, w_co, w_group, b_group, w_router, b_router, w_up, w_gate, w_down):
    bp, lp, d = x_prompt.shape
    bs, ls, _ = x_sample.shape
    depth = norm_mix.shape[0]
    mp, ms = bp * lp, bs * ls
    n_mem = mem_prompt.shape[1]
    past = page_table.shape[1] * PAGE_SIZE
    pos_p = jnp.tile(jnp.arange(lp, dtype=jnp.int32), bp)
    pos_s = jnp.tile(past + jnp.arange(ls, dtype=jnp.int32), bs)

    xp = x_prompt.reshape(mp, d)
    xs = x_sample.reshape(ms, d)
    mem = mem_prompt.reshape(bp * n_mem, d)
    ak_p, av_p, kv_p, pe_p, mc_p, mn_p, mm_p, cv_p, mk_p, mv_p = [], [], [], [], [], [], [], [], [], []
    ak_s, av_s, kv_s, pe_s, mc_s, mn_s, mm_s, cv_s = [], [], [], [], [], [], [], []
    back = CONV_W - 1
    ch = MLSTM_CHUNK
    wu_bf, wg_bf, wd_bf = w_up.astype(BF16), w_gate.astype(BF16), w_down.astype(BF16)
    mem_k_rows = cache_mem_k.reshape(depth, bs, n_mem * X_HEADS, X_HEAD_DIM)
    mem_v_rows = cache_mem_v.reshape(depth, bs, n_mem * X_HEADS, X_HEAD_DIM)
    ls_pad = -(-ls // ch) * ch
    for l in range(depth):
        j = l // 2
        if l % 2 == 0:
            n_ab = w_in_ab.shape[2]
            n_pad = AB_SPLITS[-1][0] + LANES
            w_in_bf = jnp.pad(w_in_ab[j], ((0, 0), (0, n_pad - n_ab))).astype(BF16)
            w_out_bf = w_out_ab[j].astype(BF16)
            qa, ka, va, qkb, vb, ob, gt = _even_projection(xp, norm_mix[l], w_in_bf)
            o_a = moba_prompt(qa.reshape(bp, lp, A_WIDTH), ka.reshape(bp, lp, A_WIDTH), va.reshape(bp, lp, A_WIDTH))
            qkb3 = qkb.reshape(bp, lp, 2 * B_WIDTH)
            hb, mc, mn, mm = mlstm_layer(
                qkb3, vb.reshape(bp, lp, B_WIDTH), ob.reshape(bp, lp, B_WIDTH), gt.reshape(bp, lp, LANES),
                b_gates[j], conv_w[j], norm_mlstm[j], jnp.zeros((bp, back, 2 * B_WIDTH), F32),
                jnp.zeros((bp, B_HEADS, B_HEAD_DIM, B_HEAD_DIM), F32), jnp.zeros((bp, B_HEADS, B_HEAD_DIM), F32),
                jnp.zeros((bp, B_HEADS), F32), valid=lp)
            (xp,) = fused_linear([o_a.reshape(mp, A_WIDTH), hb.reshape(mp, B_WIDTH)],
                                 [w_out_bf[:A_WIDTH], w_out_bf[A_WIDTH:]], residual=xp, name="out_proj_ab")
            ak_p.append(ka.reshape(bp, lp, A_HEADS, A_HEAD_DIM)); av_p.append(va.reshape(bp, lp, A_HEADS, A_HEAD_DIM))
            cv_p.append(qkb3[:, lp - back:, :]); mc_p.append(mc); mn_p.append(mn); mm_p.append(mm)
            qa, ka, va, qkb, vb, ob, gt = _even_projection(xs, norm_mix[l], w_in_bf)
            o_a = moba_sample(qa.reshape(bs, ls, A_WIDTH), ka.reshape(bs, ls, A_WIDTH), va.reshape(bs, ls, A_WIDTH),
                              cache_moba_k, cache_moba_v, j, page_table)
            qkb3 = qkb.reshape(bs, ls, 2 * B_WIDTH)
            hb, mc, mn, mm = mlstm_layer(
                _pad_seq(qkb3, ls_pad), _pad_seq(vb.reshape(bs, ls, B_WIDTH), ls_pad),
                _pad_seq(ob.reshape(bs, ls, B_WIDTH), ls_pad), _pad_seq(gt.reshape(bs, ls, LANES), ls_pad),
                b_gates[j], conv_w[j], norm_mlstm[j], state_conv[j],
                state_mlstm_c[j], state_mlstm_n[j], state_mlstm_m[j], valid=ls)
            (xs,) = fused_linear([o_a.reshape(ms, A_WIDTH), hb[:, :ls].reshape(ms, B_WIDTH)],
                                 [w_out_bf[:A_WIDTH], w_out_bf[A_WIDTH:]], residual=xs, name="out_proj_ab")
            ak_s.append(ka.reshape(bs, ls, A_HEADS, A_HEAD_DIM)); av_s.append(va.reshape(bs, ls, A_HEADS, A_HEAD_DIM))
            conv_all = jnp.concatenate([state_conv[j], qkb3], axis=1)
            cv_s.append(conv_all[:, ls:, :]); mc_s.append(mc); mn_s.append(mn); mm_s.append(mm)
        else:
            wvb = w_vb[j].transpose(1, 0, 2).astype(BF16)
            w_out_bf = w_out_c[j].astype(BF16)
            ckv, kpe, kcat, kvt, qt = mla_prep(xp, norm_mix[l], w_in_c[j], norm_q_lat[j], norm_kv_lat[j], w_qb[j], w_kb[j], pos_p)
            o = mla_flash_prompt(qt, kcat, kvt, w_vb[j].transpose(1, 2, 0).astype(BF16), bp)
            (xp,) = fused_linear([o], [w_out_bf], residual=xp, name="out_proj_c")
            kv_p.append(ckv.reshape(bp, lp, KV_LORA)); pe_p.append(kpe.reshape(bp, lp, ROPE_DIM))
            ckv, kpe, kcat, _, qt = mla_prep(xs, norm_mix[l], w_in_c[j], norm_q_lat[j], norm_kv_lat[j], w_qb[j], w_kb[j], pos_s)
            o_lat = mla_sample(qt, kcat, cache_mla_kv, cache_mla_pe, j, page_table, ls)
            o = head_value_proj(o_lat, wvb)
            (xs,) = fused_linear([o], [w_out_bf], residual=xs, name="out_proj_c")
            kv_s.append(ckv.reshape(bs, ls, KV_LORA)); pe_s.append(kpe.reshape(bs, ls, ROPE_DIM))
        w_ckv = jnp.concatenate([w_ck[l], w_cv[l]], axis=1).astype(BF16)
        mk, mv = fused_linear([mem], [w_ckv], splits=[(0, X_WIDTH), (X_WIDTH, X_WIDTH)], gain=norm_mem[l], name="mem_kv")
        mk_p.append(mk.reshape(bp, n_mem, X_HEADS, X_HEAD_DIM)); mv_p.append(mv.reshape(bp, n_mem, X_HEADS, X_HEAD_DIM))
        w_cq_bf, w_co_bf = w_cq[l].astype(BF16), w_co[l].astype(BF16)
        xp = cross_fused(xp, norm_cross[l], w_cq_bf, w_co_bf, mk.reshape(bp, n_mem, X_WIDTH), mv.reshape(bp, n_mem, X_WIDTH))
        (q,) = fused_linear([xs], [w_cq_bf], gain=norm_cross[l], name="cross_q")
        q8 = _pad_seq(q.reshape(bs, ls, X_WIDTH), 8)
        o = cross_core(q8, mem_k_rows, mem_v_rows, layer=l)
        (xs,) = fused_linear([o[:, :ls].reshape(ms, X_WIDTH)], [w_co_bf], residual=xs, name="cross_out")
        xp = moe_layer(xp, norm_ffn[l], w_group[l], b_group[l], w_router[l], b_router[l], wu_bf, wg_bf, wd_bf, l)
        xs = moe_layer(xs, norm_ffn[l], w_group[l], b_group[l], w_router[l], b_router[l], wu_bf, wg_bf, wd_bf, l)
    y_prompt = rmsnorm_rows(xp, norm_final).reshape(bp, lp, d)
    y_sample = rmsnorm_rows(xs, norm_final).reshape(bs, ls, d)
    return (y_prompt, y_sample,
            jnp.stack(ak_p), jnp.stack(av_p), jnp.stack(kv_p), jnp.stack(pe_p),
            jnp.stack(mc_p), jnp.stack(mn_p), jnp.stack(mm_p), jnp.stack(cv_p), jnp.stack(mk_p), jnp.stack(mv_p),
            jnp.stack(ak_s), jnp.stack(av_s), jnp.stack(kv_s), jnp.stack(pe_s),
            jnp.stack(mc_s), jnp.stack(mn_s), jnp.stack(mm_s), jnp.stack(cv_s))
```

```python
import functools
import math

import jax
import jax.numpy as jnp
from jax import lax
from jax.experimental import pallas as pl
from jax.experimental.pallas import tpu as pltpu

F32 = jnp.float32
BF16 = jnp.bfloat16
HI = lax.Precision.HIGHEST
EPS = 1e-6
NEG = -1e30
LOG2E = math.log2(math.e)
VMEM_LIMIT = 56 * 1024 * 1024
LANES = 128

PAGE_SIZE = 128
A_HEADS, A_HEAD_DIM = 8, 64
A_WIDTH = A_HEADS * A_HEAD_DIM
MOBA_BLOCK, MOBA_TOPK = 256, 3
B_HEADS, B_HEAD_DIM = 4, 128
B_WIDTH = B_HEADS * B_HEAD_DIM
CONV_W = 4
MLSTM_CHUNK = 128
C_HEADS, Q_LORA, KV_LORA, NOPE_DIM, ROPE_DIM, V_DIM = 16, 256, 128, 64, 32, 64
ROPE_THETA = 10000.0
MLA_LOGIT_SCALE = LOG2E * (NOPE_DIM + ROPE_DIM) ** -0.5
X_HEADS, X_HEAD_DIM = 4, 128
X_WIDTH = X_HEADS * X_HEAD_DIM
N_GROUPS, EXPERTS_PER_GROUP = 4, 4
N_EXPERTS = N_GROUPS * EXPERTS_PER_GROUP
PAGES_PER_STEP = 32
MOBA_PAGES_PER_STEP = 16

NT_DIMS = (((1,), (1,)), ((), ()))
TN_DIMS = (((0,), (0,)), ((), ()))


def _params(*sem):
    return pltpu.CompilerParams(dimension_semantics=sem, vmem_limit_bytes=VMEM_LIMIT)


def _nt(a, b, precision=None):
    return lax.dot_general(a, b, NT_DIMS, precision=precision, preferred_element_type=F32)


def _dot(a, b, precision=None):
    return jnp.dot(a, b, precision=precision, preferred_element_type=F32)


def _rms(x, g):
    return x * lax.rsqrt(jnp.mean(x * x, axis=-1, keepdims=True) + EPS) * g


def _sigmoid(x):
    return 1.0 / (1.0 + jnp.exp(-x))


def _linear_body(*refs, n_in, has_gain, has_res, splits):
    x_refs, w_refs = refs[:n_in], refs[n_in:2 * n_in]
    p = 2 * n_in
    g_ref = refs[p] if has_gain else None
    p += int(has_gain)
    r_ref = refs[p] if has_res else None
    p += int(has_res)
    o_refs = refs[p:]
    xs = []
    for xr in x_refs:
        x = xr[...]
        if has_gain:
            x = _rms(x, g_ref[...])
        xs.append(x.astype(BF16))
    for (off, width), o_ref in zip(splits, o_refs):
        acc = None
        for x, wr in zip(xs, w_refs):
            y = _dot(x, wr[:, off:off + width])
            acc = y if acc is None else acc + y
        if has_res:
            acc = acc + r_ref[...]
        o_ref[...] = acc.astype(o_ref.dtype)


def fused_linear(xs, ws, splits=None, gain=None, residual=None, tm=512, name="linear"):
    m, n = xs[0].shape[0], ws[0].shape[1]
    splits = splits or [(0, n)]
    tm = min(tm, m)
    assert m % tm == 0
    in_specs = [pl.BlockSpec((tm, x.shape[1]), lambda i: (i, 0)) for x in xs]
    in_specs += [pl.BlockSpec(w.shape, lambda i: (0, 0)) for w in ws]
    args = list(xs) + list(ws)
    if gain is not None:
        in_specs.append(pl.BlockSpec((1, gain.shape[-1]), lambda i: (0, 0)))
        args.append(gain.reshape(1, -1))
    if residual is not None:
        assert len(splits) == 1
        in_specs.append(pl.BlockSpec((tm, n), lambda i: (i, 0)))
        args.append(residual)
    outs = pl.pallas_call(
        functools.partial(_linear_body, n_in=len(xs), has_gain=gain is not None, has_res=residual is not None,
                          splits=tuple(splits)),
        grid=(m // tm,),
        in_specs=in_specs,
        out_specs=[pl.BlockSpec((tm, w), lambda i: (i, 0)) for _, w in splits],
        out_shape=[jax.ShapeDtypeStruct((m, w), F32) for _, w in splits],
        compiler_params=_params("parallel"),
        name=name,
    )(*args)
    return outs


def _rmsnorm_body(x_ref, g_ref, o_ref):
    o_ref[...] = _rms(x_ref[...], g_ref[...])


def rmsnorm_rows(x, g, tm=1024):
    m, d = x.shape
    tm = min(tm, m)
    return pl.pallas_call(
        _rmsnorm_body,
        grid=(m // tm,),
        in_specs=[pl.BlockSpec((tm, d), lambda i: (i, 0)), pl.BlockSpec((1, d), lambda i: (0, 0))],
        out_specs=pl.BlockSpec((tm, d), lambda i: (i, 0)),
        out_shape=jax.ShapeDtypeStruct((m, d), F32),
        compiler_params=_params("parallel"),
        name="final_norm",
    )(x, g.reshape(1, d))


def _topk_mask(g, valid, n_iota, nb, topk):
    gm = jnp.where(valid, g, -jnp.inf)
    rank = jnp.zeros(g.shape, jnp.int32)
    for m in range(nb):
        gc = gm[:, m:m + 1]
        beats = (gc > gm) | ((gc == gm) & (m < n_iota))
        rank = rank + beats.astype(jnp.int32)
    return ((rank < topk) & valid).astype(F32)


def _topk_rows(g, valid, n_iota, nb, topk):
    gm = jnp.where(valid, g, -jnp.inf)
    rank = jnp.zeros(g.shape, jnp.int32)
    for m in range(nb):
        gr = gm[m:m + 1, :]
        beats = (gr > gm) | ((gr == gm) & (m < n_iota))
        rank = rank + beats.astype(jnp.int32)
    return ((rank < topk) & valid).astype(F32)


def _moba_prompt_body(q_ref, k_ref, v_ref, o_ref, kmean_s, kh_s, vt_s, qt_s, sel_s, ml_s, acc_s, ot_s, *,
                      nb, blk, heads, dh, topk):
    i = pl.program_id(1)

    @pl.when(i == 0)
    def _():
        for n in range(nb):
            rows = slice(n * blk, (n + 1) * blk)
            kmean_s[n:n + 1, :] = jnp.mean(k_ref[0, rows, :], axis=0, keepdims=True)
            vt_s[:, rows] = v_ref[0, rows, :].T.astype(BF16)
        for h in range(heads):
            kh_s[h] = k_ref[0, :, h * dh:(h + 1) * dh].astype(BF16)

    qt = q_ref[0].T
    krow = lax.broadcasted_iota(jnp.int32, (blk, blk), 0)
    qcol = lax.broadcasted_iota(jnp.int32, (blk, blk), 1)
    dmat = (qcol - krow).astype(F32)
    n_iota = lax.broadcasted_iota(jnp.int32, (nb, blk), 0)
    slopes = [2.0 ** (-8.0 * (h + 1) / heads) for h in range(heads)]
    head_rows = [slice(h * dh, (h + 1) * dh) for h in range(heads)]
    for h in range(heads):
        gate_t = _dot(kmean_s[:, head_rows[h]].astype(BF16), qt[head_rows[h]].astype(BF16))
        sel_s[h] = _topk_rows(gate_t, n_iota < i, n_iota, nb, topk)
    qt_s[...] = (qt * dh ** -0.5).astype(BF16)

    def block_step(kstart, logits_fn, first):
        s_all = [_dot(kh_s[h, pl.ds(kstart, blk), :], qt_s[head_rows[h], :]) for h in range(heads)]
        for h in range(heads):
            logits = logits_fn(h, s_all[h])
            vt = vt_s[head_rows[h], pl.ds(kstart, blk)]
            m_row, l_row = ml_s.at[2 * h:2 * h + 1, :], ml_s.at[2 * h + 1:2 * h + 2, :]
            if first:
                m_new = jnp.max(logits, axis=0, keepdims=True)
                p = jnp.exp(logits - m_new)
                acc_s[h] = _dot(vt, p.astype(BF16))
                l_row[...] = jnp.sum(p, axis=0, keepdims=True)
            else:
                m = m_row[...]
                m_new = jnp.maximum(m, jnp.max(logits, axis=0, keepdims=True))
                alpha = jnp.exp(m - m_new)
                p = jnp.exp(logits - m_new)
                acc_s[h] = alpha * acc_s[h] + _dot(vt, p.astype(BF16))
                l_row[...] = alpha * l_row[...] + jnp.sum(p, axis=0, keepdims=True)
            m_row[...] = m_new

    block_step(pl.multiple_of(i * blk, blk), lambda h, s: jnp.where(dmat >= 0, s - slopes[h] * dmat, NEG), True)

    def body(j, carry):
        dist = dmat + ((i - j) * blk).astype(F32)
        pick = n_iota == j

        def logits_fn(h, s):
            selrow = jnp.sum(jnp.where(pick, sel_s[h], 0.0), axis=0, keepdims=True)
            return jnp.where(selrow > 0.5, s - slopes[h] * dist, NEG)

        block_step(pl.multiple_of(j * blk, blk), logits_fn, False)
        return carry

    lax.fori_loop(0, i, body, 0)
    for h in range(heads):
        ot_s[head_rows[h], :] = acc_s[h] / ml_s[2 * h + 1:2 * h + 2, :]
    o_ref[0] = ot_s[...].T


def moba_prompt(q, k, v):
    b, l, w = q.shape
    blk = MOBA_BLOCK
    nb = l // blk
    heads, dh = A_HEADS, A_HEAD_DIM
    return pl.pallas_call(
        functools.partial(_moba_prompt_body, nb=nb, blk=blk, heads=heads, dh=dh, topk=MOBA_TOPK),
        grid=(b, nb),
        in_specs=[pl.BlockSpec((1, blk, w), lambda bi, i: (bi, i, 0)),
                  pl.BlockSpec((1, l, w), lambda bi, i: (bi, 0, 0)),
                  pl.BlockSpec((1, l, w), lambda bi, i: (bi, 0, 0))],
        out_specs=pl.BlockSpec((1, blk, w), lambda bi, i: (bi, i, 0)),
        out_shape=jax.ShapeDtypeStruct((b, l, w), F32),
        scratch_shapes=[pltpu.VMEM((nb, w), F32), pltpu.VMEM((heads, l, dh), BF16), pltpu.VMEM((w, l), BF16),
                        pltpu.VMEM((w, blk), BF16), pltpu.VMEM((heads, nb, blk), F32),
                        pltpu.VMEM((2 * heads, blk), F32), pltpu.VMEM((heads, dh, blk), F32),
                        pltpu.VMEM((w, blk), F32)],
        compiler_params=_params("parallel", "arbitrary"),
        name="moba_prompt",
    )(q, k, v)


def _moba_scores_body(pt_ref, q_ref, *rest, n_u, heads):
    k_refs, (s_ref, ksum_ref) = rest[:n_u], rest[n_u:]
    c = pl.program_id(1)

    @pl.when(c == 0)
    def _():
        ksum_ref[...] = jnp.zeros(ksum_ref.shape, F32)

    ppb = MOBA_BLOCK // PAGE_SIZE
    dh = q_ref.shape[-1]
    lane = lax.broadcasted_iota(jnp.int32, (dh, LANES), 1)
    for h in range(heads):
        qh = q_ref[0, h].astype(BF16)
        ks = ksum_ref[0, h]
        for n in range(n_u // ppb):
            bsum = None
            for u in range(n * ppb, (n + 1) * ppb):
                kt = k_refs[u][0, h]
                s_ref[0, h, :, u * PAGE_SIZE:(u + 1) * PAGE_SIZE] = _dot(qh, kt.astype(BF16))
                bsum = kt if bsum is None else bsum + kt
            ks = jnp.where(lane == c * (n_u // ppb) + n, jnp.sum(bsum, axis=1, keepdims=True), ks)
        ksum_ref[0, h] = ks


def _moba_select_body(ksum_ref, q_ref, idx_ref, *, nb, topk, heads):
    rows = heads * 8
    lane = lax.broadcasted_iota(jnp.int32, (rows, LANES), 1)
    valid = lane < nb
    lane_f = lane.astype(F32)
    gates = [_dot(q_ref[0, h].astype(BF16), (ksum_ref[0, h] * (1.0 / MOBA_BLOCK)).astype(BF16)) for h in range(heads)]
    gate = jnp.where(valid, jnp.concatenate(gates, axis=0), -jnp.inf)
    rank = jnp.zeros(gate.shape, jnp.int32)
    for m in range(nb):
        gc = gate[:, m:m + 1]
        rank = rank + ((gc > gate) | ((gc == gate) & (m < lane))).astype(jnp.int32)
    out = jnp.zeros((rows, LANES), F32)
    for k in range(topk):
        idx_k = jnp.sum(jnp.where((rank == k) & valid, lane_f, 0.0), axis=1, keepdims=True)
        out = out + jnp.where(lane == k, idx_k, 0.0)
    idx_ref[0] = out.astype(jnp.int32).reshape(heads, 8, LANES)


def _moba_gather_body(pt_ref, sel_ref, s_ref, q_ref, knew_ref, vnew_ref, v_hbm, o_ref, vbuf, sem, *,
                      layer, lq, topk, heads, dh, past):
    b, h = pl.program_id(0), pl.program_id(1)
    n_h = pl.num_programs(1)
    step = b * n_h + h
    n_steps = pl.num_programs(0) * n_h
    slot = step % 2
    ppb = MOBA_BLOCK // PAGE_SIZE
    n_sel = lq * topk
    t_all = n_sel * MOBA_BLOCK

    def copies(bb, hh, sl):
        out = []
        for j in range(n_sel):
            blk = sel_ref[bb, hh * n_sel + j]
            for pg in range(ppb):
                page = pt_ref[bb, blk * ppb + pg]
                cols = pl.ds((j * ppb + pg) * PAGE_SIZE, PAGE_SIZE)
                out.append(pltpu.make_async_copy(v_hbm.at[layer, page, hh], vbuf.at[sl, :, cols], sem.at[sl]))
        return out

    @pl.when(step == 0)
    def _():
        for c in copies(b, h, slot):
            c.start()

    @pl.when(step + 1 < n_steps)
    def _():
        nxt = step + 1
        for c in copies(nxt // n_h, nxt % n_h, 1 - slot):
            c.start()

    scale = dh ** -0.5
    slope = jnp.exp2((-8.0 / heads) * (h + 1).astype(F32))
    row = lax.broadcasted_iota(jnp.int32, (8, t_all), 0)
    col = lax.broadcasted_iota(jnp.int32, (8, t_all), 1)
    choice = col // MOBA_BLOCK
    blocks = [sel_ref[b, h * n_sel + j] for j in range(n_sel)]
    s = jnp.concatenate([s_ref[0, 0, :, pl.ds(pl.multiple_of(blk * MOBA_BLOCK, MOBA_BLOCK), MOBA_BLOCK)]
                         for blk in blocks], axis=1)
    kpos = jnp.zeros((8, t_all), jnp.int32)
    for j in range(n_sel):
        kpos = jnp.where(choice == j, blocks[j] * MOBA_BLOCK, kpos)
    kpos = kpos + (col - choice * MOBA_BLOCK)
    dist = ((past + row) - kpos).astype(F32)
    logits = jnp.where((choice // topk) == row, s * scale - slope * dist, NEG)
    row1 = lax.broadcasted_iota(jnp.int32, (8, 1), 0)
    q8, knew, vnew = q_ref[0, 0], knew_ref[0, 0], vnew_ref[0, 0]
    own = []
    for t in range(lq):
        so = jnp.sum(q8 * knew[t:t + 1, :], axis=1, keepdims=True) * scale - slope * (row1 - t).astype(F32)
        own.append(jnp.where(row1 >= t, so, NEG))
    m = jnp.max(logits, axis=1, keepdims=True)
    for so in own:
        m = jnp.maximum(m, so)
    p = jnp.exp(logits - m)
    l = jnp.sum(p, axis=1, keepdims=True)

    for c in copies(b, h, slot):
        c.wait()
    acc = _nt(p.astype(BF16), vbuf[slot].astype(BF16))
    for t, so in enumerate(own):
        po = jnp.exp(so - m)
        l = l + po
        acc = acc + po * vnew[t:t + 1, :]
    o_ref[0, 0] = acc / l


def moba_sample(q, k_new, v_new, cache_k, cache_v, layer, page_table):
    b, lq, w = q.shape
    heads, dh = A_HEADS, A_HEAD_DIM
    n_pages = page_table.shape[1]
    past = n_pages * PAGE_SIZE
    nb = past // MOBA_BLOCK
    assert past % MOBA_BLOCK == 0 and MOBA_TOPK <= nb <= LANES and lq <= 8
    n_u = MOBA_PAGES_PER_STEP
    ppb = MOBA_BLOCK // PAGE_SIZE
    ck_t = cache_k.transpose(0, 1, 3, 4, 2)
    cv_t = cache_v.transpose(0, 1, 3, 4, 2)

    def per_head(a):
        a = a.reshape(b, lq, heads, dh).transpose(0, 2, 1, 3)
        return jnp.pad(a, ((0, 0), (0, 0), (0, 8 - lq), (0, 0)))

    qh = per_head(q)
    page_specs = [pl.BlockSpec((None, 1, heads, dh, PAGE_SIZE),
                               lambda bi, c, pt, u=u: (layer, pt[bi, c * n_u + u], 0, 0, 0)) for u in range(n_u)]
    scores, ksum = pl.pallas_call(
        functools.partial(_moba_scores_body, n_u=n_u, heads=heads),
        grid_spec=pltpu.PrefetchScalarGridSpec(
            num_scalar_prefetch=1, grid=(b, n_pages // n_u),
            in_specs=[pl.BlockSpec((1, heads, 8, dh), lambda bi, c, pt: (bi, 0, 0, 0))] + page_specs,
            out_specs=[pl.BlockSpec((1, heads, 8, n_u * PAGE_SIZE), lambda bi, c, pt: (bi, 0, 0, c)),
                       pl.BlockSpec((1, heads, dh, LANES), lambda bi, c, pt: (bi, 0, 0, 0))]),
        out_shape=[jax.ShapeDtypeStruct((b, heads, 8, past), F32), jax.ShapeDtypeStruct((b, heads, dh, LANES), F32)],
        compiler_params=_params("parallel", "arbitrary"),
        name="moba_sample_scores",
    )(page_table, qh, *([ck_t] * n_u))

    sel = pl.pallas_call(
        functools.partial(_moba_select_body, nb=nb, topk=MOBA_TOPK, heads=heads),
        grid=(b,),
        in_specs=[pl.BlockSpec((1, heads, dh, LANES), lambda bi: (bi, 0, 0, 0)),
                  pl.BlockSpec((1, heads, 8, dh), lambda bi: (bi, 0, 0, 0))],
        out_specs=pl.BlockSpec((1, heads, 8, LANES), lambda bi: (bi, 0, 0, 0)),
        out_shape=jax.ShapeDtypeStruct((b, heads, 8, LANES), jnp.int32),
        compiler_params=_params("parallel"),
        name="moba_sample_select",
    )(ksum, qh)
    n_sel = lq * MOBA_TOPK
    sel = sel[:, :, :lq, :MOBA_TOPK].reshape(b, heads * n_sel)

    head_spec = pl.BlockSpec((1, 1, 8, dh), lambda bi, hi, pt, sl: (bi, hi, 0, 0))
    o = pl.pallas_call(
        functools.partial(_moba_gather_body, layer=layer, lq=lq, topk=MOBA_TOPK, heads=heads, dh=dh, past=past),
        grid_spec=pltpu.PrefetchScalarGridSpec(
            num_scalar_prefetch=2, grid=(b, heads),
            in_specs=[pl.BlockSpec((1, 1, 8, past), lambda bi, hi, pt, sl: (bi, hi, 0, 0)),
                      head_spec, head_spec, head_spec, pl.BlockSpec(memory_space=pl.ANY)],
            out_specs=head_spec,
            scratch_shapes=[pltpu.VMEM((2, dh, n_sel * MOBA_BLOCK), F32), pltpu.SemaphoreType.DMA((2,))]),
        out_shape=jax.ShapeDtypeStruct((b, heads, 8, dh), F32),
        compiler_params=_params("arbitrary", "arbitrary"),
        name="moba_sample_gather",
    )(page_table, sel, scores, qh, per_head(k_new), per_head(v_new), cv_t)
    return o[:, :, :lq, :].transpose(0, 2, 1, 3).reshape(b, lq, w)


def _log_sigmoid(x):
    return -(jnp.maximum(-x, 0.0) + jnp.log1p(jnp.exp(-jnp.abs(x))))


def _mlstm_body(qk_ref, v_ref, og_ref, gcol_ref, grow_ref, bcol_ref, brow_ref, cw_ref, gh_ref, cbuf_ref,
                c0_ref, n0_ref, m0_ref, h_ref, c_out, n_out, m_out, xbuf, c_s, n_s, m_s, *, ch, valid, heads, dh):
    c = pl.program_id(1)
    width = heads * dh

    @pl.when(c == 0)
    def _():
        xbuf[0:8, :] = cbuf_ref[0]
        c_s[...] = c0_ref[0]
        n_s[...] = n0_ref[0]
        m_s[...] = m0_ref[0]

    xbuf[8:8 + ch, :] = qk_ref[0]
    cw = cw_ref[...]
    back = CONV_W - 1
    y = xbuf[8 - back:8 - back + ch, :] * cw[0:1, :]
    for t in range(1, CONV_W):
        y = y + xbuf[8 - back + t:8 - back + t + ch, :] * cw[t:t + 1, :]
    xbuf[8 - back:8, :] = xbuf[8 + ch - back:8 + ch, :]
    y = y * _sigmoid(y)

    t_col = lax.broadcasted_iota(jnp.int32, (ch, LANES), 0)
    t_row = lax.broadcasted_iota(jnp.int32, (8, ch), 1)
    gcol = gcol_ref[0] + bcol_ref[...]
    grow = grow_ref[0, 0] + brow_ref[:, :ch]
    ig_col = jnp.where(t_col < valid, gcol, NEG)
    lf_col = jnp.where(t_col < valid, _log_sigmoid(gcol), 0.0)
    ig_row = jnp.where(t_row < valid, grow, NEG)
    lf_row = jnp.where(t_row < valid, _log_sigmoid(grow), 0.0)
    ti = lax.broadcasted_iota(jnp.int32, (ch, ch), 0)
    si = lax.broadcasted_iota(jnp.int32, (ch, ch), 1)
    causal = ti >= si
    b_col = _dot(causal.astype(F32), lf_col, HI)
    b_row = _dot(lf_row, (ti <= si).astype(F32), HI)

    v = v_ref[0]
    og = og_ref[0]
    gh = gh_ref[...]
    qhs = [y[:, h * dh:(h + 1) * dh] for h in range(heads)]
    khs = [y[:, width + h * dh:width + (h + 1) * dh] * (dh ** -0.5) for h in range(heads)]
    qk_all = [_nt(qhs[h].astype(BF16), khs[h].astype(BF16)) for h in range(heads)]
    qc_all = [_dot(qhs[h].astype(BF16), c_s[h].astype(BF16)) for h in range(heads)]
    for h in range(heads):
        sl = slice(h * dh, (h + 1) * dh)
        qh, kh = qhs[h], khs[h]
        vh = v[:, sl]
        igc, bc = ig_col[:, h:h + 1], b_col[:, heads + h:heads + h + 1]
        igr, br = ig_row[h:h + 1, :], b_row[heads + h:heads + h + 1, :]
        m_prev = m_s[h][:, 0:1]
        c_prev = c_s[h]
        n_prev = n_s[h]
        log_d = jnp.where(causal, bc - br + igr, NEG)
        inter = bc + m_prev
        mt = jnp.maximum(inter, jnp.max(log_d, axis=1, keepdims=True))
        d = jnp.exp(log_d - mt)
        w_inter = jnp.exp(inter - mt)
        vb = vh.astype(BF16)
        a = qk_all[h] * d
        num = _dot(a.astype(BF16), vb) + w_inter * qc_all[h]
        den = jnp.sum(a, axis=1, keepdims=True) + w_inter * jnp.sum(qh * n_prev, axis=1, keepdims=True)
        hh = num / jnp.maximum(jnp.abs(den), jnp.exp(-mt))
        b_last = bc[ch - 1:ch, :]
        logw = b_last - bc + igc
        m_new = jnp.maximum(b_last + m_prev, jnp.max(logw, axis=0, keepdims=True))
        ws = jnp.exp(logw - m_new)
        decay = jnp.exp(b_last + m_prev - m_new)
        kw = kh * ws
        c_s[h] = decay * c_prev + lax.dot_general(kw.astype(BF16), vb, TN_DIMS, preferred_element_type=F32)
        n_s[h] = decay * n_prev + jnp.sum(kw, axis=0, keepdims=True)
        m_s[h] = jnp.broadcast_to(m_new, (1, LANES))
        hn = _rms(hh, gh[:, sl])
        h_ref[0, :, sl] = hn * _sigmoid(og[:, sl])

    @pl.when(c == pl.num_programs(1) - 1)
    def _():
        c_out[0] = c_s[...]
        n_out[0] = n_s[...]
        m_out[0] = m_s[...]


def mlstm_layer(qk, v, og, gates, b_g, conv_w, g_h, conv_buf, c0, n0, m0, valid):
    b, lp, w2 = qk.shape
    w = w2 // 2
    heads, dh, ch = B_HEADS, B_HEAD_DIM, MLSTM_CHUNK
    nc = lp // ch
    assert valid == lp or nc == 1
    grow = gates[:, :, :8].reshape(b, nc, ch, 8).transpose(0, 1, 3, 2)
    bcol = jnp.pad(b_g, (0, LANES - 8)).reshape(1, LANES)
    brow = jnp.broadcast_to(b_g[:, None], (8, LANES))
    cw = jnp.pad(conv_w, ((0, 8 - CONV_W), (0, 0)))
    cbuf = jnp.pad(conv_buf, ((0, 0), (8 - (CONV_W - 1), 0), (0, 0)))
    n0 = n0.reshape(b, heads, 1, dh)
    m0 = jnp.broadcast_to(m0[:, :, None, None], (b, heads, 1, LANES))
    full = lambda shape: pl.BlockSpec(shape, lambda bi, c: (0,) * len(shape))
    per_b = lambda shape: pl.BlockSpec(shape, lambda bi, c: (bi,) + (0,) * (len(shape) - 1))
    seq = lambda width: pl.BlockSpec((1, ch, width), lambda bi, c: (bi, c, 0))
    h, c_f, n_f, m_f = pl.pallas_call(
        functools.partial(_mlstm_body, ch=ch, valid=valid if nc == 1 else ch, heads=heads, dh=dh),
        grid=(b, nc),
        in_specs=[seq(w2), seq(w), seq(w), seq(LANES),
                  pl.BlockSpec((1, 1, 8, ch), lambda bi, c: (bi, c, 0, 0)),
                  full((1, LANES)), full((8, LANES)), full((8, w2)), full((1, w)),
                  per_b((1, 8, w2)), per_b((1, heads, dh, dh)), per_b((1, heads, 1, dh)), per_b((1, heads, 1, LANES))],
        out_specs=[seq(w), per_b((1, heads, dh, dh)), per_b((1, heads, 1, dh)), per_b((1, heads, 1, LANES))],
        out_shape=[jax.ShapeDtypeStruct((b, lp, w), F32), jax.ShapeDtypeStruct((b, heads, dh, dh), F32),
                   jax.ShapeDtypeStruct((b, heads, 1, dh), F32), jax.ShapeDtypeStruct((b, heads, 1, LANES), F32)],
        scratch_shapes=[pltpu.VMEM((8 + ch, w2), F32), pltpu.VMEM((heads, dh, dh), F32),
                        pltpu.VMEM((heads, 1, dh), F32), pltpu.VMEM((heads, 1, LANES), F32)],
        compiler_params=_params("parallel", "arbitrary"),
        name="mlstm",
    )(qk, v, og, gates, grow, bcol, brow, cw, g_h.reshape(1, w), cbuf, c0, n0, m0)
    return h, c_f, n_f.reshape(b, heads, dh), m_f[:, :, 0, 0]


def _mla_prep_body(x_ref, g_ref, win_ref, gq_ref, gkv_ref, wnt_ref, wat_ref, wbt_ref, wkb_ref,
                   cosk_ref, sink_ref, cost_ref, sint_ref,
                   ckv_ref, kpe_ref, kcat_ref, kvt_ref, qt_ref, *, heads):
    xn = _rms(x_ref[...], g_ref[...]).astype(BF16)
    y = _dot(xn, win_ref[...])
    ckv = _rms(y[:, Q_LORA:Q_LORA + KV_LORA], gkv_ref[...])
    k0 = Q_LORA + KV_LORA
    kpe = y[:, k0:k0 + ROPE_DIM] * cosk_ref[...] + y[:, k0 + ROPE_DIM:k0 + 2 * ROPE_DIM] * sink_ref[...]
    ckv_ref[...] = ckv
    kpe_ref[...] = kpe
    kcat_ref[:, 0:KV_LORA] = ckv.astype(BF16)
    kvt_ref[...] = ckv.T.astype(BF16)
    kcat_ref[:, KV_LORA:KV_LORA + ROPE_DIM] = kpe.astype(BF16)
    cqt = _rms(y[:, :Q_LORA], gq_ref[...]).T.astype(BF16)
    nope_t = _dot(wnt_ref[...], cqt)
    qpe_t = _dot(wat_ref[...], cqt) * cost_ref[...] + _dot(wbt_ref[...], cqt) * sint_ref[...]
    for h in range(heads):
        lat_t = _dot(wkb_ref[h], nope_t[h * NOPE_DIM:(h + 1) * NOPE_DIM].astype(BF16))
        qt_ref[h, 0:KV_LORA, :] = lat_t.astype(BF16)
        qt_ref[h, KV_LORA:KV_LORA + ROPE_DIM, :] = qpe_t[h * ROPE_DIM:(h + 1) * ROPE_DIM].astype(BF16)


def _rot_half(wpe):
    half = wpe.shape[-1] // 2
    return jnp.concatenate([-wpe[..., half:], wpe[..., :half]], axis=-1)


def mla_prep(x, gain, w_in, g_q, g_kv, w_qb, w_kb, pos, tm=512):
    m, d = x.shape
    heads = C_HEADS
    tm = min(tm, m)
    half = ROPE_DIM // 2
    freqs = ROPE_THETA ** (-jnp.arange(half, dtype=F32) / half)
    ang = pos.astype(F32)[:, None] * freqs
    cos_k = jnp.tile(jnp.cos(ang), (1, 2))
    sin_k = jnp.tile(jnp.sin(ang), (1, 2))
    cos_t = jnp.tile(cos_k, (1, heads)).T
    sin_t = jnp.tile(sin_k, (1, heads)).T
    k0 = Q_LORA + KV_LORA
    win = jnp.concatenate([w_in, _rot_half(w_in[:, k0:k0 + ROPE_DIM]),
                           jnp.zeros((d, 512 - k0 - 2 * ROPE_DIM), F32)], axis=1).astype(BF16)
    wq = w_qb.reshape(Q_LORA, heads, NOPE_DIM + ROPE_DIM)
    wnt = wq[:, :, :NOPE_DIM].reshape(Q_LORA, heads * NOPE_DIM).T.astype(BF16)
    wat = wq[:, :, NOPE_DIM:].reshape(Q_LORA, heads * ROPE_DIM).T.astype(BF16)
    wbt = _rot_half(wq[:, :, NOPE_DIM:]).reshape(Q_LORA, heads * ROPE_DIM).T.astype(BF16)
    wkb = w_kb.transpose(1, 0, 2).astype(BF16)
    full = lambda a: pl.BlockSpec(a.shape, lambda i: (0,) * a.ndim)
    rows = lambda width: pl.BlockSpec((tm, width), lambda i: (i, 0))
    cols = lambda height: pl.BlockSpec((height, tm), lambda i: (0, i))
    dcat = KV_LORA + ROPE_DIM
    gq, gkv, gm = g_q.reshape(1, -1), g_kv.reshape(1, -1), gain.reshape(1, -1)
    return pl.pallas_call(
        functools.partial(_mla_prep_body, heads=heads),
        grid=(m // tm,),
        in_specs=[rows(d), full(gm), full(win), full(gq), full(gkv), full(wnt), full(wat), full(wbt), full(wkb),
                  rows(ROPE_DIM), rows(ROPE_DIM), cols(heads * ROPE_DIM), cols(heads * ROPE_DIM)],
        out_specs=[rows(KV_LORA), rows(ROPE_DIM), rows(dcat), cols(KV_LORA),
                   pl.BlockSpec((heads, dcat, tm), lambda i: (0, 0, i))],
        out_shape=[jax.ShapeDtypeStruct((m, KV_LORA), F32), jax.ShapeDtypeStruct((m, ROPE_DIM), F32),
                   jax.ShapeDtypeStruct((m, dcat), BF16), jax.ShapeDtypeStruct((KV_LORA, m), BF16),
                   jax.ShapeDtypeStruct((heads, dcat, m), BF16)],
        compiler_params=_params("parallel"),
        name="mla_prep",
    )(x, gm, win, gq, gkv, wnt, wat, wbt, wkb, cos_k, sin_k, cos_t, sin_t)


def _mla_flash_body(qt_ref, k_ref, kvt_ref, wvbt_ref, o_ref, ml_s, acc_s, ot_s, *, heads, tq):
    i = pl.program_id(1)
    krow = lax.broadcasted_iota(jnp.int32, (tq, tq), 0)
    qcol = lax.broadcasted_iota(jnp.int32, (tq, tq), 1)
    causal = krow <= qcol

    def tile_step(kstart, first):
        k = k_ref[pl.ds(kstart, tq), :]
        vt = kvt_ref[:, pl.ds(kstart, tq)]
        s_all = [_dot(k, qt_ref[h]) * MLA_LOGIT_SCALE for h in range(heads)]
        for h in range(heads):
            m_row, l_row = ml_s.at[2 * h:2 * h + 1, :], ml_s.at[2 * h + 1:2 * h + 2, :]
            if first:
                s = jnp.where(causal, s_all[h], NEG)
                m_new = jnp.max(s, axis=0, keepdims=True)
                p = jnp.exp2(s - m_new)
                acc_s[h] = _dot(vt, p.astype(BF16))
                l_row[...] = jnp.sum(p, axis=0, keepdims=True)
            else:
                s = s_all[h]
                m = m_row[...]
                m_new = jnp.maximum(m, jnp.max(s, axis=0, keepdims=True))
                alpha = jnp.exp2(m - m_new)
                p = jnp.exp2(s - m_new)
                acc_s[h] = alpha * acc_s[h] + _dot(vt, p.astype(BF16))
                l_row[...] = alpha * l_row[...] + jnp.sum(p, axis=0, keepdims=True)
            m_row[...] = m_new

    tile_step(pl.multiple_of(i * tq, tq), True)

    def body(j, carry):
        tile_step(pl.multiple_of(j * tq, tq), False)
        return carry

    lax.fori_loop(0, i, body, 0)
    for h in range(heads):
        o_lat = (acc_s[h] / ml_s[2 * h + 1:2 * h + 2, :]).astype(BF16)
        ot_s[h * V_DIM:(h + 1) * V_DIM, :] = _dot(wvbt_ref[h], o_lat)
    o_ref[...] = ot_s[...].T


def mla_flash_prompt(qt, kcat, kvt, wvbt, batch, tq=256):
    heads, dcat, m = qt.shape
    l = m // batch
    nq = l // tq
    return pl.pallas_call(
        functools.partial(_mla_flash_body, heads=heads, tq=tq),
        grid=(batch, nq),
        in_specs=[pl.BlockSpec((heads, dcat, tq), lambda b, i: (0, 0, b * nq + i)),
                  pl.BlockSpec((l, dcat), lambda b, i: (b, 0)),
                  pl.BlockSpec((KV_LORA, l), lambda b, i: (0, b)),
                  pl.BlockSpec(wvbt.shape, lambda b, i: (0, 0, 0))],
        out_specs=pl.BlockSpec((tq, heads * V_DIM), lambda b, i: (b * nq + i, 0)),
        out_shape=jax.ShapeDtypeStruct((m, heads * V_DIM), F32),
        scratch_shapes=[pltpu.VMEM((2 * heads, tq), F32), pltpu.VMEM((heads, KV_LORA, tq), F32),
                        pltpu.VMEM((heads * V_DIM, tq), F32)],
        compiler_params=_params("parallel", "arbitrary"),
        name="mla_flash_prompt",
    )(qt, kcat, kvt, wvbt)


def _mla_sample_body(pt_ref, q_ref, knew_ref, *rest, n_u, lq):
    kv_refs, pe_refs = rest[:n_u], rest[n_u:2 * n_u]
    o_ref, m_s, l_s, acc_s, kbuf, pbuf = rest[2 * n_u:]
    c = pl.program_id(1)
    q = q_ref[0]

    @pl.when(c == 0)
    def _():
        kn = knew_ref[0]
        s = _nt(q, kn) * MLA_LOGIT_SCALE
        t = lax.broadcasted_iota(jnp.int32, s.shape, 1)
        qpos = lax.broadcasted_iota(jnp.int32, s.shape, 0) % lq
        s = jnp.where(t <= qpos, s, NEG)
        m0 = jnp.max(s, axis=1, keepdims=True)
        p = jnp.exp2(s - m0)
        m_s[...] = m0
        l_s[...] = jnp.sum(p, axis=1, keepdims=True)
        acc_s[...] = _dot(p.astype(BF16), kn[:, :KV_LORA])

    for u in range(n_u):
        rows = slice(u * PAGE_SIZE, (u + 1) * PAGE_SIZE)
        kbuf[rows, :] = kv_refs[u][0].astype(BF16)
        pbuf[:, rows] = pe_refs[u][0].astype(BF16)
    s = (_nt(q[:, :KV_LORA], kbuf[...]) + _dot(q[:, KV_LORA:], pbuf[...])) * MLA_LOGIT_SCALE
    m_prev = m_s[...]
    m_new = jnp.maximum(m_prev, jnp.max(s, axis=1, keepdims=True))
    alpha = jnp.exp2(m_prev - m_new)
    p = jnp.exp2(s - m_new)
    l_s[...] = alpha * l_s[...] + jnp.sum(p, axis=1, keepdims=True)
    acc_s[...] = alpha * acc_s[...] + _dot(p.astype(BF16), kbuf[...])
    m_s[...] = m_new

    @pl.when(c == pl.num_programs(1) - 1)
    def _():
        o_ref[0] = acc_s[...] / l_s[...]


def mla_sample(qt, kcat_new, cache_kv, cache_pe, layer, page_table, lq):
    heads, dcat, m = qt.shape
    b = m // lq
    n_pages = page_table.shape[1]
    n_u = PAGES_PER_STEP
    r_n = heads * lq
    q = qt.reshape(heads, dcat, b, lq).transpose(2, 0, 3, 1).reshape(b, r_n, dcat)
    knew = jnp.pad(kcat_new.reshape(b, lq, dcat), ((0, 0), (0, PAGE_SIZE - lq), (0, 0)))
    kv_specs = [pl.BlockSpec((None, 1, PAGE_SIZE, KV_LORA), lambda bi, c, pt, u=u: (layer, pt[bi, c * n_u + u], 0, 0))
                for u in range(n_u)]
    pe_t = cache_pe.transpose(0, 1, 3, 2)
    pe_specs = [pl.BlockSpec((None, 1, ROPE_DIM, PAGE_SIZE), lambda bi, c, pt, u=u: (layer, pt[bi, c * n_u + u], 0, 0))
                for u in range(n_u)]
    o = pl.pallas_call(
        functools.partial(_mla_sample_body, n_u=n_u, lq=lq),
        grid_spec=pltpu.PrefetchScalarGridSpec(
            num_scalar_prefetch=1, grid=(b, n_pages // n_u),
            in_specs=[pl.BlockSpec((1, r_n, dcat), lambda bi, c, pt: (bi, 0, 0)),
                      pl.BlockSpec((1, PAGE_SIZE, dcat), lambda bi, c, pt: (bi, 0, 0))] + kv_specs + pe_specs,
            out_specs=pl.BlockSpec((1, r_n, KV_LORA), lambda bi, c, pt: (bi, 0, 0)),
            scratch_shapes=[pltpu.VMEM((r_n, 1), F32), pltpu.VMEM((r_n, 1), F32), pltpu.VMEM((r_n, KV_LORA), F32),
                            pltpu.VMEM((n_u * PAGE_SIZE, KV_LORA), BF16), pltpu.VMEM((ROPE_DIM, n_u * PAGE_SIZE), BF16)]),
        out_shape=jax.ShapeDtypeStruct((b, r_n, KV_LORA), F32),
        compiler_params=_params("parallel", "arbitrary"),
        name="mla_sample",
    )(page_table, q, knew, *([cache_kv] * n_u), *([pe_t] * n_u))
    return o.reshape(b, heads, lq, KV_LORA).transpose(1, 0, 2, 3).reshape(heads, m, KV_LORA)


def _headproj_body(x_ref, w_ref, o_ref, *, heads):
    for h in range(heads):
        o_ref[:, h * V_DIM:(h + 1) * V_DIM] = _dot(x_ref[h].astype(BF16), w_ref[h])


def head_value_proj(o_lat, wvb):
    heads, m, _ = o_lat.shape
    return pl.pallas_call(
        functools.partial(_headproj_body, heads=heads),
        out_shape=jax.ShapeDtypeStruct((m, heads * V_DIM), F32),
        compiler_params=pltpu.CompilerParams(vmem_limit_bytes=VMEM_LIMIT),
        name="mla_value_proj",
    )(o_lat, wvb)


def _cross_body(q_ref, k_ref, v_ref, o_ref, *, heads, dh, interleaved):
    q = q_ref[0]
    scale = LOG2E * dh ** -0.5
    ks, vs = [], []
    for h in range(heads):
        if interleaved:
            rows = pl.ds(h, k_ref.shape[1] // heads, stride=heads)
            ks.append(k_ref[0, rows, :])
            vs.append(v_ref[0, rows, :])
        else:
            ks.append(k_ref[0, :, h * dh:(h + 1) * dh])
            vs.append(v_ref[0, :, h * dh:(h + 1) * dh])
    s_all = [_nt(q[:, h * dh:(h + 1) * dh].astype(BF16), ks[h].astype(BF16)) * scale for h in range(heads)]
    for h in range(heads):
        s = s_all[h]
        m = jnp.max(s, axis=1, keepdims=True)
        p = jnp.exp2(s - m)
        l = jnp.sum(p, axis=1, keepdims=True)
        o_ref[0, :, h * dh:(h + 1) * dh] = _dot(p.astype(BF16), vs[h].astype(BF16)) / l


def cross_core(q, mk, mv, tq=512, layer=None):
    b, lq, w = q.shape
    tq = min(tq, lq)
    if layer is None:
        kv_spec = pl.BlockSpec((1,) + mk.shape[1:], lambda bi, i: (bi, 0, 0))
    else:
        kv_spec = pl.BlockSpec((None, 1) + mk.shape[2:], lambda bi, i: (layer, bi, 0, 0))
    return pl.pallas_call(
        functools.partial(_cross_body, heads=X_HEADS, dh=X_HEAD_DIM, interleaved=layer is not None),
        grid=(b, lq // tq),
        in_specs=[pl.BlockSpec((1, tq, w), lambda bi, i: (bi, i, 0)), kv_spec, kv_spec],
        out_specs=pl.BlockSpec((1, tq, w), lambda bi, i: (bi, i, 0)),
        out_shape=jax.ShapeDtypeStruct((b, lq, w), F32),
        compiler_params=_params("parallel", "parallel"),
        name="cross_core",
    )(q, mk, mv)


def _cross_fused_body(x_ref, g_ref, wq_ref, wo_ref, k_ref, v_ref, o_ref, *, heads, dh):
    x = x_ref[...]
    q = _dot(_rms(x, g_ref[...]).astype(BF16), wq_ref[...])
    scale = LOG2E * dh ** -0.5
    head = [slice(h * dh, (h + 1) * dh) for h in range(heads)]
    s_all = [_nt(q[:, sl].astype(BF16), k_ref[0, :, sl].astype(BF16)) * scale for sl in head]
    outs = []
    for h, sl in enumerate(head):
        s = s_all[h]
        m = jnp.max(s, axis=1, keepdims=True)
        p = jnp.exp2(s - m)
        l = jnp.sum(p, axis=1, keepdims=True)
        outs.append((_dot(p.astype(BF16), v_ref[0, :, sl].astype(BF16)) / l).astype(BF16))
    o_ref[...] = x + _dot(jnp.concatenate(outs, axis=1), wo_ref[...])


def cross_fused(x, gain, w_q, w_o, mk, mv, tq=512):
    m, d = x.shape
    b, n_mem, w = mk.shape
    l = m // b
    tq = min(tq, l)
    nq = l // tq
    return pl.pallas_call(
        functools.partial(_cross_fused_body, heads=X_HEADS, dh=X_HEAD_DIM),
        grid=(b, nq),
        in_specs=[pl.BlockSpec((tq, d), lambda bi, i: (bi * nq + i, 0)),
                  pl.BlockSpec((1, d), lambda bi, i: (0, 0)),
                  pl.BlockSpec(w_q.shape, lambda bi, i: (0, 0)),
                  pl.BlockSpec(w_o.shape, lambda bi, i: (0, 0)),
                  pl.BlockSpec((1, n_mem, w), lambda bi, i: (bi, 0, 0)),
                  pl.BlockSpec((1, n_mem, w), lambda bi, i: (bi, 0, 0))],
        out_specs=pl.BlockSpec((tq, d), lambda bi, i: (bi * nq + i, 0)),
        out_shape=jax.ShapeDtypeStruct((m, d), F32),
        compiler_params=_params("parallel", "parallel"),
        name="cross_fused",
    )(x, gain.reshape(1, d), w_q, w_o, mk, mv)


def _moe_gates(logits):
    lane = lax.broadcasted_iota(jnp.int32, logits.shape, 1).astype(F32)
    big = 1e9
    is_g = lane < N_GROUPS
    gl = jnp.where(is_g, logits, -jnp.inf)
    gmax = jnp.max(gl, axis=1, keepdims=True)
    grp = jnp.min(jnp.where(is_g & (gl == gmax), lane, big), axis=1, keepdims=True)
    p_grp = 1.0 / jnp.sum(jnp.where(is_g, jnp.exp(gl - gmax), 0.0), axis=1, keepdims=True)
    e_idx = lane - N_GROUPS
    in_grp = (e_idx >= grp * EXPERTS_PER_GROUP) & (e_idx < (grp + 1) * EXPERTS_PER_GROUP)
    el = jnp.where(in_grp, logits, -jnp.inf)
    t1 = jnp.max(el, axis=1, keepdims=True)
    i1 = jnp.min(jnp.where(in_grp & (el == t1), lane, big), axis=1, keepdims=True)
    el2 = jnp.where(lane == i1, -jnp.inf, el)
    t2 = jnp.max(el2, axis=1, keepdims=True)
    i2 = jnp.min(jnp.where(in_grp & (lane != i1) & (el2 == t2), lane, big), axis=1, keepdims=True)
    e2 = jnp.exp(t2 - t1)
    w1 = 1.0 / (1.0 + e2)
    w2 = e2 / (1.0 + e2)
    return p_grp * (jnp.where(lane == i1, w1, 0.0) + jnp.where(lane == i2, w2, 0.0))


def _moe_body(x_ref, g_ref, wr_ref, br_ref, wg_ref, wu_ref, wd_ref, o_ref, xn_s, gate_s, acc_s):
    e = pl.program_id(1)

    @pl.when(e == 0)
    def _():
        x = x_ref[...]
        xn = _rms(x, g_ref[...])
        xn_s[...] = xn.astype(BF16)
        gate_s[...] = _moe_gates(_dot(xn.astype(BF16), wr_ref[...]) + br_ref[...])
        acc_s[...] = x

    xn = xn_s[...]
    lane = lax.broadcasted_iota(jnp.int32, gate_s.shape, 1)
    ge = jnp.sum(jnp.where(lane == e + N_GROUPS, gate_s[...], 0.0), axis=1, keepdims=True)
    a = _dot(xn, wg_ref[0])
    u = _dot(xn, wu_ref[0])
    hid = (a * _sigmoid(a)) * u * ge
    acc_s[...] += _dot(hid.astype(BF16), wd_ref[0])

    @pl.when(e == pl.num_programs(1) - 1)
    def _():
        o_ref[...] = acc_s[...]


def moe_layer(x, gain, w_group, b_group, w_router, b_router, w_up, w_gate, w_down, layer, tm=1024):
    m, d = x.shape
    tm = min(tm, m)
    _, n_e, _, f = w_up.shape
    wr = jnp.concatenate([w_group, w_router, jnp.zeros((d, LANES - N_GROUPS - N_EXPERTS), F32)], axis=1).astype(BF16)
    br = jnp.concatenate([b_group, b_router, jnp.zeros((LANES - N_GROUPS - N_EXPERTS,), F32)]).reshape(1, LANES)
    return pl.pallas_call(
        _moe_body,
        grid=(m // tm, n_e),
        in_specs=[pl.BlockSpec((tm, d), lambda i, e: (i, 0)),
                  pl.BlockSpec((1, d), lambda i, e: (0, 0)),
                  pl.BlockSpec((d, LANES), lambda i, e: (0, 0)),
                  pl.BlockSpec((1, LANES), lambda i, e: (0, 0)),
                  pl.BlockSpec((None, 1, d, f), lambda i, e: (layer, e, 0, 0)),
                  pl.BlockSpec((None, 1, d, f), lambda i, e: (layer, e, 0, 0)),
                  pl.BlockSpec((None, 1, f, d), lambda i, e: (layer, e, 0, 0))],
        out_specs=pl.BlockSpec((tm, d), lambda i, e: (i, 0)),
        out_shape=jax.ShapeDtypeStruct((m, d), F32),
        scratch_shapes=[pltpu.VMEM((tm, d), BF16), pltpu.VMEM((tm, LANES), F32), pltpu.VMEM((tm, d), F32)],
        compiler_params=_params("parallel", "arbitrary"),
        name="moe",
    )(x, gain.reshape(1, d), wr, br, w_gate, w_up, w_down)


AB_SPLITS = [(0, A_WIDTH), (A_WIDTH, A_WIDTH), (2 * A_WIDTH, A_WIDTH), (3 * A_WIDTH, 2 * B_WIDTH),
             (3 * A_WIDTH + 2 * B_WIDTH, B_WIDTH), (3 * A_WIDTH + 3 * B_WIDTH, B_WIDTH),
             (3 * A_WIDTH + 4 * B_WIDTH, LANES)]


def _even_projection(x, gain, w_in_bf):
    return fused_linear([x], [w_in_bf], splits=AB_SPLITS, gain=gain, name="in_proj_ab")


def _pad_seq(a, lp):
    return jnp.pad(a, ((0, 0), (0, lp - a.shape[1]), (0, 0)))


def kernel(x_prompt, x_sample, mem_prompt, cache_moba_k, cache_moba_v, cache_mla_kv, cache_mla_pe, state_mlstm_c, state_mlstm_n, state_mlstm_m, state_conv, cache_mem_k, cache_mem_v, page_table, norm_mix, norm_cross, norm_mem, norm_ffn, norm_final, w_in_ab, b_gates, conv_w, norm_mlstm, w_out_ab, w_in_c, norm_q_lat, norm_kv_lat, w_qb, w_kb, w_vb, w_out_c, w_cq, w_ck, w_cv, w_co, w_group, b_group, w_router, b_router, w_up, w_gate, w_down):
    bp, lp, d = x_prompt.shape
    bs, ls, _ = x_sample.shape
    depth = norm_mix.shape[0]
    mp, ms = bp * lp, bs * ls
    n_mem = mem_prompt.shape[1]
    past = page_table.shape[1] * PAGE_SIZE
    pos_p = jnp.tile(jnp.arange(lp, dtype=jnp.int32), bp)
    pos_s = jnp.tile(past + jnp.arange(ls, dtype=jnp.int32), bs)

    xp = x_prompt.reshape(mp, d)
    xs = x_sample.reshape(ms, d)
    mem = mem_prompt.reshape(bp * n_mem, d)
    ak_p, av_p, kv_p, pe_p, mc_p, mn_p, mm_p, cv_p, mk_p, mv_p = [], [], [], [], [], [], [], [], [], []
    ak_s, av_s, kv_s, pe_s, mc_s, mn_s, mm_s, cv_s = [], [], [], [], [], [], [], []
    back = CONV_W - 1
    ch = MLSTM_CHUNK
    wu_bf, wg_bf, wd_bf = w_up.astype(BF16), w_gate.astype(BF16), w_down.astype(BF16)
    mem_k_rows = cache_mem_k.reshape(depth, bs, n_mem * X_HEADS, X_HEAD_DIM)
    mem_v_rows = cache_mem_v.reshape(depth, bs, n_mem * X_HEADS, X_HEAD_DIM)
    ls_pad = -(-ls // ch) * ch
    for l in range(depth):
        j = l // 2
        if l % 2 == 0:
            n_ab = w_in_ab.shape[2]
            n_pad = AB_SPLITS[-1][0] + LANES
            w_in_bf = jnp.pad(w_in_ab[j], ((0, 0), (0, n_pad - n_ab))).astype(BF16)
            w_out_bf = w_out_ab[j].astype(BF16)
            qa, ka, va, qkb, vb, ob, gt = _even_projection(xp, norm_mix[l], w_in_bf)
            o_a = moba_prompt(qa.reshape(bp, lp, A_WIDTH), ka.reshape(bp, lp, A_WIDTH), va.reshape(bp, lp, A_WIDTH))
            qkb3 = qkb.reshape(bp, lp, 2 * B_WIDTH)
            hb, mc, mn, mm = mlstm_layer(
                qkb3, vb.reshape(bp, lp, B_WIDTH), ob.reshape(bp, lp, B_WIDTH), gt.reshape(bp, lp, LANES),
                b_gates[j], conv_w[j], norm_mlstm[j], jnp.zeros((bp, back, 2 * B_WIDTH), F32),
                jnp.zeros((bp, B_HEADS, B_HEAD_DIM, B_HEAD_DIM), F32), jnp.zeros((bp, B_HEADS, B_HEAD_DIM), F32),
                jnp.zeros((bp, B_HEADS), F32), valid=lp)
            (xp,) = fused_linear([o_a.reshape(mp, A_WIDTH), hb.reshape(mp, B_WIDTH)],
                                 [w_out_bf[:A_WIDTH], w_out_bf[A_WIDTH:]], residual=xp, name="out_proj_ab")
            ak_p.append(ka.reshape(bp, lp, A_HEADS, A_HEAD_DIM)); av_p.append(va.reshape(bp, lp, A_HEADS, A_HEAD_DIM))
            cv_p.append(qkb3[:, lp - back:, :]); mc_p.append(mc); mn_p.append(mn); mm_p.append(mm)
            qa, ka, va, qkb, vb, ob, gt = _even_projection(xs, norm_mix[l], w_in_bf)
            o_a = moba_sample(qa.reshape(bs, ls, A_WIDTH), ka.reshape(bs, ls, A_WIDTH), va.reshape(bs, ls, A_WIDTH),
                              cache_moba_k, cache_moba_v, j, page_table)
            qkb3 = qkb.reshape(bs, ls, 2 * B_WIDTH)
            hb, mc, mn, mm = mlstm_layer(
                _pad_seq(qkb3, ls_pad), _pad_seq(vb.reshape(bs, ls, B_WIDTH), ls_pad),
                _pad_seq(ob.reshape(bs, ls, B_WIDTH), ls_pad), _pad_seq(gt.reshape(bs, ls, LANES), ls_pad),
                b_gates[j], conv_w[j], norm_mlstm[j], state_conv[j],
                state_mlstm_c[j], state_mlstm_n[j], state_mlstm_m[j], valid=ls)
            (xs,) = fused_linear([o_a.reshape(ms, A_WIDTH), hb[:, :ls].reshape(ms, B_WIDTH)],
                                 [w_out_bf[:A_WIDTH], w_out_bf[A_WIDTH:]], residual=xs, name="out_proj_ab")
            ak_s.append(ka.reshape(bs, ls, A_HEADS, A_HEAD_DIM)); av_s.append(va.reshape(bs, ls, A_HEADS, A_HEAD_DIM))
            conv_all = jnp.concatenate([state_conv[j], qkb3], axis=1)
            cv_s.append(conv_all[:, ls:, :]); mc_s.append(mc); mn_s.append(mn); mm_s.append(mm)
        else:
            wvb = w_vb[j].transpose(1, 0, 2).astype(BF16)
            w_out_bf = w_out_c[j].astype(BF16)
            ckv, kpe, kcat, kvt, qt = mla_prep(xp, norm_mix[l], w_in_c[j], norm_q_lat[j], norm_kv_lat[j], w_qb[j], w_kb[j], pos_p)
            o = mla_flash_prompt(qt, kcat, kvt, w_vb[j].transpose(1, 2, 0).astype(BF16), bp)
            (xp,) = fused_linear([o], [w_out_bf], residual=xp, name="out_proj_c")
            kv_p.append(ckv.reshape(bp, lp, KV_LORA)); pe_p.append(kpe.reshape(bp, lp, ROPE_DIM))
            ckv, kpe, kcat, _, qt = mla_prep(xs, norm_mix[l], w_in_c[j], norm_q_lat[j], norm_kv_lat[j], w_qb[j], w_kb[j], pos_s)
            o_lat = mla_sample(qt, kcat, cache_mla_kv, cache_mla_pe, j, page_table, ls)
            o = head_value_proj(o_lat, wvb)
            (xs,) = fused_linear([o], [w_out_bf], residual=xs, name="out_proj_c")
            kv_s.append(ckv.reshape(bs, ls, KV_LORA)); pe_s.append(kpe.reshape(bs, ls, ROPE_DIM))
        w_ckv = jnp.concatenate([w_ck[l], w_cv[l]], axis=1).astype(BF16)
        mk, mv = fused_linear([mem], [w_ckv], splits=[(0, X_WIDTH), (X_WIDTH, X_WIDTH)], gain=norm_mem[l], name="mem_kv")
        mk_p.append(mk.reshape(bp, n_mem, X_HEADS, X_HEAD_DIM)); mv_p.append(mv.reshape(bp, n_mem, X_HEADS, X_HEAD_DIM))
        w_cq_bf, w_co_bf = w_cq[l].astype(BF16), w_co[l].astype(BF16)
        xp = cross_fused(xp, norm_cross[l], w_cq_bf, w_co_bf, mk.reshape(bp, n_mem, X_WIDTH), mv.reshape(bp, n_mem, X_WIDTH))
        (q,) = fused_linear([xs], [w_cq_bf], gain=norm_cross[l], name="cross_q")
        q8 = _pad_seq(q.reshape(bs, ls, X_WIDTH), 8)
        o = cross_core(q8, mem_k_rows, mem_v_rows, layer=l)
        (xs,) = fused_linear([o[:, :ls].reshape(ms, X_WIDTH)], [w_co_bf], residual=xs, name="cross_out")
        xp = moe_layer(xp, norm_ffn[l], w_group[l], b_group[l], w_router[l], b_router[l], wu_bf, wg_bf, wd_bf, l)
        xs = moe_layer(xs, norm_ffn[l], w_group[l], b_group[l], w_router[l], b_router[l], wu_bf, wg_bf, wd_bf, l)
    y_prompt = rmsnorm_rows(xp, norm_final).reshape(bp, lp, d)
    y_sample = rmsnorm_rows(xs, norm_final).reshape(bs, ls, d)
    return (y_prompt, y_sample,
            jnp.stack(ak_p), jnp.stack(av_p), jnp.stack(kv_p), jnp.stack(pe_p),
            jnp.stack(mc_p), jnp.stack(mn_p), jnp.stack(mm_p), jnp.stack(cv_p), jnp.stack(mk_p), jnp.stack(mv_p),
            jnp.stack(ak_s), jnp.stack(av_s), jnp.stack(kv_s), jnp.stack(pe_s),
            jnp.stack(mc_s), jnp.stack(mn_s), jnp.stack(mm_s), jnp.stack(cv_s))
```

```python
import functools
import math

import jax
import jax.numpy as jnp
from jax import lax
from jax.experimental import pallas as pl
from jax.experimental.pallas import tpu as pltpu

F32 = jnp.float32
BF16 = jnp.bfloat16
HI = lax.Precision.HIGHEST
EPS = 1e-6
NEG = -1e30
LOG2E = math.log2(math.e)
VMEM_LIMIT = 56 * 1024 * 1024
LANES = 128

PAGE_SIZE = 128
A_HEADS, A_HEAD_DIM = 8, 64
A_WIDTH = A_HEADS * A_HEAD_DIM
MOBA_BLOCK, MOBA_TOPK = 256, 3
B_HEADS, B_HEAD_DIM = 4, 128
B_WIDTH = B_HEADS * B_HEAD_DIM
CONV_W = 4
MLSTM_CHUNK = 128
C_HEADS, Q_LORA, KV_LORA, NOPE_DIM, ROPE_DIM, V_DIM = 16, 256, 128, 64, 32, 64
ROPE_THETA = 10000.0
MLA_LOGIT_SCALE = LOG2E * (NOPE_DIM + ROPE_DIM) ** -0.5
X_HEADS, X_HEAD_DIM = 4, 128
X_WIDTH = X_HEADS * X_HEAD_DIM
N_GROUPS, EXPERTS_PER_GROUP = 4, 4
N_EXPERTS = N_GROUPS * EXPERTS_PER_GROUP
PAGES_PER_STEP = 32
MOBA_PAGES_PER_STEP = 16

NT_DIMS = (((1,), (1,)), ((), ()))
TN_DIMS = (((0,), (0,)), ((), ()))


def _params(*sem):
    return pltpu.CompilerParams(dimension_semantics=sem, vmem_limit_bytes=VMEM_LIMIT)


def _nt(a, b, precision=None):
    return lax.dot_general(a, b, NT_DIMS, precision=precision, preferred_element_type=F32)


def _dot(a, b, precision=None):
    return jnp.dot(a, b, precision=precision, preferred_element_type=F32)


def _rms(x, g):
    return x * lax.rsqrt(jnp.mean(x * x, axis=-1, keepdims=True) + EPS) * g


def _sigmoid(x):
    return 1.0 / (1.0 + jnp.exp(-x))


def _linear_body(*refs, n_in, has_gain, has_res, splits):
    x_refs, w_refs = refs[:n_in], refs[n_in:2 * n_in]
    p = 2 * n_in
    g_ref = refs[p] if has_gain else None
    p += int(has_gain)
    r_ref = refs[p] if has_res else None
    p += int(has_res)
    o_refs = refs[p:]
    xs = []
    for xr in x_refs:
        x = xr[...]
        if has_gain:
            x = _rms(x, g_ref[...])
        xs.append(x.astype(BF16))
    for (off, width), o_ref in zip(splits, o_refs):
        acc = None
        for x, wr in zip(xs, w_refs):
            y = _dot(x, wr[:, off:off + width])
            acc = y if acc is None else acc + y
        if has_res:
            acc = acc + r_ref[...]
        o_ref[...] = acc.astype(o_ref.dtype)


def fused_linear(xs, ws, splits=None, gain=None, residual=None, tm=512, name="linear"):
    m, n = xs[0].shape[0], ws[0].shape[1]
    splits = splits or [(0, n)]
    tm = min(tm, m)
    assert m % tm == 0
    in_specs = [pl.BlockSpec((tm, x.shape[1]), lambda i: (i, 0)) for x in xs]
    in_specs += [pl.BlockSpec(w.shape, lambda i: (0, 0)) for w in ws]
    args = list(xs) + list(ws)
    if gain is not None:
        in_specs.append(pl.BlockSpec((1, gain.shape[-1]), lambda i: (0, 0)))
        args.append(gain.reshape(1, -1))
    if residual is not None:
        assert len(splits) == 1
        in_specs.append(pl.BlockSpec((tm, n), lambda i: (i, 0)))
        args.append(residual)
    outs = pl.pallas_call(
        functools.partial(_linear_body, n_in=len(xs), has_gain=gain is not None, has_res=residual is not None,
                          splits=tuple(splits)),
        grid=(m // tm,),
        in_specs=in_specs,
        out_specs=[pl.BlockSpec((tm, w), lambda i: (i, 0)) for _, w in splits],
        out_shape=[jax.ShapeDtypeStruct((m, w), F32) for _, w in splits],
        compiler_params=_params("parallel"),
        name=name,
    )(*args)
    return outs


def _rmsnorm_body(x_ref, g_ref, o_ref):
    o_ref[...] = _rms(x_ref[...], g_ref[...])


def rmsnorm_rows(x, g, tm=1024):
    m, d = x.shape
    tm = min(tm, m)
    return pl.pallas_call(
        _rmsnorm_body,
        grid=(m // tm,),
        in_specs=[pl.BlockSpec((tm, d), lambda i: (i, 0)), pl.BlockSpec((1, d), lambda i: (0, 0))],
        out_specs=pl.BlockSpec((tm, d), lambda i: (i, 0)),
        out_shape=jax.ShapeDtypeStruct((m, d), F32),
        compiler_params=_params("parallel"),
        name="final_norm",
    )(x, g.reshape(1, d))


def _topk_mask(g, valid, n_iota, nb, topk):
    gm = jnp.where(valid, g, -jnp.inf)
    rank = jnp.zeros(g.shape, jnp.int32)
    for m in range(nb):
        gc = gm[:, m:m + 1]
        beats = (gc > gm) | ((gc == gm) & (m < n_iota))
        rank = rank + beats.astype(jnp.int32)
    return ((rank < topk) & valid).astype(F32)


def _topk_rows(g, valid, n_iota, nb, topk):
    gm = jnp.where(valid, g, -jnp.inf)
    rank = jnp.zeros(g.shape, jnp.int32)
    for m in range(nb):
        gr = gm[m:m + 1, :]
        beats = (gr > gm) | ((gr == gm) & (m < n_iota))
        rank = rank + beats.astype(jnp.int32)
    return ((rank < topk) & valid).astype(F32)


def _moba_prompt_body(q_ref, k_ref, v_ref, o_ref, kmean_s, kh_s, vt_s, qt_s, sel_s, ml_s, acc_s, ot_s, *,
                      nb, blk, heads, dh, topk):
    i = pl.program_id(1)

    @pl.when(i == 0)
    def _():
        for n in range(nb):
            rows = slice(n * blk, (n + 1) * blk)
            kmean_s[n:n + 1, :] = jnp.mean(k_ref[0, rows, :], axis=0, keepdims=True)
            vt_s[:, rows] = v_ref[0, rows, :].T.astype(BF16)
        for h in range(heads):
            kh_s[h] = k_ref[0, :, h * dh:(h + 1) * dh].astype(BF16)

    qt = q_ref[0].T
    krow = lax.broadcasted_iota(jnp.int32, (blk, blk), 0)
    qcol = lax.broadcasted_iota(jnp.int32, (blk, blk), 1)
    dmat = (qcol - krow).astype(F32)
    n_iota = lax.broadcasted_iota(jnp.int32, (nb, blk), 0)
    slopes = [2.0 ** (-8.0 * (h + 1) / heads) for h in range(heads)]
    head_rows = [slice(h * dh, (h + 1) * dh) for h in range(heads)]
    for h in range(heads):
        gate_t = _dot(kmean_s[:, head_rows[h]], qt[head_rows[h]], HI)
        sel_s[h] = _topk_rows(gate_t, n_iota < i, n_iota, nb, topk)
    qt_s[...] = (qt * dh ** -0.5).astype(BF16)

    def block_step(kstart, logits_fn, first):
        s_all = [_dot(kh_s[h, pl.ds(kstart, blk), :], qt_s[head_rows[h], :]) for h in range(heads)]
        for h in range(heads):
            logits = logits_fn(h, s_all[h])
            vt = vt_s[head_rows[h], pl.ds(kstart, blk)]
            m_row, l_row = ml_s.at[2 * h:2 * h + 1, :], ml_s.at[2 * h + 1:2 * h + 2, :]
            if first:
                m_new = jnp.max(logits, axis=0, keepdims=True)
                p = jnp.exp(logits - m_new)
                acc_s[h] = _dot(vt, p.astype(BF16))
                l_row[...] = jnp.sum(p, axis=0, keepdims=True)
            else:
                m = m_row[...]
                m_new = jnp.maximum(m, jnp.max(logits, axis=0, keepdims=True))
                alpha = jnp.exp(m - m_new)
                p = jnp.exp(logits - m_new)
                acc_s[h] = alpha * acc_s[h] + _dot(vt, p.astype(BF16))
                l_row[...] = alpha * l_row[...] + jnp.sum(p, axis=0, keepdims=True)
            m_row[...] = m_new

    block_step(pl.multiple_of(i * blk, blk), lambda h, s: jnp.where(dmat >= 0, s - slopes[h] * dmat, NEG), True)

    def body(j, carry):
        dist = dmat + ((i - j) * blk).astype(F32)
        pick = n_iota == j

        def logits_fn(h, s):
            selrow = jnp.sum(jnp.where(pick, sel_s[h], 0.0), axis=0, keepdims=True)
            return jnp.where(selrow > 0.5, s - slopes[h] * dist, NEG)

        block_step(pl.multiple_of(j * blk, blk), logits_fn, False)
        return carry

    lax.fori_loop(0, i, body, 0)
    for h in range(heads):
        ot_s[head_rows[h], :] = acc_s[h] / ml_s[2 * h + 1:2 * h + 2, :]
    o_ref[0] = ot_s[...].T


def moba_prompt(q, k, v):
    b, l, w = q.shape
    blk = MOBA_BLOCK
    nb = l // blk
    heads, dh = A_HEADS, A_HEAD_DIM
    return pl.pallas_call(
        functools.partial(_moba_prompt_body, nb=nb, blk=blk, heads=heads, dh=dh, topk=MOBA_TOPK),
        grid=(b, nb),
        in_specs=[pl.BlockSpec((1, blk, w), lambda bi, i: (bi, i, 0)),
                  pl.BlockSpec((1, l, w), lambda bi, i: (bi, 0, 0)),
                  pl.BlockSpec((1, l, w), lambda bi, i: (bi, 0, 0))],
        out_specs=pl.BlockSpec((1, blk, w), lambda bi, i: (bi, i, 0)),
        out_shape=jax.ShapeDtypeStruct((b, l, w), F32),
        scratch_shapes=[pltpu.VMEM((nb, w), F32), pltpu.VMEM((heads, l, dh), BF16), pltpu.VMEM((w, l), BF16),
                        pltpu.VMEM((w, blk), BF16), pltpu.VMEM((heads, nb, blk), F32),
                        pltpu.VMEM((2 * heads, blk), F32), pltpu.VMEM((heads, dh, blk), F32),
                        pltpu.VMEM((w, blk), F32)],
        compiler_params=_params("parallel", "arbitrary"),
        name="moba_prompt",
    )(q, k, v)


def _moba_scores_body(pt_ref, q_ref, *rest, n_u, heads):
    k_refs, (s_ref, ksum_ref) = rest[:n_u], rest[n_u:]
    c = pl.program_id(1)

    @pl.when(c == 0)
    def _():
        ksum_ref[...] = jnp.zeros(ksum_ref.shape, F32)

    ppb = MOBA_BLOCK // PAGE_SIZE
    dh = q_ref.shape[-1]
    lane = lax.broadcasted_iota(jnp.int32, (dh, LANES), 1)
    for h in range(heads):
        qh = q_ref[0, h].astype(BF16)
        ks = ksum_ref[0, h]
        for n in range(n_u // ppb):
            bsum = None
            for u in range(n * ppb, (n + 1) * ppb):
                kt = k_refs[u][0, h]
                s_ref[0, h, :, u * PAGE_SIZE:(u + 1) * PAGE_SIZE] = _dot(qh, kt.astype(BF16))
                bsum = kt if bsum is None else bsum + kt
            ks = jnp.where(lane == c * (n_u // ppb) + n, jnp.sum(bsum, axis=1, keepdims=True), ks)
        ksum_ref[0, h] = ks


def _moba_select_body(ksum_ref, q_ref, idx_ref, *, nb, topk, heads):
    rows = heads * 8
    lane = lax.broadcasted_iota(jnp.int32, (rows, LANES), 1)
    valid = lane < nb
    lane_f = lane.astype(F32)
    gates = [_dot(q_ref[0, h], ksum_ref[0, h] * (1.0 / MOBA_BLOCK), HI) for h in range(heads)]
    gate = jnp.where(valid, jnp.concatenate(gates, axis=0), -jnp.inf)
    rank = jnp.zeros(gate.shape, jnp.int32)
    for m in range(nb):
        gc = gate[:, m:m + 1]
        rank = rank + ((gc > gate) | ((gc == gate) & (m < lane))).astype(jnp.int32)
    out = jnp.zeros((rows, LANES), F32)
    for k in range(topk):
        idx_k = jnp.sum(jnp.where((rank == k) & valid, lane_f, 0.0), axis=1, keepdims=True)
        out = out + jnp.where(lane == k, idx_k, 0.0)
    idx_ref[0] = out.astype(jnp.int32).reshape(heads, 8, LANES)


def _moba_gather_body(pt_ref, sel_ref, s_ref, q_ref, knew_ref, vnew_ref, v_hbm, o_ref, vbuf, sem, *,
                      layer, lq, topk, heads, dh, past):
    b, h = pl.program_id(0), pl.program_id(1)
    n_h = pl.num_programs(1)
    step = b * n_h + h
    n_steps = pl.num_programs(0) * n_h
    slot = step % 2
    ppb = MOBA_BLOCK // PAGE_SIZE
    n_sel = lq * topk
    t_all = n_sel * MOBA_BLOCK

    def copies(bb, hh, sl):
        out = []
        for j in range(n_sel):
            blk = sel_ref[bb, hh * n_sel + j]
            for pg in range(ppb):
                page = pt_ref[bb, blk * ppb + pg]
                cols = pl.ds((j * ppb + pg) * PAGE_SIZE, PAGE_SIZE)
                out.append(pltpu.make_async_copy(v_hbm.at[layer, page, hh], vbuf.at[sl, :, cols], sem.at[sl]))
        return out

    @pl.when(step == 0)
    def _():
        for c in copies(b, h, slot):
            c.start()

    @pl.when(step + 1 < n_steps)
    def _():
        nxt = step + 1
        for c in copies(nxt // n_h, nxt % n_h, 1 - slot):
            c.start()

    scale = dh ** -0.5
    slope = jnp.exp2((-8.0 / heads) * (h + 1).astype(F32))
    row = lax.broadcasted_iota(jnp.int32, (8, t_all), 0)
    col = lax.broadcasted_iota(jnp.int32, (8, t_all), 1)
    choice = col // MOBA_BLOCK
    blocks = [sel_ref[b, h * n_sel + j] for j in range(n_sel)]
    s = jnp.concatenate([s_ref[0, 0, :, pl.ds(pl.multiple_of(blk * MOBA_BLOCK, MOBA_BLOCK), MOBA_BLOCK)]
                         for blk in blocks], axis=1)
    kpos = jnp.zeros((8, t_all), jnp.int32)
    for j in range(n_sel):
        kpos = jnp.where(choice == j, blocks[j] * MOBA_BLOCK, kpos)
    kpos = kpos + (col - choice * MOBA_BLOCK)
    dist = ((past + row) - kpos).astype(F32)
    logits = jnp.where((choice // topk) == row, s * scale - slope * dist, NEG)
    row1 = lax.broadcasted_iota(jnp.int32, (8, 1), 0)
    q8, knew, vnew = q_ref[0, 0], knew_ref[0, 0], vnew_ref[0, 0]
    own = []
    for t in range(lq):
        so = jnp.sum(q8 * knew[t:t + 1, :], axis=1, keepdims=True) * scale - slope * (row1 - t).astype(F32)
        own.append(jnp.where(row1 >= t, so, NEG))
    m = jnp.max(logits, axis=1, keepdims=True)
    for so in own:
        m = jnp.maximum(m, so)
    p = jnp.exp(logits - m)
    l = jnp.sum(p, axis=1, keepdims=True)

    for c in copies(b, h, slot):
        c.wait()
    acc = _nt(p.astype(BF16), vbuf[slot].astype(BF16))
    for t, so in enumerate(own):
        po = jnp.exp(so - m)
        l = l + po
        acc = acc + po * vnew[t:t + 1, :]
    o_ref[0, 0] = acc / l


def moba_sample(q, k_new, v_new, cache_k, cache_v, layer, page_table):
    b, lq, w = q.shape
    heads, dh = A_HEADS, A_HEAD_DIM
    n_pages = page_table.shape[1]
    past = n_pages * PAGE_SIZE
    nb = past // MOBA_BLOCK
    assert past % MOBA_BLOCK == 0 and MOBA_TOPK <= nb <= LANES and lq <= 8
    n_u = MOBA_PAGES_PER_STEP
    ppb = MOBA_BLOCK // PAGE_SIZE
    ck_t = cache_k.transpose(0, 1, 3, 4, 2)
    cv_t = cache_v.transpose(0, 1, 3, 4, 2)

    def per_head(a):
        a = a.reshape(b, lq, heads, dh).transpose(0, 2, 1, 3)
        return jnp.pad(a, ((0, 0), (0, 0), (0, 8 - lq), (0, 0)))

    qh = per_head(q)
    page_specs = [pl.BlockSpec((None, 1, heads, dh, PAGE_SIZE),
                               lambda bi, c, pt, u=u: (layer, pt[bi, c * n_u + u], 0, 0, 0)) for u in range(n_u)]
    scores, ksum = pl.pallas_call(
        functools.partial(_moba_scores_body, n_u=n_u, heads=heads),
        grid_spec=pltpu.PrefetchScalarGridSpec(
            num_scalar_prefetch=1, grid=(b, n_pages // n_u),
            in_specs=[pl.BlockSpec((1, heads, 8, dh), lambda bi, c, pt: (bi, 0, 0, 0))] + page_specs,
            out_specs=[pl.BlockSpec((1, heads, 8, n_u * PAGE_SIZE), lambda bi, c, pt: (bi, 0, 0, c)),
                       pl.BlockSpec((1, heads, dh, LANES), lambda bi, c, pt: (bi, 0, 0, 0))]),
        out_shape=[jax.ShapeDtypeStruct((b, heads, 8, past), F32), jax.ShapeDtypeStruct((b, heads, dh, LANES), F32)],
        compiler_params=_params("parallel", "arbitrary"),
        name="moba_sample_scores",
    )(page_table, qh, *([ck_t] * n_u))

    sel = pl.pallas_call(
        functools.partial(_moba_select_body, nb=nb, topk=MOBA_TOPK, heads=heads),
        grid=(b,),
        in_specs=[pl.BlockSpec((1, heads, dh, LANES), lambda bi: (bi, 0, 0, 0)),
                  pl.BlockSpec((1, heads, 8, dh), lambda bi: (bi, 0, 0, 0))],
        out_specs=pl.BlockSpec((1, heads, 8, LANES), lambda bi: (bi, 0, 0, 0)),
        out_shape=jax.ShapeDtypeStruct((b, heads, 8, LANES), jnp.int32),
        compiler_params=_params("parallel"),
        name="moba_sample_select",
    )(ksum, qh)
    n_sel = lq * MOBA_TOPK
    sel = sel[:, :, :lq, :MOBA_TOPK].reshape(b, heads * n_sel)

    head_spec = pl.BlockSpec((1, 1, 8, dh), lambda bi, hi, pt, sl: (bi, hi, 0, 0))
    o = pl.pallas_call(
        functools.partial(_moba_gather_body, layer=layer, lq=lq, topk=MOBA_TOPK, heads=heads, dh=dh, past=past),
        grid_spec=pltpu.PrefetchScalarGridSpec(
            num_scalar_prefetch=2, grid=(b, heads),
            in_specs=[pl.BlockSpec((1, 1, 8, past), lambda bi, hi, pt, sl: (bi, hi, 0, 0)),
                      head_spec, head_spec, head_spec, pl.BlockSpec(memory_space=pl.ANY)],
            out_specs=head_spec,
            scratch_shapes=[pltpu.VMEM((2, dh, n_sel * MOBA_BLOCK), F32), pltpu.SemaphoreType.DMA((2,))]),
        out_shape=jax.ShapeDtypeStruct((b, heads, 8, dh), F32),
        compiler_params=_params("arbitrary", "arbitrary"),
        name="moba_sample_gather",
    )(page_table, sel, scores, qh, per_head(k_new), per_head(v_new), cv_t)
    return o[:, :, :lq, :].transpose(0, 2, 1, 3).reshape(b, lq, w)


def _log_sigmoid(x):
    return -(jnp.maximum(-x, 0.0) + jnp.log1p(jnp.exp(-jnp.abs(x))))


def _mlstm_body(qk_ref, v_ref, og_ref, gcol_ref, grow_ref, bcol_ref, brow_ref, cw_ref, gh_ref, cbuf_ref,
                c0_ref, n0_ref, m0_ref, h_ref, c_out, n_out, m_out, xbuf, c_s, n_s, m_s, *, ch, valid, heads, dh):
    c = pl.program_id(1)
    width = heads * dh

    @pl.when(c == 0)
    def _():
        xbuf[0:8, :] = cbuf_ref[0]
        c_s[...] = c0_ref[0]
        n_s[...] = n0_ref[0]
        m_s[...] = m0_ref[0]

    xbuf[8:8 + ch, :] = qk_ref[0]
    cw = cw_ref[...]
    back = CONV_W - 1
    y = xbuf[8 - back:8 - back + ch, :] * cw[0:1, :]
    for t in range(1, CONV_W):
        y = y + xbuf[8 - back + t:8 - back + t + ch, :] * cw[t:t + 1, :]
    xbuf[8 - back:8, :] = xbuf[8 + ch - back:8 + ch, :]
    y = y * _sigmoid(y)

    t_col = lax.broadcasted_iota(jnp.int32, (ch, LANES), 0)
    t_row = lax.broadcasted_iota(jnp.int32, (8, ch), 1)
    gcol = gcol_ref[0] + bcol_ref[...]
    grow = grow_ref[0, 0] + brow_ref[:, :ch]
    ig_col = jnp.where(t_col < valid, gcol, NEG)
    lf_col = jnp.where(t_col < valid, _log_sigmoid(gcol), 0.0)
    ig_row = jnp.where(t_row < valid, grow, NEG)
    lf_row = jnp.where(t_row < valid, _log_sigmoid(grow), 0.0)
    ti = lax.broadcasted_iota(jnp.int32, (ch, ch), 0)
    si = lax.broadcasted_iota(jnp.int32, (ch, ch), 1)
    causal = ti >= si
    b_col = _dot(causal.astype(F32), lf_col, HI)
    b_row = _dot(lf_row, (ti <= si).astype(F32), HI)

    v = v_ref[0]
    og = og_ref[0]
    gh = gh_ref[...]
    qhs = [y[:, h * dh:(h + 1) * dh] for h in range(heads)]
    khs = [y[:, width + h * dh:width + (h + 1) * dh] * (dh ** -0.5) for h in range(heads)]
    qk_all = [_nt(qhs[h].astype(BF16), khs[h].astype(BF16)) for h in range(heads)]
    qc_all = [_dot(qhs[h].astype(BF16), c_s[h].astype(BF16)) for h in range(heads)]
    for h in range(heads):
        sl = slice(h * dh, (h + 1) * dh)
        qh, kh = qhs[h], khs[h]
        vh = v[:, sl]
        igc, bc = ig_col[:, h:h + 1], b_col[:, heads + h:heads + h + 1]
        igr, br = ig_row[h:h + 1, :], b_row[heads + h:heads + h + 1, :]
        m_prev = m_s[h][:, 0:1]
        c_prev = c_s[h]
        n_prev = n_s[h]
        log_d = jnp.where(causal, bc - br + igr, NEG)
        inter = bc + m_prev
        mt = jnp.maximum(inter, jnp.max(log_d, axis=1, keepdims=True))
        d = jnp.exp(log_d - mt)
        w_inter = jnp.exp(inter - mt)
        vb = vh.astype(BF16)
        a = qk_all[h] * d
        num = _dot(a.astype(BF16), vb) + w_inter * qc_all[h]
        den = jnp.sum(a, axis=1, keepdims=True) + w_inter * jnp.sum(qh * n_prev, axis=1, keepdims=True)
        hh = num / jnp.maximum(jnp.abs(den), jnp.exp(-mt))
        b_last = bc[ch - 1:ch, :]
        logw = b_last - bc + igc
        m_new = jnp.maximum(b_last + m_prev, jnp.max(logw, axis=0, keepdims=True))
        ws = jnp.exp(logw - m_new)
        decay = jnp.exp(b_last + m_prev - m_new)
        kw = kh * ws
        c_s[h] = decay * c_prev + lax.dot_general(kw.astype(BF16), vb, TN_DIMS, preferred_element_type=F32)
        n_s[h] = decay * n_prev + jnp.sum(kw, axis=0, keepdims=True)
        m_s[h] = jnp.broadcast_to(m_new, (1, LANES))
        hn = _rms(hh, gh[:, sl])
        h_ref[0, :, sl] = hn * _sigmoid(og[:, sl])

    @pl.when(c == pl.num_programs(1) - 1)
    def _():
        c_out[0] = c_s[...]
        n_out[0] = n_s[...]
        m_out[0] = m_s[...]


def mlstm_layer(qk, v, og, gates, b_g, conv_w, g_h, conv_buf, c0, n0, m0, valid):
    b, lp, w2 = qk.shape
    w = w2 // 2
    heads, dh, ch = B_HEADS, B_HEAD_DIM, MLSTM_CHUNK
    nc = lp // ch
    assert valid == lp or nc == 1
    grow = gates[:, :, :8].reshape(b, nc, ch, 8).transpose(0, 1, 3, 2)
    bcol = jnp.pad(b_g, (0, LANES - 8)).reshape(1, LANES)
    brow = jnp.broadcast_to(b_g[:, None], (8, LANES))
    cw = jnp.pad(conv_w, ((0, 8 - CONV_W), (0, 0)))
    cbuf = jnp.pad(conv_buf, ((0, 0), (8 - (CONV_W - 1), 0), (0, 0)))
    n0 = n0.reshape(b, heads, 1, dh)
    m0 = jnp.broadcast_to(m0[:, :, None, None], (b, heads, 1, LANES))
    full = lambda shape: pl.BlockSpec(shape, lambda bi, c: (0,) * len(shape))
    per_b = lambda shape: pl.BlockSpec(shape, lambda bi, c: (bi,) + (0,) * (len(shape) - 1))
    seq = lambda width: pl.BlockSpec((1, ch, width), lambda bi, c: (bi, c, 0))
    h, c_f, n_f, m_f = pl.pallas_call(
        functools.partial(_mlstm_body, ch=ch, valid=valid if nc == 1 else ch, heads=heads, dh=dh),
        grid=(b, nc),
        in_specs=[seq(w2), seq(w), seq(w), seq(LANES),
                  pl.BlockSpec((1, 1, 8, ch), lambda bi, c: (bi, c, 0, 0)),
                  full((1, LANES)), full((8, LANES)), full((8, w2)), full((1, w)),
                  per_b((1, 8, w2)), per_b((1, heads, dh, dh)), per_b((1, heads, 1, dh)), per_b((1, heads, 1, LANES))],
        out_specs=[seq(w), per_b((1, heads, dh, dh)), per_b((1, heads, 1, dh)), per_b((1, heads, 1, LANES))],
        out_shape=[jax.ShapeDtypeStruct((b, lp, w), F32), jax.ShapeDtypeStruct((b, heads, dh, dh), F32),
                   jax.ShapeDtypeStruct((b, heads, 1, dh), F32), jax.ShapeDtypeStruct((b, heads, 1, LANES), F32)],
        scratch_shapes=[pltpu.VMEM((8 + ch, w2), F32), pltpu.VMEM((heads, dh, dh), F32),
                        pltpu.VMEM((heads, 1, dh), F32), pltpu.VMEM((heads, 1, LANES), F32)],
        compiler_params=_params("parallel", "arbitrary"),
        name="mlstm",
    )(qk, v, og, gates, grow, bcol, brow, cw, g_h.reshape(1, w), cbuf, c0, n0, m0)
    return h, c_f, n_f.reshape(b, heads, dh), m_f[:, :, 0, 0]


def _mla_prep_body(x_ref, g_ref, win_ref, gq_ref, gkv_ref, wnt_ref, wat_ref, wbt_ref, wkb_ref,
                   cosk_ref, sink_ref, cost_ref, sint_ref,
                   ckv_ref, kpe_ref, kcat_ref, kvt_ref, qt_ref, *, heads):
    xn = _rms(x_ref[...], g_ref[...]).astype(BF16)
    y = _dot(xn, win_ref[...])
    ckv = _rms(y[:, Q_LORA:Q_LORA + KV_LORA], gkv_ref[...])
    k0 = Q_LORA + KV_LORA
    kpe = y[:, k0:k0 + ROPE_DIM] * cosk_ref[...] + y[:, k0 + ROPE_DIM:k0 + 2 * ROPE_DIM] * sink_ref[...]
    ckv_ref[...] = ckv
    kpe_ref[...] = kpe
    kcat_ref[:, 0:KV_LORA] = ckv.astype(BF16)
    kvt_ref[...] = ckv.T.astype(BF16)
    kcat_ref[:, KV_LORA:KV_LORA + ROPE_DIM] = kpe.astype(BF16)
    cqt = _rms(y[:, :Q_LORA], gq_ref[...]).T.astype(BF16)
    nope_t = _dot(wnt_ref[...], cqt)
    qpe_t = _dot(wat_ref[...], cqt) * cost_ref[...] + _dot(wbt_ref[...], cqt) * sint_ref[...]
    for h in range(heads):
        lat_t = _dot(wkb_ref[h], nope_t[h * NOPE_DIM:(h + 1) * NOPE_DIM].astype(BF16))
        qt_ref[h, 0:KV_LORA, :] = lat_t.astype(BF16)
        qt_ref[h, KV_LORA:KV_LORA + ROPE_DIM, :] = qpe_t[h * ROPE_DIM:(h + 1) * ROPE_DIM].astype(BF16)


def _rot_half(wpe):
    half = wpe.shape[-1] // 2
    return jnp.concatenate([-wpe[..., half:], wpe[..., :half]], axis=-1)


def mla_prep(x, gain, w_in, g_q, g_kv, w_qb, w_kb, pos, tm=512):
    m, d = x.shape
    heads = C_HEADS
    tm = min(tm, m)
    half = ROPE_DIM // 2
    freqs = ROPE_THETA ** (-jnp.arange(half, dtype=F32) / half)
    ang = pos.astype(F32)[:, None] * freqs
    cos_k = jnp.tile(jnp.cos(ang), (1, 2))
    sin_k = jnp.tile(jnp.sin(ang), (1, 2))
    cos_t = jnp.tile(cos_k, (1, heads)).T
    sin_t = jnp.tile(sin_k, (1, heads)).T
    k0 = Q_LORA + KV_LORA
    win = jnp.concatenate([w_in, _rot_half(w_in[:, k0:k0 + ROPE_DIM]),
                           jnp.zeros((d, 512 - k0 - 2 * ROPE_DIM), F32)], axis=1).astype(BF16)
    wq = w_qb.reshape(Q_LORA, heads, NOPE_DIM + ROPE_DIM)
    wnt = wq[:, :, :NOPE_DIM].reshape(Q_LORA, heads * NOPE_DIM).T.astype(BF16)
    wat = wq[:, :, NOPE_DIM:].reshape(Q_LORA, heads * ROPE_DIM).T.astype(BF16)
    wbt = _rot_half(wq[:, :, NOPE_DIM:]).reshape(Q_LORA, heads * ROPE_DIM).T.astype(BF16)
    wkb = w_kb.transpose(1, 0, 2).astype(BF16)
    full = lambda a: pl.BlockSpec(a.shape, lambda i: (0,) * a.ndim)
    rows = lambda width: pl.BlockSpec((tm, width), lambda i: (i, 0))
    cols = lambda height: pl.BlockSpec((height, tm), lambda i: (0, i))
    dcat = KV_LORA + ROPE_DIM
    gq, gkv, gm = g_q.reshape(1, -1), g_kv.reshape(1, -1), gain.reshape(1, -1)
    return pl.pallas_call(
        functools.partial(_mla_prep_body, heads=heads),
        grid=(m // tm,),
        in_specs=[rows(d), full(gm), full(win), full(gq), full(gkv), full(wnt), full(wat), full(wbt), full(wkb),
                  rows(ROPE_DIM), rows(ROPE_DIM), cols(heads * ROPE_DIM), cols(heads * ROPE_DIM)],
        out_specs=[rows(KV_LORA), rows(ROPE_DIM), rows(dcat), cols(KV_LORA),
                   pl.BlockSpec((heads, dcat, tm), lambda i: (0, 0, i))],
        out_shape=[jax.ShapeDtypeStruct((m, KV_LORA), F32), jax.ShapeDtypeStruct((m, ROPE_DIM), F32),
                   jax.ShapeDtypeStruct((m, dcat), BF16), jax.ShapeDtypeStruct((KV_LORA, m), BF16),
                   jax.ShapeDtypeStruct((heads, dcat, m), BF16)],
        compiler_params=_params("parallel"),
        name="mla_prep",
    )(x, gm, win, gq, gkv, wnt, wat, wbt, wkb, cos_k, sin_k, cos_t, sin_t)


def _mla_flash_body(qt_ref, k_ref, kvt_ref, wvbt_ref, o_ref, ml_s, acc_s, ot_s, *, heads, tq):
    i = pl.program_id(1)
    krow = lax.broadcasted_iota(jnp.int32, (tq, tq), 0)
    qcol = lax.broadcasted_iota(jnp.int32, (tq, tq), 1)
    causal = krow <= qcol

    def tile_step(kstart, first):
        k = k_ref[pl.ds(kstart, tq), :]
        vt = kvt_ref[:, pl.ds(kstart, tq)]
        s_all = [_dot(k, qt_ref[h]) * MLA_LOGIT_SCALE for h in range(heads)]
        for h in range(heads):
            m_row, l_row = ml_s.at[2 * h:2 * h + 1, :], ml_s.at[2 * h + 1:2 * h + 2, :]
            if first:
                s = jnp.where(causal, s_all[h], NEG)
                m_new = jnp.max(s, axis=0, keepdims=True)
                p = jnp.exp2(s - m_new)
                acc_s[h] = _dot(vt, p.astype(BF16))
                l_row[...] = jnp.sum(p, axis=0, keepdims=True)
            else:
                s = s_all[h]
                m = m_row[...]
                m_new = jnp.maximum(m, jnp.max(s, axis=0, keepdims=True))
                alpha = jnp.exp2(m - m_new)
                p = jnp.exp2(s - m_new)
                acc_s[h] = alpha * acc_s[h] + _dot(vt, p.astype(BF16))
                l_row[...] = alpha * l_row[...] + jnp.sum(p, axis=0, keepdims=True)
            m_row[...] = m_new

    tile_step(pl.multiple_of(i * tq, tq), True)

    def body(j, carry):
        tile_step(pl.multiple_of(j * tq, tq), False)
        return carry

    lax.fori_loop(0, i, body, 0)
    for h in range(heads):
        o_lat = (acc_s[h] / ml_s[2 * h + 1:2 * h + 2, :]).astype(BF16)
        ot_s[h * V_DIM:(h + 1) * V_DIM, :] = _dot(wvbt_ref[h], o_lat)
    o_ref[...] = ot_s[...].T


def mla_flash_prompt(qt, kcat, kvt, wvbt, batch, tq=256):
    heads, dcat, m = qt.shape
    l = m // batch
    nq = l // tq
    return pl.pallas_call(
        functools.partial(_mla_flash_body, heads=heads, tq=tq),
        grid=(batch, nq),
        in_specs=[pl.BlockSpec((heads, dcat, tq), lambda b, i: (0, 0, b * nq + i)),
                  pl.BlockSpec((l, dcat), lambda b, i: (b, 0)),
                  pl.BlockSpec((KV_LORA, l), lambda b, i: (0, b)),
                  pl.BlockSpec(wvbt.shape, lambda b, i: (0, 0, 0))],
        out_specs=pl.BlockSpec((tq, heads * V_DIM), lambda b, i: (b * nq + i, 0)),
        out_shape=jax.ShapeDtypeStruct((m, heads * V_DIM), F32),
        scratch_shapes=[pltpu.VMEM((2 * heads, tq), F32), pltpu.VMEM((heads, KV_LORA, tq), F32),
                        pltpu.VMEM((heads * V_DIM, tq), F32)],
        compiler_params=_params("parallel", "arbitrary"),
        name="mla_flash_prompt",
    )(qt, kcat, kvt, wvbt)


def _mla_sample_body(pt_ref, q_ref, knew_ref, *rest, n_u, lq):
    kv_refs, pe_refs = rest[:n_u], rest[n_u:2 * n_u]
    o_ref, m_s, l_s, acc_s, kbuf, pbuf = rest[2 * n_u:]
    c = pl.program_id(1)
    q = q_ref[0]

    @pl.when(c == 0)
    def _():
        kn = knew_ref[0]
        s = _nt(q, kn) * MLA_LOGIT_SCALE
        t = lax.broadcasted_iota(jnp.int32, s.shape, 1)
        qpos = lax.broadcasted_iota(jnp.int32, s.shape, 0) % lq
        s = jnp.where(t <= qpos, s, NEG)
        m0 = jnp.max(s, axis=1, keepdims=True)
        p = jnp.exp2(s - m0)
        m_s[...] = m0
        l_s[...] = jnp.sum(p, axis=1, keepdims=True)
        acc_s[...] = _dot(p.astype(BF16), kn[:, :KV_LORA])

    for u in range(n_u):
        rows = slice(u * PAGE_SIZE, (u + 1) * PAGE_SIZE)
        kbuf[rows, :] = kv_refs[u][0].astype(BF16)
        pbuf[:, rows] = pe_refs[u][0].astype(BF16)
    s = (_nt(q[:, :KV_LORA], kbuf[...]) + _dot(q[:, KV_LORA:], pbuf[...])) * MLA_LOGIT_SCALE
    m_prev = m_s[...]
    m_new = jnp.maximum(m_prev, jnp.max(s, axis=1, keepdims=True))
    alpha = jnp.exp2(m_prev - m_new)
    p = jnp.exp2(s - m_new)
    l_s[...] = alpha * l_s[...] + jnp.sum(p, axis=1, keepdims=True)
    acc_s[...] = alpha * acc_s[...] + _dot(p.astype(BF16), kbuf[...])
    m_s[...] = m_new

    @pl.when(c == pl.num_programs(1) - 1)
    def _():
        o_ref[0] = acc_s[...] / l_s[...]


def mla_sample(qt, kcat_new, cache_kv, cache_pe, layer, page_table, lq):
    heads, dcat, m = qt.shape
    b = m // lq
    n_pages = page_table.shape[1]
    n_u = PAGES_PER_STEP
    r_n = heads * lq
    q = qt.reshape(heads, dcat, b, lq).transpose(2, 0, 3, 1).reshape(b, r_n, dcat)
    knew = jnp.pad(kcat_new.reshape(b, lq, dcat), ((0, 0), (0, PAGE_SIZE - lq), (0, 0)))
    kv_specs = [pl.BlockSpec((None, 1, PAGE_SIZE, KV_LORA), lambda bi, c, pt, u=u: (layer, pt[bi, c * n_u + u], 0, 0))
                for u in range(n_u)]
    pe_t = cache_pe.transpose(0, 1, 3, 2)
    pe_specs = [pl.BlockSpec((None, 1, ROPE_DIM, PAGE_SIZE), lambda bi, c, pt, u=u: (layer, pt[bi, c * n_u + u], 0, 0))
                for u in range(n_u)]
    o = pl.pallas_call(
        functools.partial(_mla_sample_body, n_u=n_u, lq=lq),
        grid_spec=pltpu.PrefetchScalarGridSpec(
            num_scalar_prefetch=1, grid=(b, n_pages // n_u),
            in_specs=[pl.BlockSpec((1, r_n, dcat), lambda bi, c, pt: (bi, 0, 0)),
                      pl.BlockSpec((1, PAGE_SIZE, dcat), lambda bi, c, pt: (bi, 0, 0))] + kv_specs + pe_specs,
            out_specs=pl.BlockSpec((1, r_n, KV_LORA), lambda bi, c, pt: (bi, 0, 0)),
            scratch_shapes=[pltpu.VMEM((r_n, 1), F32), pltpu.VMEM((r_n, 1), F32), pltpu.VMEM((r_n, KV_LORA), F32),
                            pltpu.VMEM((n_u * PAGE_SIZE, KV_LORA), BF16), pltpu.VMEM((ROPE_DIM, n_u * PAGE_SIZE), BF16)]),
        out_shape=jax.ShapeDtypeStruct((b, r_n, KV_LORA), F32),
        compiler_params=_params("parallel", "arbitrary"),
        name="mla_sample",
    )(page_table, q, knew, *([cache_kv] * n_u), *([pe_t] * n_u))
    return o.reshape(b, heads, lq, KV_LORA).transpose(1, 0, 2, 3).reshape(heads, m, KV_LORA)


def _headproj_body(x_ref, w_ref, o_ref, *, heads):
    for h in range(heads):
        o_ref[:, h * V_DIM:(h + 1) * V_DIM] = _dot(x_ref[h].astype(BF16), w_ref[h])


def head_value_proj(o_lat, wvb):
    heads, m, _ = o_lat.shape
    return pl.pallas_call(
        functools.partial(_headproj_body, heads=heads),
        out_shape=jax.ShapeDtypeStruct((m, heads * V_DIM), F32),
        compiler_params=pltpu.CompilerParams(vmem_limit_bytes=VMEM_LIMIT),
        name="mla_value_proj",
    )(o_lat, wvb)


def _cross_body(q_ref, k_ref, v_ref, o_ref, *, heads, dh, interleaved):
    q = q_ref[0]
    scale = LOG2E * dh ** -0.5
    ks, vs = [], []
    for h in range(heads):
        if interleaved:
            rows = pl.ds(h, k_ref.shape[1] // heads, stride=heads)
            ks.append(k_ref[0, rows, :])
            vs.append(v_ref[0, rows, :])
        else:
            ks.append(k_ref[0, :, h * dh:(h + 1) * dh])
            vs.append(v_ref[0, :, h * dh:(h + 1) * dh])
    s_all = [_nt(q[:, h * dh:(h + 1) * dh].astype(BF16), ks[h].astype(BF16)) * scale for h in range(heads)]
    for h in range(heads):
        s = s_all[h]
        m = jnp.max(s, axis=1, keepdims=True)
        p = jnp.exp2(s - m)
        l = jnp.sum(p, axis=1, keepdims=True)
        o_ref[0, :, h * dh:(h + 1) * dh] = _dot(p.astype(BF16), vs[h].astype(BF16)) / l


def cross_core(q, mk, mv, tq=512, layer=None):
    b, lq, w = q.shape
    tq = min(tq, lq)
    if layer is None:
        kv_spec = pl.BlockSpec((1,) + mk.shape[1:], lambda bi, i: (bi, 0, 0))
    else:
        kv_spec = pl.BlockSpec((None, 1) + mk.shape[2:], lambda bi, i: (layer, bi, 0, 0))
    return pl.pallas_call(
        functools.partial(_cross_body, heads=X_HEADS, dh=X_HEAD_DIM, interleaved=layer is not None),
        grid=(b, lq // tq),
        in_specs=[pl.BlockSpec((1, tq, w), lambda bi, i: (bi, i, 0)), kv_spec, kv_spec],
        out_specs=pl.BlockSpec((1, tq, w), lambda bi, i: (bi, i, 0)),
        out_shape=jax.ShapeDtypeStruct((b, lq, w), F32),
        compiler_params=_params("parallel", "parallel"),
        name="cross_core",
    )(q, mk, mv)


def _cross_fused_body(x_ref, g_ref, wq_ref, wo_ref, k_ref, v_ref, o_ref, *, heads, dh):
    x = x_ref[...]
    q = _dot(_rms(x, g_ref[...]).astype(BF16), wq_ref[...])
    scale = LOG2E * dh ** -0.5
    head = [slice(h * dh, (h + 1) * dh) for h in range(heads)]
    s_all = [_nt(q[:, sl].astype(BF16), k_ref[0, :, sl].astype(BF16)) * scale for sl in head]
    outs = []
    for h, sl in enumerate(head):
        s = s_all[h]
        m = jnp.max(s, axis=1, keepdims=True)
        p = jnp.exp2(s - m)
        l = jnp.sum(p, axis=1, keepdims=True)
        outs.append((_dot(p.astype(BF16), v_ref[0, :, sl].astype(BF16)) / l).astype(BF16))
    o_ref[...] = x + _dot(jnp.concatenate(outs, axis=1), wo_ref[...])


def cross_fused(x, gain, w_q, w_o, mk, mv, tq=512):
    m, d = x.shape
    b, n_mem, w = mk.shape
    l = m // b
    tq = min(tq, l)
    nq = l // tq
    return pl.pallas_call(
        functools.partial(_cross_fused_body, heads=X_HEADS, dh=X_HEAD_DIM),
        grid=(b, nq),
        in_specs=[pl.BlockSpec((tq, d), lambda bi, i: (bi * nq + i, 0)),
                  pl.BlockSpec((1, d), lambda bi, i: (0, 0)),
                  pl.BlockSpec(w_q.shape, lambda bi, i: (0, 0)),
                  pl.BlockSpec(w_o.shape, lambda bi, i: (0, 0)),
                  pl.BlockSpec((1, n_mem, w), lambda bi, i: (bi, 0, 0)),
                  pl.BlockSpec((1, n_mem, w), lambda bi, i: (bi, 0, 0))],
        out_specs=pl.BlockSpec((tq, d), lambda bi, i: (bi * nq + i, 0)),
        out_shape=jax.ShapeDtypeStruct((m, d), F32),
        compiler_params=_params("parallel", "parallel"),
        name="cross_fused",
    )(x, gain.reshape(1, d), w_q, w_o, mk, mv)


def _moe_gates(logits):
    lane = lax.broadcasted_iota(jnp.int32, logits.shape, 1).astype(F32)
    big = 1e9
    is_g = lane < N_GROUPS
    gl = jnp.where(is_g, logits, -jnp.inf)
    gmax = jnp.max(gl, axis=1, keepdims=True)
    grp = jnp.min(jnp.where(is_g & (gl == gmax), lane, big), axis=1, keepdims=True)
    p_grp = 1.0 / jnp.sum(jnp.where(is_g, jnp.exp(gl - gmax), 0.0), axis=1, keepdims=True)
    e_idx = lane - N_GROUPS
    in_grp = (e_idx >= grp * EXPERTS_PER_GROUP) & (e_idx < (grp + 1) * EXPERTS_PER_GROUP)
    el = jnp.where(in_grp, logits, -jnp.inf)
    t1 = jnp.max(el, axis=1, keepdims=True)
    i1 = jnp.min(jnp.where(in_grp & (el == t1), lane, big), axis=1, keepdims=True)
    el2 = jnp.where(lane == i1, -jnp.inf, el)
    t2 = jnp.max(el2, axis=1, keepdims=True)
    i2 = jnp.min(jnp.where(in_grp & (lane != i1) & (el2 == t2), lane, big), axis=1, keepdims=True)
    e2 = jnp.exp(t2 - t1)
    w1 = 1.0 / (1.0 + e2)
    w2 = e2 / (1.0 + e2)
    return p_grp * (jnp.where(lane == i1, w1, 0.0) + jnp.where(lane == i2, w2, 0.0))


def _moe_body(x_ref, g_ref, wr_ref, br_ref, wg_ref, wu_ref, wd_ref, o_ref, xn_s, gate_s, acc_s):
    e = pl.program_id(1)

    @pl.when(e == 0)
    def _():
        x = x_ref[...]
        xn = _rms(x, g_ref[...])
        xn_s[...] = xn.astype(BF16)
        gate_s[...] = _moe_gates(_dot(xn, wr_ref[...], HI) + br_ref[...])
        acc_s[...] = x

    xn = xn_s[...]
    lane = lax.broadcasted_iota(jnp.int32, gate_s.shape, 1)
    ge = jnp.sum(jnp.where(lane == e + N_GROUPS, gate_s[...], 0.0), axis=1, keepdims=True)
    a = _dot(xn, wg_ref[0])
    u = _dot(xn, wu_ref[0])
    hid = (a * _sigmoid(a)) * u * ge
    acc_s[...] += _dot(hid.astype(BF16), wd_ref[0])

    @pl.when(e == pl.num_programs(1) - 1)
    def _():
        o_ref[...] = acc_s[...]


def moe_layer(x, gain, w_group, b_group, w_router, b_router, w_up, w_gate, w_down, layer, tm=1024):
    m, d = x.shape
    tm = min(tm, m)
    _, n_e, _, f = w_up.shape
    wr = jnp.concatenate([w_group, w_router, jnp.zeros((d, LANES - N_GROUPS - N_EXPERTS), F32)], axis=1)
    br = jnp.concatenate([b_group, b_router, jnp.zeros((LANES - N_GROUPS - N_EXPERTS,), F32)]).reshape(1, LANES)
    return pl.pallas_call(
        _moe_body,
        grid=(m // tm, n_e),
        in_specs=[pl.BlockSpec((tm, d), lambda i, e: (i, 0)),
                  pl.BlockSpec((1, d), lambda i, e: (0, 0)),
                  pl.BlockSpec((d, LANES), lambda i, e: (0, 0)),
                  pl.BlockSpec((1, LANES), lambda i, e: (0, 0)),
                  pl.BlockSpec((None, 1, d, f), lambda i, e: (layer, e, 0, 0)),
                  pl.BlockSpec((None, 1, d, f), lambda i, e: (layer, e, 0, 0)),
                  pl.BlockSpec((None, 1, f, d), lambda i, e: (layer, e, 0, 0))],
        out_specs=pl.BlockSpec((tm, d), lambda i, e: (i, 0)),
        out_shape=jax.ShapeDtypeStruct((m, d), F32),
        scratch_shapes=[pltpu.VMEM((tm, d), BF16), pltpu.VMEM((tm, LANES), F32), pltpu.VMEM((tm, d), F32)],
        compiler_params=_params("parallel", "arbitrary"),
        name="moe",
    )(x, gain.reshape(1, d), wr, br, w_gate, w_up, w_down)


AB_SPLITS = [(0, A_WIDTH), (A_WIDTH, A_WIDTH), (2 * A_WIDTH, A_WIDTH), (3 * A_WIDTH, 2 * B_WIDTH),
             (3 * A_WIDTH + 2 * B_WIDTH, B_WIDTH), (3 * A_WIDTH + 3 * B_WIDTH, B_WIDTH),
             (3 * A_WIDTH + 4 * B_WIDTH, LANES)]


def _even_projection(x, gain, w_in_bf):
    return fused_linear([x], [w_in_bf], splits=AB_SPLITS, gain=gain, name="in_proj_ab")


def _pad_seq(a, lp):
    return jnp.pad(a, ((0, 0), (0, lp - a.shape[1]), (0, 0)))


def kernel(x_prompt, x_sample, mem_prompt, cache_moba_k, cache_moba_v, cache_mla_kv, cache_mla_pe, state_mlstm_c, state_mlstm_n, state_mlstm_m, state_conv, cache_mem_k, cache_mem_v, page_table, norm_mix, norm_cross, norm_mem, norm_ffn, norm_final, w_in_ab, b_gates, conv_w, norm_mlstm, w_out_ab, w_in_c, norm_q_lat, norm_kv_lat, w_qb, w_kb, w_vb, w_out_c, w_cq, w_ck, w_cv, w_co, w_group, b_group, w_router, b_router, w_up, w_gate, w_down):
    bp, lp, d = x_prompt.shape
    bs, ls, _ = x_sample.shape
    depth = norm_mix.shape[0]
    mp, ms = bp * lp, bs * ls
    n_mem = mem_prompt.shape[1]
    past = page_table.shape[1] * PAGE_SIZE
    pos_p = jnp.tile(jnp.arange(lp, dtype=jnp.int32), bp)
    pos_s = jnp.tile(past + jnp.arange(ls, dtype=jnp.int32), bs)

    xp = x_prompt.reshape(mp, d)
    xs = x_sample.reshape(ms, d)
    mem = mem_prompt.reshape(bp * n_mem, d)
    ak_p, av_p, kv_p, pe_p, mc_p, mn_p, mm_p, cv_p, mk_p, mv_p = [], [], [], [], [], [], [], [], [], []
    ak_s, av_s, kv_s, pe_s, mc_s, mn_s, mm_s, cv_s = [], [], [], [], [], [], [], []
    back = CONV_W - 1
    ch = MLSTM_CHUNK
    wu_bf, wg_bf, wd_bf = w_up.astype(BF16), w_gate.astype(BF16), w_down.astype(BF16)
    mem_k_rows = cache_mem_k.reshape(depth, bs, n_mem * X_HEADS, X_HEAD_DIM)
    mem_v_rows = cache_mem_v.reshape(depth, bs, n_mem * X_HEADS, X_HEAD_DIM)
    ls_pad = -(-ls // ch) * ch
    for l in range(depth):
        j = l // 2
        if l % 2 == 0:
            n_ab = w_in_ab.shape[2]
            n_pad = AB_SPLITS[-1][0] + LANES
            w_in_bf = jnp.pad(w_in_ab[j], ((0, 0), (0, n_pad - n_ab))).astype(BF16)
            w_out_bf = w_out_ab[j].astype(BF16)
            qa, ka, va, qkb, vb, ob, gt = _even_projection(xp, norm_mix[l], w_in_bf)
            o_a = moba_prompt(qa.reshape(bp, lp, A_WIDTH), ka.reshape(bp, lp, A_WIDTH), va.reshape(bp, lp, A_WIDTH))
            qkb3 = qkb.reshape(bp, lp, 2 * B_WIDTH)
            hb, mc, mn, mm = mlstm_layer(
                qkb3, vb.reshape(bp, lp, B_WIDTH), ob.reshape(bp, lp, B_WIDTH), gt.reshape(bp, lp, LANES),
                b_gates[j], conv_w[j], norm_mlstm[j], jnp.zeros((bp, back, 2 * B_WIDTH), F32),
                jnp.zeros((bp, B_HEADS, B_HEAD_DIM, B_HEAD_DIM), F32), jnp.zeros((bp, B_HEADS, B_HEAD_DIM), F32),
                jnp.zeros((bp, B_HEADS), F32), valid=lp)
            (xp,) = fused_linear([o_a.reshape(mp, A_WIDTH), hb.reshape(mp, B_WIDTH)],
                                 [w_out_bf[:A_WIDTH], w_out_bf[A_WIDTH:]], residual=xp, name="out_proj_ab")
            ak_p.append(ka.reshape(bp, lp, A_HEADS, A_HEAD_DIM)); av_p.append(va.reshape(bp, lp, A_HEADS, A_HEAD_DIM))
            cv_p.append(qkb3[:, lp - back:, :]); mc_p.append(mc); mn_p.append(mn); mm_p.append(mm)
            qa, ka, va, qkb, vb, ob, gt = _even_projection(xs, norm_mix[l], w_in_bf)
            o_a = moba_sample(qa.reshape(bs, ls, A_WIDTH), ka.reshape(bs, ls, A_WIDTH), va.reshape(bs, ls, A_WIDTH),
                              cache_moba_k, cache_moba_v, j, page_table)
            qkb3 = qkb.reshape(bs, ls, 2 * B_WIDTH)
            hb, mc, mn, mm = mlstm_layer(
                _pad_seq(qkb3, ls_pad), _pad_seq(vb.reshape(bs, ls, B_WIDTH), ls_pad),
                _pad_seq(ob.reshape(bs, ls, B_WIDTH), ls_pad), _pad_seq(gt.reshape(bs, ls, LANES), ls_pad),
                b_gates[j], conv_w[j], norm_mlstm[j], state_conv[j],
                state_mlstm_c[j], state_mlstm_n[j], state_mlstm_m[j], valid=ls)
            (xs,) = fused_linear([o_a.reshape(ms, A_WIDTH), hb[:, :ls].reshape(ms, B_WIDTH)],
                                 [w_out_bf[:A_WIDTH], w_out_bf[A_WIDTH:]], residual=xs, name="out_proj_ab")
            ak_s.append(ka.reshape(bs, ls, A_HEADS, A_HEAD_DIM)); av_s.append(va.reshape(bs, ls, A_HEADS, A_HEAD_DIM))
            conv_all = jnp.concatenate([state_conv[j], qkb3], axis=1)
            cv_s.append(conv_all[:, ls:, :]); mc_s.append(mc); mn_s.append(mn); mm_s.append(mm)
        else:
            wvb = w_vb[j].transpose(1, 0, 2).astype(BF16)
            w_out_bf = w_out_c[j].astype(BF16)
            ckv, kpe, kcat, kvt, qt = mla_prep(xp, norm_mix[l], w_in_c[j], norm_q_lat[j], norm_kv_lat[j], w_qb[j], w_kb[j], pos_p)
            o = mla_flash_prompt(qt, kcat, kvt, w_vb[j].transpose(1, 2, 0).astype(BF16), bp)
            (xp,) = fused_linear([o], [w_out_bf], residual=xp, name="out_proj_c")
            kv_p.append(ckv.reshape(bp, lp, KV_LORA)); pe_p.append(kpe.reshape(bp, lp, ROPE_DIM))
            ckv, kpe, kcat, _, qt = mla_prep(xs, norm_mix[l], w_in_c[j], norm_q_lat[j], norm_kv_lat[j], w_qb[j], w_kb[j], pos_s)
            o_lat = mla_sample(qt, kcat, cache_mla_kv, cache_mla_pe, j, page_table, ls)
            o = head_value_proj(o_lat, wvb)
            (xs,) = fused_linear([o], [w_out_bf], residual=xs, name="out_proj_c")
            kv_s.append(ckv.reshape(bs, ls, KV_LORA)); pe_s.append(kpe.reshape(bs, ls, ROPE_DIM))
        w_ckv = jnp.concatenate([w_ck[l], w_cv[l]], axis=1).astype(BF16)
        mk, mv = fused_linear([mem], [w_ckv], splits=[(0, X_WIDTH), (X_WIDTH, X_WIDTH)], gain=norm_mem[l], name="mem_kv")
        mk_p.append(mk.reshape(bp, n_mem, X_HEADS, X_HEAD_DIM)); mv_p.append(mv.reshape(bp, n_mem, X_HEADS, X_HEAD_DIM))
        w_cq_bf, w_co_bf = w_cq[l].astype(BF16), w_co[l].astype(BF16)
        xp = cross_fused(xp, norm_cross[l], w_cq_bf, w_co_bf, mk.reshape(bp, n_mem, X_WIDTH), mv.reshape(bp, n_mem, X_WIDTH))
        (q,) = fused_linear([xs], [w_cq_bf], gain=norm_cross[l], name="cross_q")
        q8 = _pad_seq(q.reshape(bs, ls, X_WIDTH), 8)
        o = cross_core(q8, mem_k_rows, mem_v_rows, layer=l)
        (xs,) = fused_linear([o[:, :ls].reshape(ms, X_WIDTH)], [w_co_bf], residual=xs, name="cross_out")
        xp = moe_layer(xp, norm_ffn[l], w_group[l], b_group[l], w_router[l], b_router[l], wu_bf, wg_bf, wd_bf, l)
        xs = moe_layer(xs, norm_ffn[l], w_group[l], b_group[l], w_router[l], b_router[l], wu_bf, wg_bf, wd_bf, l)
    y_prompt = rmsnorm_rows(xp, norm_final).reshape(bp, lp, d)
    y_sample = rmsnorm_rows(xs, norm_final).reshape(bs, ls, d)
    return (y_prompt, y_sample,
            jnp.stack(ak_p), jnp.stack(av_p), jnp.stack(kv_p), jnp.stack(pe_p),
            jnp.stack(mc_p), jnp.stack(mn_p), jnp.stack(mm_p), jnp.stack(cv_p), jnp.stack(mk_p), jnp.stack(mv_p),
            jnp.stack(ak_s), jnp.stack(av_s), jnp.stack(kv_s), jnp.stack(pe_s),
            jnp.stack(mc_s), jnp.stack(mn_s), jnp.stack(mm_s), jnp.stack(cv_s))
```
